```python
import math
import jax, jax.numpy as jnp
from jax import lax
import numpy as np

D_MODEL = 1024
BATCH = 2
SEQ = 8192
DEPTH = 4
DEC_BATCH = 8
DEC_SEQ = 4096
PAST_LEN = 128

HEAD_DIM = 64
N_Q_HEADS = 8
N_KV_HEADS = 2
ATTN_WIDTH = N_Q_HEADS * HEAD_DIM
KV_WIDTH = N_KV_HEADS * HEAD_DIM
Q_BLOCK = 128
ROPE_THETA = 10000.0
GRID_W = 64
N_GMLP_HEADS = 8
GMLP_HEAD_DIM = 64
GMLP_WIDTH = N_GMLP_HEADS * GMLP_HEAD_DIM
CHUNK = 128
MIX_WIDTH = ATTN_WIDTH + GMLP_WIDTH
IN_WIDTH = ATTN_WIDTH + 2 * KV_WIDTH + 2 * GMLP_WIDTH
N_EXPERTS = 16
N_EXPERT_GROUPS = 4
EXPERTS_PER_GROUP = N_EXPERTS // N_EXPERT_GROUPS
TOP_K = 2
EXPERT_FF = 512
EPS = 1e-6
DEEPNORM_ALPHA = float((2 * DEPTH) ** 0.25)
DEEPNORM_BETA = float((8 * DEPTH) ** -0.25)

kernel_name = "hybrid_gqa_gmlp_moe_encoder"

F32 = jnp.float32


def layer_norm(x, g, b):
    xf = x.astype(F32)
    mu = jnp.mean(xf, -1, keepdims=True)
    var = jnp.mean(jnp.square(xf - mu), -1, keepdims=True)
    return ((xf - mu) * lax.rsqrt(var + EPS) * g.astype(F32) + b.astype(F32)).astype(x.dtype)


def rms_norm(x, g):
    xf = x.astype(F32)
    return (xf * lax.rsqrt(jnp.mean(xf * xf, -1, keepdims=True) + EPS) * g.astype(F32)).astype(x.dtype)


def axial_rope_tables(seq_len):
    rows = seq_len // GRID_W
    row = jnp.repeat(jnp.arange(rows, dtype=F32), GRID_W)
    col = jnp.tile(jnp.arange(GRID_W, dtype=F32), rows)
    n_pairs_axis = HEAD_DIM // 4
    inv_freq = 1.0 / (ROPE_THETA ** (jnp.arange(n_pairs_axis, dtype=F32) / n_pairs_axis))
    ang = jnp.concatenate([row[:, None] * inv_freq, col[:, None] * inv_freq], -1)
    return jnp.cos(ang), jnp.sin(ang)


def apply_rope(x, cos, sin):
    b, s, h, d = x.shape
    xp = x.astype(F32).reshape(b, s, h, d // 2, 2)
    x0, x1 = xp[..., 0], xp[..., 1]
    c = cos[None, :, None, :]
    sn = sin[None, :, None, :]
    out = jnp.stack([x0 * c - x1 * sn, x0 * sn + x1 * c], -1)
    return out.reshape(b, s, h, d).astype(x.dtype)


def gqa_attention(q, k, v):
    b, s, _, d = q.shape
    n_blk = s // Q_BLOCK
    g = N_Q_HEADS // N_KV_HEADS
    qb = q.reshape(b, n_blk, Q_BLOCK, N_KV_HEADS, g, d).transpose(1, 0, 2, 3, 4, 5)
    scale = HEAD_DIM ** -0.5

    def block(q_blk):
        sc = jnp.einsum('bqkgd,bskd->bkgqs', q_blk, k).astype(F32) * scale
        p = jax.nn.softmax(sc, axis=-1).astype(v.dtype)
        return jnp.einsum('bkgqs,bskd->bqkgd', p, v)

    out = lax.map(block, qb)
    return out.transpose(1, 0, 2, 3, 4, 5).reshape(b, s, N_Q_HEADS * d)


def spatial_gating(u, vg, ln_g, ln_b, w_s, b_s):
    b, s, _ = u.shape
    u = jax.nn.gelu(u)
    vg = layer_norm(jax.nn.gelu(vg), ln_g, ln_b)
    n_chunk = s // CHUNK
    vc = vg.reshape(b, n_chunk, CHUNK, N_GMLP_HEADS, GMLP_HEAD_DIM)
    mixed = jnp.einsum('hpq,bcqhd->bcphd', w_s, vc) + b_s.T[None, None, :, :, None]
    return u * mixed.reshape(b, s, GMLP_WIDTH)


def route(xf, w_router, router_bias):
    scores = jax.nn.sigmoid((xf @ w_router).astype(F32))
    biased = scores + router_bias.astype(F32)
    grouped = biased.reshape(-1, N_EXPERT_GROUPS, EXPERTS_PER_GROUP)
    group_score = jnp.sum(lax.top_k(grouped, TOP_K)[0], -1)
    best_group = jnp.argmax(group_score, -1)
    expert_group = jnp.arange(N_EXPERTS) // EXPERTS_PER_GROUP
    in_group = expert_group[None, :] == best_group[:, None]
    masked = jnp.where(in_group, biased, -jnp.inf)
    _, idx = lax.top_k(masked, TOP_K)
    w = jnp.take_along_axis(scores, idx, -1)
    w = w / jnp.sum(w, -1, keepdims=True)
    return jnp.sum(jax.nn.one_hot(idx, N_EXPERTS, dtype=F32) * w[..., None], axis=1)


def moe(x, w_router, router_bias, w_gate, w_up, w_down):
    b, s, d = x.shape
    xf = x.reshape(-1, d)
    gates = route(xf, w_router, router_bias).astype(x.dtype)
    y = jnp.zeros_like(xf)
    for e in range(N_EXPERTS):
        h = jax.nn.silu(xf @ w_gate[e]) * (xf @ w_up[e])
        y = y + gates[:, e:e + 1] * (h @ w_down[e])
    return y.reshape(b, s, d)


def encoder_layer(x, cos, sin, w_in, q_norm, k_norm, gmlp_ln_g, gmlp_ln_b, w_spatial, b_spatial,
                  attn_out_norm, gmlp_out_norm, w_out, ln1_g, ln1_b, w_router, router_bias,
                  w_gate, w_up, w_down, ln2_g, ln2_b):
    b, s, d = x.shape
    h = x @ w_in
    splits = [ATTN_WIDTH, ATTN_WIDTH + KV_WIDTH, ATTN_WIDTH + 2 * KV_WIDTH,
              ATTN_WIDTH + 2 * KV_WIDTH + GMLP_WIDTH]
    q, k, v, gu, gv = jnp.split(h, splits, axis=-1)
    q = apply_rope(rms_norm(q.reshape(b, s, N_Q_HEADS, HEAD_DIM), q_norm), cos, sin)
    k = apply_rope(rms_norm(k.reshape(b, s, N_KV_HEADS, HEAD_DIM), k_norm), cos, sin)
    v = v.reshape(b, s, N_KV_HEADS, HEAD_DIM)
    attn = rms_norm(gqa_attention(q, k, v), attn_out_norm)
    gm = rms_norm(spatial_gating(gu, gv, gmlp_ln_g, gmlp_ln_b, w_spatial, b_spatial), gmlp_out_norm)
    mix = jnp.concatenate([attn, gm], axis=-1) @ w_out
    x = layer_norm(DEEPNORM_ALPHA * x + mix, ln1_g, ln1_b)
    x = layer_norm(DEEPNORM_ALPHA * x + moe(x, w_router, router_bias, w_gate, w_up, w_down), ln2_g, ln2_b)
    return x


def trunk(x, w_in, q_norm, k_norm, gmlp_ln_g, gmlp_ln_b, w_spatial, b_spatial, attn_out_norm,
          gmlp_out_norm, w_out, ln1_g, ln1_b, w_router, router_bias, w_gate, w_up, w_down,
          ln2_g, ln2_b):
    cos, sin = axial_rope_tables(x.shape[1])
    for l in range(DEPTH):
        x = encoder_layer(x, cos, sin, w_in[l], q_norm[l], k_norm[l], gmlp_ln_g[l], gmlp_ln_b[l],
                          w_spatial[l], b_spatial[l], attn_out_norm[l], gmlp_out_norm[l], w_out[l],
                          ln1_g[l], ln1_b[l], w_router, router_bias, w_gate[l], w_up[l], w_down[l],
                          ln2_g[l], ln2_b[l])
    return x


def setup_inputs(seed: int = 0) -> dict:
    key = jax.random.key(seed)
    ks = jax.random.split(key, 24)
    nrm = lambda k, shape: jax.random.normal(k, shape, F32)
    gain = lambda k, shape: 1.0 + 0.02 * nrm(k, shape)
    small = lambda k, shape: 0.02 * nrm(k, shape)
    return {
        "x_prompt": nrm(ks[0], (BATCH, SEQ, D_MODEL)),
        "x_sample": nrm(ks[1], (DEC_BATCH, DEC_SEQ, D_MODEL)),
        "w_in": nrm(ks[2], (DEPTH, D_MODEL, IN_WIDTH)) * D_MODEL ** -0.5,
        "q_norm": gain(ks[3], (DEPTH, HEAD_DIM)),
        "k_norm": gain(ks[4], (DEPTH, HEAD_DIM)),
        "gmlp_ln_g": gain(ks[5], (DEPTH, GMLP_WIDTH)),
        "gmlp_ln_b": small(ks[6], (DEPTH, GMLP_WIDTH)),
        "w_spatial": nrm(ks[7], (DEPTH, N_GMLP_HEADS, CHUNK, CHUNK)) * CHUNK ** -0.5,
        "b_spatial": 1.0 + 0.1 * nrm(ks[8], (DEPTH, N_GMLP_HEADS, CHUNK)),
        "attn_out_norm": gain(ks[9], (DEPTH, ATTN_WIDTH)),
        "gmlp_out_norm": gain(ks[10], (DEPTH, GMLP_WIDTH)),
        "w_out": nrm(ks[11], (DEPTH, MIX_WIDTH, D_MODEL)) * MIX_WIDTH ** -0.5 * DEEPNORM_BETA,
        "ln1_g": gain(ks[12], (DEPTH, D_MODEL)),
        "ln1_b": small(ks[13], (DEPTH, D_MODEL)),
        "w_router": nrm(ks[14], (D_MODEL, N_EXPERTS)) * D_MODEL ** -0.5,
        "router_bias": 0.01 * nrm(ks[15], (N_EXPERTS,)),
        "w_gate": nrm(ks[16], (DEPTH, N_EXPERTS, D_MODEL, EXPERT_FF)) * D_MODEL ** -0.5,
        "w_up": nrm(ks[17], (DEPTH, N_EXPERTS, D_MODEL, EXPERT_FF)) * D_MODEL ** -0.5,
        "w_down": nrm(ks[18], (DEPTH, N_EXPERTS, EXPERT_FF, D_MODEL)) * EXPERT_FF ** -0.5 * DEEPNORM_BETA,
        "ln2_g": gain(ks[19], (DEPTH, D_MODEL)),
        "ln2_b": small(ks[20], (DEPTH, D_MODEL)),
    }


def reference(x_prompt, x_sample, w_in, q_norm, k_norm, gmlp_ln_g, gmlp_ln_b, w_spatial, b_spatial,
              attn_out_norm, gmlp_out_norm, w_out, ln1_g, ln1_b, w_router, router_bias,
              w_gate, w_up, w_down, ln2_g, ln2_b):
    y_prompt = trunk(x_prompt, w_in, q_norm, k_norm, gmlp_ln_g, gmlp_ln_b, w_spatial, b_spatial,
                     attn_out_norm, gmlp_out_norm, w_out, ln1_g, ln1_b, w_router, router_bias,
                     w_gate, w_up, w_down, ln2_g, ln2_b)
    y_sample = trunk(x_sample, w_in, q_norm, k_norm, gmlp_ln_g, gmlp_ln_b, w_spatial, b_spatial,
                     attn_out_norm, gmlp_out_norm, w_out, ln1_g, ln1_b, w_router, router_bias,
                     w_gate, w_up, w_down, ln2_g, ln2_b)
    return (y_prompt, y_sample)
```

```python
import functools

import jax
import jax.numpy as jnp
from jax import lax
from jax.experimental import pallas as pl
from jax.experimental.pallas import tpu as pltpu

D_MODEL = 1024
DEPTH = 4
HEAD_DIM = 64
N_Q_HEADS = 8
N_KV_HEADS = 2
ATTN_WIDTH = N_Q_HEADS * HEAD_DIM
KV_WIDTH = N_KV_HEADS * HEAD_DIM
GMLP_WIDTH = 512
N_GMLP_HEADS = 8
GMLP_HEAD_DIM = 64
CHUNK = 128
IN_WIDTH = ATTN_WIDTH + 2 * KV_WIDTH + 2 * GMLP_WIDTH
N_EXPERTS = 16
EXPERTS_PER_GROUP = 4
EXPERT_FF = 512
ROPE_THETA = 10000.0
GRID_W = 64
EPS = 1e-6
DEEPNORM_ALPHA = float((2 * DEPTH) ** 0.25)

LANES = 128
F32 = jnp.float32
BF16 = jnp.bfloat16
VMEM_LIMIT_BYTES = 48 * 1024 * 1024


def _compiler_params(n_grid_dims):
    return pltpu.CompilerParams(
        dimension_semantics=("arbitrary",) * n_grid_dims,
        vmem_limit_bytes=VMEM_LIMIT_BYTES,
    )


def _dot(a, b):
    return jnp.dot(a, b, preferred_element_type=F32)


def _layer_norm(x, g, b):
    mu = jnp.mean(x, axis=-1, keepdims=True)
    xc = x - mu
    var = jnp.mean(xc * xc, axis=-1, keepdims=True)
    return xc * lax.rsqrt(var + EPS) * g + b


def _rms_norm(x, g):
    return x * lax.rsqrt(jnp.mean(x * x, axis=-1, keepdims=True) + EPS) * g


def _head_rms_norm(x, seg, g):
    ssq = _dot((x * x).astype(BF16), seg)
    return x * lax.rsqrt(ssq * (1.0 / HEAD_DIM) + EPS) * g


def _rope_half_split(x, cos, sin_signed):
    first_half = (lax.broadcasted_iota(jnp.int32, (1, LANES), 1) % HEAD_DIM) < (HEAD_DIM // 2)
    cols = []
    for c in range(x.shape[1] // LANES):
        xc = x[:, c * LANES:(c + 1) * LANES]
        partner = jnp.where(first_half,
                            pltpu.roll(xc, LANES - HEAD_DIM // 2, 1),
                            pltpu.roll(xc, HEAD_DIM // 2, 1))
        cols.append(xc * cos + partner * sin_signed)
    return cols[0] if len(cols) == 1 else jnp.concatenate(cols, axis=1)


def _proj_kernel(l_ref, x_ref, w_in_ref, cos_ref, sin_ref, seg_ref, qn_ref, kn_ref,
                 lng_ref, lnb_ref, wcat_ref, bsp_ref, gon_ref,
                 q_ref, k_ref, vt_ref, gm_ref):
    del l_ref
    tt = x_ref.shape[0]
    h = _dot(x_ref[...].astype(BF16), w_in_ref[...])
    cos = cos_ref[...]
    sin_signed = sin_ref[...]

    q = _head_rms_norm(h[:, :ATTN_WIDTH], seg_ref[...], qn_ref[...])
    q = _rope_half_split(q, cos, sin_signed) * (HEAD_DIM ** -0.5)
    q_ref[...] = q.astype(BF16)

    k0 = ATTN_WIDTH
    k = _head_rms_norm(h[:, k0:k0 + KV_WIDTH], seg_ref[:KV_WIDTH, :KV_WIDTH], kn_ref[...])
    k = _rope_half_split(k, cos, sin_signed).astype(BF16)
    for hk in range(N_KV_HEADS):
        k_ref[hk] = k[:, hk * HEAD_DIM:(hk + 1) * HEAD_DIM]

    v0 = k0 + KV_WIDTH
    vt_ref[...] = h[:, v0:v0 + KV_WIDTH].T.astype(BF16)

    u0 = v0 + KV_WIDTH
    u = jax.nn.gelu(h[:, u0:u0 + GMLP_WIDTH])
    vg = _layer_norm(jax.nn.gelu(h[:, u0 + GMLP_WIDTH:]), lng_ref[...], lnb_ref[...])
    lane_head = lax.broadcasted_iota(jnp.int32, (1, GMLP_WIDTH), 1) // GMLP_HEAD_DIM
    mixed = []
    for c in range(tt // CHUNK):
        vc = vg[c * CHUNK:(c + 1) * CHUNK].astype(BF16)
        stack = jnp.concatenate(
            [jnp.where(lane_head == hh, vc, jnp.zeros_like(vc)) for hh in range(N_GMLP_HEADS)],
            axis=0)
        mixed.append(_dot(wcat_ref[...], stack) + bsp_ref[...])
    mixed = mixed[0] if len(mixed) == 1 else jnp.concatenate(mixed, axis=0)
    gm_ref[...] = _rms_norm(u * mixed, gon_ref[...]).astype(BF16)


def _proj_call(layer, x, p, tt, rope_period_blocks):
    n = x.shape[0]
    n_prompt_blocks, prompt_period, sample_period = rope_period_blocks

    def tok(i, l):
        return (i, 0)

    def rope_idx(i, l):
        return (jnp.where(i < n_prompt_blocks, i % prompt_period, i % sample_period), 0)

    def lay3(i, l):
        return (l[0], 0, 0)

    def const2(i, l):
        return (0, 0)

    grid_spec = pltpu.PrefetchScalarGridSpec(
        num_scalar_prefetch=1,
        grid=(n // tt,),
        in_specs=[
            pl.BlockSpec((tt, D_MODEL), tok),
            pl.BlockSpec((None, D_MODEL, IN_WIDTH), lay3),
            pl.BlockSpec((tt, LANES), rope_idx),
            pl.BlockSpec((tt, LANES), rope_idx),
            pl.BlockSpec((ATTN_WIDTH, ATTN_WIDTH), const2),
            pl.BlockSpec((None, 1, ATTN_WIDTH), lay3),
            pl.BlockSpec((None, 1, KV_WIDTH), lay3),
            pl.BlockSpec((None, 1, GMLP_WIDTH), lay3),
            pl.BlockSpec((None, 1, GMLP_WIDTH), lay3),
            pl.BlockSpec((None, CHUNK, N_GMLP_HEADS * CHUNK), lay3),
            pl.BlockSpec((None, CHUNK, GMLP_WIDTH), lay3),
            pl.BlockSpec((None, 1, GMLP_WIDTH), lay3),
        ],
        out_specs=[
            pl.BlockSpec((tt, ATTN_WIDTH), tok),
            pl.BlockSpec((N_KV_HEADS, tt, HEAD_DIM), lambda i, l: (0, i, 0)),
            pl.BlockSpec((KV_WIDTH, tt), lambda i, l: (0, i)),
            pl.BlockSpec((tt, GMLP_WIDTH), tok),
        ],
    )
    return pl.pallas_call(
        _proj_kernel,
        grid_spec=grid_spec,
        out_shape=[
            jax.ShapeDtypeStruct((n, ATTN_WIDTH), BF16),
            jax.ShapeDtypeStruct((N_KV_HEADS, n, HEAD_DIM), BF16),
            jax.ShapeDtypeStruct((KV_WIDTH, n), BF16),
            jax.ShapeDtypeStruct((n, GMLP_WIDTH), BF16),
        ],
        compiler_params=_compiler_params(1),
        name="proj",
    )(layer, x, p["w_in"], p["cos"], p["sin"], p["seg"], p["q_norm"], p["k_norm"],
      p["gmlp_ln_g"], p["gmlp_ln_b"], p["w_cat"], p["b_sp"], p["gmlp_out_norm"])


def _attn_kernel(q_ref, k_ref, vt_ref, o_ref, *, tk):
    tq = q_ref.shape[0]
    s_len = k_ref.shape[0]
    heads = q_ref.shape[1] // HEAD_DIM
    q_all = q_ref[...]
    qs = [q_all[:, g * HEAD_DIM:(g + 1) * HEAD_DIM] for g in range(heads)]

    def body(i, carry):
        start = pl.multiple_of(i * tk, tk)
        k_blk = k_ref[pl.ds(start, tk), :]
        vt_blk = vt_ref[:, pl.ds(start, tk)]
        new = []
        for g in range(heads):
            m, l, acc = carry[g]
            s = lax.dot_general(k_blk, qs[g], (((1,), (1,)), ((), ())),
                                preferred_element_type=F32)
            m_new = jnp.maximum(m, jnp.max(s, axis=0, keepdims=True))
            alpha = jnp.exp(m - m_new)
            p = jnp.exp(s - m_new)
            l_new = alpha * l + jnp.sum(p, axis=0, keepdims=True)
            acc_new = alpha * acc + _dot(vt_blk, p.astype(BF16))
            new.append((m_new, l_new, acc_new))
        return tuple(new)

    init = tuple((jnp.full((1, tq), -jnp.inf, F32), jnp.zeros((1, tq), F32),
                  jnp.zeros((HEAD_DIM, tq), F32)) for _ in range(heads))
    final = lax.fori_loop(0, s_len // tk, body, init)
    out_t = jnp.concatenate([acc / l for (_, l, acc) in final], axis=0)
    o_ref[...] = out_t.T.astype(o_ref.dtype)


def _attn_call(q, k, vt, token_start, n_seq, s_len, tq, tk):
    heads_per_step = 2
    q_cols = heads_per_step * HEAD_DIM
    steps_per_kv = (N_Q_HEADS // N_KV_HEADS) // heads_per_step
    q_blk0 = token_start // tq
    seq0 = token_start // s_len
    nq = s_len // tq

    def q_idx(b, hk, gp, j):
        return (q_blk0 + b * nq + j, hk * steps_per_kv + gp)

    return pl.pallas_call(
        functools.partial(_attn_kernel, tk=tk),
        grid=(n_seq, N_KV_HEADS, steps_per_kv, nq),
        in_specs=[
            pl.BlockSpec((tq, q_cols), q_idx),
            pl.BlockSpec((None, s_len, HEAD_DIM), lambda b, hk, gp, j: (hk, seq0 + b, 0)),
            pl.BlockSpec((HEAD_DIM, s_len), lambda b, hk, gp, j: (hk, seq0 + b)),
        ],
        out_specs=pl.BlockSpec((tq, q_cols),
                               lambda b, hk, gp, j: (b * nq + j, hk * steps_per_kv + gp)),
        out_shape=jax.ShapeDtypeStruct((n_seq * s_len, ATTN_WIDTH), BF16),
        compiler_params=_compiler_params(4),
        name="attn",
    )(q, k, vt)


def _group_partner(x, lane, d, group):
    unit = group // 4
    pos = (lane % group) // unit
    return jnp.where(pos + d < 4,
                     pltpu.roll(x, LANES - d * unit, 1),
                     pltpu.roll(x, (4 - d) * unit, 1))


def _route(scores, bias, lane):
    valid = lane < N_EXPERTS
    biased = scores + bias
    rank = jnp.zeros(biased.shape, jnp.int32)
    pos = lane % EXPERTS_PER_GROUP
    for d in range(1, EXPERTS_PER_GROUP):
        other = _group_partner(biased, lane, d, EXPERTS_PER_GROUP)
        other_is_lower = (pos + d) >= EXPERTS_PER_GROUP
        beats = (other > biased) | ((other == biased) & other_is_lower)
        rank = rank + beats.astype(jnp.int32)
    top2 = rank < 2
    contrib = jnp.where(top2, biased, 0.0)
    group_score = contrib
    for d in range(1, EXPERTS_PER_GROUP):
        group_score = group_score + _group_partner(contrib, lane, d, EXPERTS_PER_GROUP)
    n_groups = N_EXPERTS // EXPERTS_PER_GROUP
    gpos = lane // EXPERTS_PER_GROUP
    losses = jnp.zeros(biased.shape, jnp.int32)
    for d in range(1, n_groups):
        other = _group_partner(group_score, lane, d, N_EXPERTS)
        other_is_lower = (gpos + d) >= n_groups
        beats = (other > group_score) | ((other == group_score) & other_is_lower)
        losses = losses + beats.astype(jnp.int32)
    selected = top2 & (losses == 0) & valid
    w = jnp.where(selected, scores, 0.0)
    return w / jnp.sum(w, axis=-1, keepdims=True)


def _post_kernel(l_ref, x_ref, a_ref, gm_ref, aon_ref, wo_ref, g1_ref, b1_ref,
                 wr_hi_ref, wr_lo_ref, rb_ref, x1_ref, gates_ref):
    del l_ref
    a = _rms_norm(a_ref[...].astype(F32), aon_ref[...]).astype(BF16)
    mix = _dot(a, wo_ref[:ATTN_WIDTH, :]) + _dot(gm_ref[...], wo_ref[ATTN_WIDTH:, :])
    x1 = _layer_norm(DEEPNORM_ALPHA * x_ref[...] + mix, g1_ref[...], b1_ref[...])
    x1_ref[...] = x1
    x_hi = x1.astype(BF16)
    x_lo = (x1 - x_hi.astype(F32)).astype(BF16)
    logits = (_dot(x_hi, wr_hi_ref[...]) + _dot(x_lo, wr_hi_ref[...])
              + _dot(x_hi, wr_lo_ref[...]))
    lane = lax.broadcasted_iota(jnp.int32, (1, LANES), 1)
    gates_ref[...] = _route(jax.nn.sigmoid(logits), rb_ref[...], lane)


def _post_call(layer, x, attn, gm, p, tt):
    n = x.shape[0]

    def tok(i, l):
        return (i, 0)

    def lay3(i, l):
        return (l[0], 0, 0)

    def const2(i, l):
        return (0, 0)

    grid_spec = pltpu.PrefetchScalarGridSpec(
        num_scalar_prefetch=1,
        grid=(n // tt,),
        in_specs=[
            pl.BlockSpec((tt, D_MODEL), tok),
            pl.BlockSpec((tt, ATTN_WIDTH), tok),
            pl.BlockSpec((tt, GMLP_WIDTH), tok),
            pl.BlockSpec((None, 1, ATTN_WIDTH), lay3),
            pl.BlockSpec((None, D_MODEL, D_MODEL), lay3),
            pl.BlockSpec((None, 1, D_MODEL), lay3),
            pl.BlockSpec((None, 1, D_MODEL), lay3),
            pl.BlockSpec((D_MODEL, LANES), const2),
            pl.BlockSpec((D_MODEL, LANES), const2),
            pl.BlockSpec((1, LANES), const2),
        ],
        out_specs=[pl.BlockSpec((tt, D_MODEL), tok), pl.BlockSpec((tt, LANES), tok)],
    )
    return pl.pallas_call(
        _post_kernel,
        grid_spec=grid_spec,
        out_shape=[jax.ShapeDtypeStruct((n, D_MODEL), F32),
                   jax.ShapeDtypeStruct((n, LANES), F32)],
        compiler_params=_compiler_params(1),
        name="post",
    )(layer, x, attn, gm, p["attn_out_norm"], p["w_out"], p["ln1_g"], p["ln1_b"],
      p["wr_hi"], p["wr_lo"], p["router_bias"])


def _moe_kernel(l_ref, x_ref, gates_ref, wg_ref, wu_ref, wd_ref, g2_ref, b2_ref,
                o_ref, xb_ref, acc_ref):
    del l_ref
    e = pl.program_id(1)

    @pl.when(e == 0)
    def _():
        xb_ref[...] = x_ref[...].astype(BF16)
        acc_ref[...] = jnp.zeros_like(acc_ref)

    xb = xb_ref[...]
    hidden = jax.nn.silu(_dot(xb, wg_ref[...])) * _dot(xb, wu_ref[...])
    lane = lax.broadcasted_iota(jnp.int32, (1, LANES), 1)
    gate = jnp.sum(jnp.where(lane == e, gates_ref[...], 0.0), axis=-1, keepdims=True)
    acc_ref[...] += gate * _dot(hidden.astype(BF16), wd_ref[...])

    @pl.when(e == N_EXPERTS - 1)
    def _():
        o_ref[...] = _layer_norm(DEEPNORM_ALPHA * x_ref[...] + acc_ref[...],
                                 g2_ref[...], b2_ref[...])


def _moe_call(layer, x1, gates, p, tt):
    n = x1.shape[0]

    def tok(i, e, l):
        return (i, 0)

    def lay3(i, e, l):
        return (l[0], 0, 0)

    def expert(i, e, l):
        return (l[0], e, 0, 0)

    grid_spec = pltpu.PrefetchScalarGridSpec(
        num_scalar_prefetch=1,
        grid=(n // tt, N_EXPERTS),
        in_specs=[
            pl.BlockSpec((tt, D_MODEL), tok),
            pl.BlockSpec((tt, LANES), tok),
            pl.BlockSpec((None, None, D_MODEL, EXPERT_FF), expert),
            pl.BlockSpec((None, None, D_MODEL, EXPERT_FF), expert),
            pl.BlockSpec((None, None, EXPERT_FF, D_MODEL), expert),
            pl.BlockSpec((None, 1, D_MODEL), lay3),
            pl.BlockSpec((None, 1, D_MODEL), lay3),
        ],
        out_specs=pl.BlockSpec((tt, D_MODEL), tok),
        scratch_shapes=[pltpu.VMEM((tt, D_MODEL), BF16), pltpu.VMEM((tt, D_MODEL), F32)],
    )
    return pl.pallas_call(
        _moe_kernel,
        grid_spec=grid_spec,
        out_shape=jax.ShapeDtypeStruct((n, D_MODEL), F32),
        compiler_params=_compiler_params(2),
        name="moe",
    )(layer, x1, gates, p["w_gate"], p["w_up"], p["w_down"], p["ln2_g"], p["ln2_b"])


def _rope_tables(max_len):
    t = jnp.arange(max_len, dtype=jnp.int32)
    row = (t // GRID_W).astype(F32)
    col = (t % GRID_W).astype(F32)
    n_pairs_axis = HEAD_DIM // 4
    inv_freq = 1.0 / (ROPE_THETA ** (jnp.arange(n_pairs_axis, dtype=F32) / n_pairs_axis))
    ang = jnp.concatenate([row[:, None] * inv_freq, col[:, None] * inv_freq], -1)
    cos, sin = jnp.cos(ang), jnp.sin(ang)
    reps = LANES // HEAD_DIM
    cos128 = jnp.tile(jnp.concatenate([cos, cos], -1), (1, reps))
    sin128 = jnp.tile(jnp.concatenate([-sin, sin], -1), (1, reps))
    return cos128, sin128


def _prepare_params(max_len, w_in, q_norm, k_norm, gmlp_ln_g, gmlp_ln_b, w_spatial, b_spatial,
                    attn_out_norm, gmlp_out_norm, w_out, ln1_g, ln1_b, w_router, router_bias,
                    w_gate, w_up, w_down, ln2_g, ln2_b):
    depth = w_in.shape[0]
    perm64 = jnp.concatenate([jnp.arange(0, HEAD_DIM, 2), jnp.arange(1, HEAD_DIM, 2)])
    n_rot_heads = N_Q_HEADS + N_KV_HEADS
    rot_cols = (jnp.arange(n_rot_heads)[:, None] * HEAD_DIM + perm64[None, :]).reshape(-1)
    cols = jnp.concatenate([rot_cols, jnp.arange(n_rot_heads * HEAD_DIM, IN_WIDTH)])
    cos128, sin128 = _rope_tables(max_len)
    head_of_lane = jnp.arange(ATTN_WIDTH) // HEAD_DIM
    row3 = lambda a: a.reshape(depth, 1, -1).astype(F32)
    wr = jnp.zeros((D_MODEL, LANES), F32).at[:, :N_EXPERTS].set(w_router.astype(F32))
    wr_hi = wr.astype(BF16)
    return {
        "w_in": w_in[:, :, cols].astype(BF16),
        "cos": cos128,
        "sin": sin128,
        "seg": (head_of_lane[:, None] == head_of_lane[None, :]).astype(BF16),
        "q_norm": row3(jnp.tile(q_norm[:, perm64], (1, N_Q_HEADS))),
        "k_norm": row3(jnp.tile(k_norm[:, perm64], (1, N_KV_HEADS))),
        "gmlp_ln_g": row3(gmlp_ln_g),
        "gmlp_ln_b": row3(gmlp_ln_b),
        "w_cat": w_spatial.transpose(0, 2, 1, 3).reshape(depth, CHUNK, N_GMLP_HEADS * CHUNK).astype(BF16),
        "b_sp": jnp.repeat(b_spatial.transpose(0, 2, 1), GMLP_HEAD_DIM, axis=2).astype(F32),
        "attn_out_norm": row3(attn_out_norm),
        "gmlp_out_norm": row3(gmlp_out_norm),
        "w_out": w_out.astype(BF16),
        "ln1_g": row3(ln1_g),
        "ln1_b": row3(ln1_b),
        "wr_hi": wr_hi,
        "wr_lo": (wr - wr_hi.astype(F32)).astype(BF16),
        "router_bias": jnp.zeros((1, LANES), F32).at[0, :N_EXPERTS].set(router_bias.astype(F32)),
        "w_gate": w_gate.astype(BF16),
        "w_up": w_up.astype(BF16),
        "w_down": w_down.astype(BF16),
        "ln2_g": row3(ln2_g),
        "ln2_b": row3(ln2_b),
    }


def _trunk(x_tokens, groups, p, depth, tt, tt_moe, tq, tk):
    (_, n_seq0, s0), = groups[:1]
    n_prompt_blocks = (n_seq0 * s0) // tt
    prompt_period = s0 // tt
    sample_period = (groups[1][2] if len(groups) > 1 else s0) // tt
    periods = (n_prompt_blocks, prompt_period, sample_period)

    def layer_body(l, x):
        layer = jnp.reshape(l, (1,)).astype(jnp.int32)
        q, k, vt, gm = _proj_call(layer, x, p, tt, periods)
        attn = [_attn_call(q, k, vt, start, n_seq, s_len, min(tq, s_len), min(tk, s_len))
                for (start, n_seq, s_len) in groups]
        attn = attn[0] if len(attn) == 1 else jnp.concatenate(attn, axis=0)
        x1, gates = _post_call(layer, x, attn, gm, p, tt)
        return _moe_call(layer, x1, gates, p, tt_moe)

    return lax.fori_loop(0, depth, layer_body, x_tokens)


def kernel(x_prompt, x_sample, w_in, q_norm, k_norm, gmlp_ln_g, gmlp_ln_b, w_spatial, b_spatial,
           attn_out_norm, gmlp_out_norm, w_out, ln1_g, ln1_b, w_router, router_bias,
           w_gate, w_up, w_down, ln2_g, ln2_b):
    bp, sp, d = x_prompt.shape
    bs, ss, _ = x_sample.shape
    assert sp % ss == 0 and (bp * sp) % ss == 0, "sample sequences must tile the prompt stream"
    p = _prepare_params(max(sp, ss), w_in, q_norm, k_norm, gmlp_ln_g, gmlp_ln_b, w_spatial,
                        b_spatial, attn_out_norm, gmlp_out_norm, w_out, ln1_g, ln1_b, w_router,
                        router_bias, w_gate, w_up, w_down, ln2_g, ln2_b)
    x = jnp.concatenate([x_prompt.reshape(bp * sp, d), x_sample.reshape(bs * ss, d)], axis=0)
    groups = ((0, bp, sp), (bp * sp, bs, ss))
    y = _trunk(x.astype(F32), groups, p, w_in.shape[0], tt=256, tt_moe=512, tq=256, tk=256)
    return (y[:bp * sp].reshape(bp, sp, d), y[bp * sp:].reshape(bs, ss, d))
```

```python
import functools

import jax
import jax.numpy as jnp
from jax import lax
from jax.experimental import pallas as pl
from jax.experimental.pallas import tpu as pltpu

D_MODEL = 1024
DEPTH = 4
HEAD_DIM = 64
N_Q_HEADS = 8
N_KV_HEADS = 2
ATTN_WIDTH = N_Q_HEADS * HEAD_DIM
KV_WIDTH = N_KV_HEADS * HEAD_DIM
GMLP_WIDTH = 512
N_GMLP_HEADS = 8
GMLP_HEAD_DIM = 64
CHUNK = 128
IN_WIDTH = ATTN_WIDTH + 2 * KV_WIDTH + 2 * GMLP_WIDTH
N_EXPERTS = 16
EXPERTS_PER_GROUP = 4
EXPERT_FF = 512
ROPE_THETA = 10000.0
GRID_W = 64
EPS = 1e-6
DEEPNORM_ALPHA = float((2 * DEPTH) ** 0.25)
LOG2_E = 1.4426950408889634
STAT_ROWS = 8
MAX_SAFE_OFFSET = 60.0

LANES = 128
F32 = jnp.float32
BF16 = jnp.bfloat16
VMEM_LIMIT_BYTES = 48 * 1024 * 1024


def _compiler_params(n_grid_dims):
    return pltpu.CompilerParams(
        dimension_semantics=("arbitrary",) * n_grid_dims,
        vmem_limit_bytes=VMEM_LIMIT_BYTES,
    )


def _dot(a, b):
    return jnp.dot(a, b, preferred_element_type=F32)


def _layer_norm(x, g, b):
    mu = jnp.mean(x, axis=-1, keepdims=True)
    xc = x - mu
    var = jnp.mean(xc * xc, axis=-1, keepdims=True)
    return xc * lax.rsqrt(var + EPS) * g + b


def _rms_norm(x, g):
    return x * lax.rsqrt(jnp.mean(x * x, axis=-1, keepdims=True) + EPS) * g


def _head_rms_norm(x, seg, g):
    ssq = _dot((x * x).astype(BF16), seg)
    return x * lax.rsqrt(ssq * (1.0 / HEAD_DIM) + EPS) * g


def _rope_half_split(x, cos, sin_signed):
    first_half = (lax.broadcasted_iota(jnp.int32, (1, LANES), 1) % HEAD_DIM) < (HEAD_DIM // 2)
    cols = []
    for c in range(x.shape[1] // LANES):
        xc = x[:, c * LANES:(c + 1) * LANES]
        partner = jnp.where(first_half,
                            pltpu.roll(xc, LANES - HEAD_DIM // 2, 1),
                            pltpu.roll(xc, HEAD_DIM // 2, 1))
        cols.append(xc * cos + partner * sin_signed)
    return cols[0] if len(cols) == 1 else jnp.concatenate(cols, axis=1)


def _proj_kernel(l_ref, x_ref, w_in_ref, cos_ref, sin_ref, seg_ref, qn_ref, kn_ref,
                 lng_ref, lnb_ref, wcat_ref, bsp_ref, gon_ref,
                 q_ref, k_ref, vt_ref, gm_ref, kstat_ref):
    del l_ref
    tt = x_ref.shape[0]
    h = _dot(x_ref[...].astype(BF16), w_in_ref[...])
    cos = cos_ref[...]
    sin_signed = sin_ref[...]

    q = _head_rms_norm(h[:, :ATTN_WIDTH], seg_ref[...], qn_ref[...])
    q = _rope_half_split(q, cos, sin_signed) * (LOG2_E * HEAD_DIM ** -0.5)
    q_ref[...] = q.astype(BF16)

    k0 = ATTN_WIDTH
    seg_kv = seg_ref[:KV_WIDTH, :KV_WIDTH]
    k = _head_rms_norm(h[:, k0:k0 + KV_WIDTH], seg_kv, kn_ref[...])
    k = _rope_half_split(k, cos, sin_signed).astype(BF16)
    kf = k.astype(F32)
    k_sq = _dot((kf * kf).astype(BF16), seg_kv)
    kstat_ref[...] = jnp.broadcast_to(jnp.max(k_sq, axis=0, keepdims=True), kstat_ref.shape)
    for hk in range(N_KV_HEADS):
        k_ref[hk] = k[:, hk * HEAD_DIM:(hk + 1) * HEAD_DIM]

    v0 = k0 + KV_WIDTH
    vt_ref[...] = h[:, v0:v0 + KV_WIDTH].T.astype(BF16)

    u0 = v0 + KV_WIDTH
    u = jax.nn.gelu(h[:, u0:u0 + GMLP_WIDTH])
    vg = _layer_norm(jax.nn.gelu(h[:, u0 + GMLP_WIDTH:]), lng_ref[...], lnb_ref[...])
    lane_head = lax.broadcasted_iota(jnp.int32, (1, GMLP_WIDTH), 1) // GMLP_HEAD_DIM
    mixed = []
    for c in range(tt // CHUNK):
        vc = vg[c * CHUNK:(c + 1) * CHUNK].astype(BF16)
        stack = jnp.concatenate(
            [jnp.where(lane_head == hh, vc, jnp.zeros_like(vc)) for hh in range(N_GMLP_HEADS)],
            axis=0)
        mixed.append(_dot(wcat_ref[...], stack) + bsp_ref[...])
    mixed = mixed[0] if len(mixed) == 1 else jnp.concatenate(mixed, axis=0)
    gm_ref[...] = _rms_norm(u * mixed, gon_ref[...]).astype(BF16)


def _proj_call(layer, x, p, tt, rope_period_blocks):
    n = x.shape[0]
    n_prompt_blocks, prompt_period, sample_period = rope_period_blocks

    def tok(i, l):
        return (i, 0)

    def rope_idx(i, l):
        return (jnp.where(i < n_prompt_blocks, i % prompt_period, i % sample_period), 0)

    def lay3(i, l):
        return (l[0], 0, 0)

    def const2(i, l):
        return (0, 0)

    grid_spec = pltpu.PrefetchScalarGridSpec(
        num_scalar_prefetch=1,
        grid=(n // tt,),
        in_specs=[
            pl.BlockSpec((tt, D_MODEL), tok),
            pl.BlockSpec((None, D_MODEL, IN_WIDTH), lay3),
            pl.BlockSpec((tt, LANES), rope_idx),
            pl.BlockSpec((tt, LANES), rope_idx),
            pl.BlockSpec((ATTN_WIDTH, ATTN_WIDTH), const2),
            pl.BlockSpec((None, 1, ATTN_WIDTH), lay3),
            pl.BlockSpec((None, 1, KV_WIDTH), lay3),
            pl.BlockSpec((None, 1, GMLP_WIDTH), lay3),
            pl.BlockSpec((None, 1, GMLP_WIDTH), lay3),
            pl.BlockSpec((None, CHUNK, N_GMLP_HEADS * CHUNK), lay3),
            pl.BlockSpec((None, CHUNK, GMLP_WIDTH), lay3),
            pl.BlockSpec((None, 1, GMLP_WIDTH), lay3),
        ],
        out_specs=[
            pl.BlockSpec((tt, ATTN_WIDTH), tok),
            pl.BlockSpec((N_KV_HEADS, tt, HEAD_DIM), lambda i, l: (0, i, 0)),
            pl.BlockSpec((KV_WIDTH, tt), lambda i, l: (0, i)),
            pl.BlockSpec((tt, GMLP_WIDTH), tok),
            pl.BlockSpec((STAT_ROWS, KV_WIDTH), tok),
        ],
    )
    return pl.pallas_call(
        _proj_kernel,
        grid_spec=grid_spec,
        out_shape=[
            jax.ShapeDtypeStruct((n, ATTN_WIDTH), BF16),
            jax.ShapeDtypeStruct((N_KV_HEADS, n, HEAD_DIM), BF16),
            jax.ShapeDtypeStruct((KV_WIDTH, n), BF16),
            jax.ShapeDtypeStruct((n, GMLP_WIDTH), BF16),
            jax.ShapeDtypeStruct((n // tt * STAT_ROWS, KV_WIDTH), F32),
        ],
        compiler_params=_compiler_params(1),
        name="proj",
    )(layer, x, p["w_in"], p["cos"], p["sin"], p["seg"], p["q_norm"], p["k_norm"],
      p["gmlp_ln_g"], p["gmlp_ln_b"], p["w_cat"], p["b_sp"], p["gmlp_out_norm"])


def _scores(k_blk, q):
    return lax.dot_general(k_blk, q, (((1,), (1,)), ((), ())), preferred_element_type=F32)


def _attn_kernel(q_ref, k_ref, vt_ref, kstat_ref, o_ref, *, tk, exact_max):
    tq = q_ref.shape[0]
    s_len = k_ref.shape[0]
    heads = q_ref.shape[1] // HEAD_DIM
    q_all = q_ref[...]
    qs = [q_all[:, g * HEAD_DIM:(g + 1) * HEAD_DIM] for g in range(heads)]
    if exact_max:
        def max_body(i, ms):
            k_blk = k_ref[pl.ds(pl.multiple_of(i * tk, tk), tk), :]
            return tuple(jnp.maximum(ms[g], jnp.max(_scores(k_blk, qs[g]), axis=0, keepdims=True))
                         for g in range(heads))
        offs = lax.fori_loop(0, s_len // tk, max_body,
                             tuple(jnp.full((1, tq), -jnp.inf, F32) for _ in range(heads)))
    else:
        hk = pl.program_id(1)
        lane = lax.broadcasted_iota(jnp.int32, (1, LANES), 1)
        kstat = jnp.max(kstat_ref[...], axis=0, keepdims=True)
        kmax2 = jnp.max(jnp.where(lane // HEAD_DIM == hk, kstat, 0.0), axis=1, keepdims=True)
        ones = jnp.ones((8, HEAD_DIM), BF16)
        offs = []
        for g in range(heads):
            qf = qs[g].astype(F32)
            qq = _scores(ones, (qf * qf).astype(BF16))
            offs.append(jnp.sqrt(qq[0:1] * kmax2))

    l8 = [jnp.zeros((8, tq), F32) for _ in range(heads)]
    acc = [jnp.zeros((HEAD_DIM, tq), F32) for _ in range(heads)]
    for c in range(s_len // tk):
        k_blk = k_ref[c * tk:(c + 1) * tk, :]
        vt_blk = vt_ref[:, c * tk:(c + 1) * tk]
        for g in range(heads):
            p = jnp.exp2(_scores(k_blk, qs[g]) - offs[g])
            l8[g] = l8[g] + jnp.sum(p.reshape(tk // 8, 8, tq), axis=0)
            acc[g] = acc[g] + _dot(vt_blk, p.astype(BF16))
    out_t = jnp.concatenate(
        [acc[g] / jnp.sum(l8[g], axis=0, keepdims=True) for g in range(heads)], axis=0)
    o_ref[...] = out_t.T.astype(o_ref.dtype)


def _attn_call(q, k, vt, kstat, token_start, n_seq, s_len, tq, tk, tt, exact_max):
    heads_per_step = 2
    q_cols = heads_per_step * HEAD_DIM
    steps_per_kv = (N_Q_HEADS // N_KV_HEADS) // heads_per_step
    q_blk0 = token_start // tq
    seq0 = token_start // s_len
    nq = s_len // tq
    stat_rows = s_len // tt * STAT_ROWS

    def q_idx(b, hk, gp, j):
        return (q_blk0 + b * nq + j, hk * steps_per_kv + gp)

    return pl.pallas_call(
        functools.partial(_attn_kernel, tk=tk, exact_max=exact_max),
        grid=(n_seq, N_KV_HEADS, steps_per_kv, nq),
        in_specs=[
            pl.BlockSpec((tq, q_cols), q_idx),
            pl.BlockSpec((None, s_len, HEAD_DIM), lambda b, hk, gp, j: (hk, seq0 + b, 0)),
            pl.BlockSpec((HEAD_DIM, s_len), lambda b, hk, gp, j: (hk, seq0 + b)),
            pl.BlockSpec((stat_rows, KV_WIDTH), lambda b, hk, gp, j: (seq0 + b, 0)),
        ],
        out_specs=pl.BlockSpec((tq, q_cols),
                               lambda b, hk, gp, j: (b * nq + j, hk * steps_per_kv + gp)),
        out_shape=jax.ShapeDtypeStruct((n_seq * s_len, ATTN_WIDTH), BF16),
        compiler_params=_compiler_params(4),
        name="attn_exact_max" if exact_max else "attn",
    )(q, k, vt, kstat)


def _group_partner(x, lane, d, group):
    unit = group // 4
    pos = (lane % group) // unit
    return jnp.where(pos + d < 4,
                     pltpu.roll(x, LANES - d * unit, 1),
                     pltpu.roll(x, (4 - d) * unit, 1))


def _route(scores, bias, lane):
    valid = lane < N_EXPERTS
    biased = scores + bias
    rank = jnp.zeros(biased.shape, jnp.int32)
    pos = lane % EXPERTS_PER_GROUP
    for d in range(1, EXPERTS_PER_GROUP):
        other = _group_partner(biased, lane, d, EXPERTS_PER_GROUP)
        other_is_lower = (pos + d) >= EXPERTS_PER_GROUP
        beats = (other > biased) | ((other == biased) & other_is_lower)
        rank = rank + beats.astype(jnp.int32)
    top2 = rank < 2
    contrib = jnp.where(top2, biased, 0.0)
    group_score = contrib
    for d in range(1, EXPERTS_PER_GROUP):
        group_score = group_score + _group_partner(contrib, lane, d, EXPERTS_PER_GROUP)
    n_groups = N_EXPERTS // EXPERTS_PER_GROUP
    gpos = lane // EXPERTS_PER_GROUP
    losses = jnp.zeros(biased.shape, jnp.int32)
    for d in range(1, n_groups):
        other = _group_partner(group_score, lane, d, N_EXPERTS)
        other_is_lower = (gpos + d) >= n_groups
        beats = (other > group_score) | ((other == group_score) & other_is_lower)
        losses = losses + beats.astype(jnp.int32)
    selected = top2 & (losses == 0) & valid
    w = jnp.where(selected, scores, 0.0)
    return w / jnp.sum(w, axis=-1, keepdims=True)


def _post_kernel(l_ref, x_ref, a_ref, gm_ref, aon_ref, wo_ref, g1_ref, b1_ref,
                 wr_hi_ref, wr_lo_ref, rb_ref, x1_ref, gates_ref):
    del l_ref
    a = _rms_norm(a_ref[...].astype(F32), aon_ref[...]).astype(BF16)
    mix = _dot(a, wo_ref[:ATTN_WIDTH, :]) + _dot(gm_ref[...], wo_ref[ATTN_WIDTH:, :])
    x1 = _layer_norm(DEEPNORM_ALPHA * x_ref[...] + mix, g1_ref[...], b1_ref[...])
    x1_ref[...] = x1
    x_hi = x1.astype(BF16)
    x_lo = (x1 - x_hi.astype(F32)).astype(BF16)
    logits = (_dot(x_hi, wr_hi_ref[...]) + _dot(x_lo, wr_hi_ref[...])
              + _dot(x_hi, wr_lo_ref[...]))
    lane = lax.broadcasted_iota(jnp.int32, (1, LANES), 1)
    gates_ref[...] = _route(jax.nn.sigmoid(logits), rb_ref[...], lane)


def _post_call(layer, x, attn, gm, p, tt):
    n = x.shape[0]

    def tok(i, l):
        return (i, 0)

    def lay3(i, l):
        return (l[0], 0, 0)

    def const2(i, l):
        return (0, 0)

    grid_spec = pltpu.PrefetchScalarGridSpec(
        num_scalar_prefetch=1,
        grid=(n // tt,),
        in_specs=[
            pl.BlockSpec((tt, D_MODEL), tok),
            pl.BlockSpec((tt, ATTN_WIDTH), tok),
            pl.BlockSpec((tt, GMLP_WIDTH), tok),
            pl.BlockSpec((None, 1, ATTN_WIDTH), lay3),
            pl.BlockSpec((None, D_MODEL, D_MODEL), lay3),
            pl.BlockSpec((None, 1, D_MODEL), lay3),
            pl.BlockSpec((None, 1, D_MODEL), lay3),
            pl.BlockSpec((D_MODEL, LANES), const2),
            pl.BlockSpec((D_MODEL, LANES), const2),
            pl.BlockSpec((1, LANES), const2),
        ],
        out_specs=[pl.BlockSpec((tt, D_MODEL), tok), pl.BlockSpec((tt, LANES), tok)],
    )
    return pl.pallas_call(
        _post_kernel,
        grid_spec=grid_spec,
        out_shape=[jax.ShapeDtypeStruct((n, D_MODEL), F32),
                   jax.ShapeDtypeStruct((n, LANES), F32)],
        compiler_params=_compiler_params(1),
        name="post",
    )(layer, x, attn, gm, p["attn_out_norm"], p["w_out"], p["ln1_g"], p["ln1_b"],
      p["wr_hi"], p["wr_lo"], p["router_bias"])


def _moe_kernel(l_ref, x_ref, gates_ref, wg_ref, wu_ref, wd_ref, g2_ref, b2_ref,
                o_ref, xb_ref, acc_ref):
    del l_ref
    e = pl.program_id(1)

    @pl.when(e == 0)
    def _():
        xb_ref[...] = x_ref[...].astype(BF16)
        acc_ref[...] = jnp.zeros_like(acc_ref)

    xb = xb_ref[...]
    hidden = jax.nn.silu(_dot(xb, wg_ref[...])) * _dot(xb, wu_ref[...])
    lane = lax.broadcasted_iota(jnp.int32, (1, LANES), 1)
    gate = jnp.sum(jnp.where(lane == e, gates_ref[...], 0.0), axis=-1, keepdims=True)
    acc_ref[...] += gate * _dot(hidden.astype(BF16), wd_ref[...])

    @pl.when(e == N_EXPERTS - 1)
    def _():
        o_ref[...] = _layer_norm(DEEPNORM_ALPHA * x_ref[...] + acc_ref[...],
                                 g2_ref[...], b2_ref[...])


def _moe_call(layer, x1, gates, p, tt):
    n = x1.shape[0]

    def tok(i, e, l):
        return (i, 0)

    def lay3(i, e, l):
        return (l[0], 0, 0)

    def expert(i, e, l):
        return (l[0], e, 0, 0)

    grid_spec = pltpu.PrefetchScalarGridSpec(
        num_scalar_prefetch=1,
        grid=(n // tt, N_EXPERTS),
        in_specs=[
            pl.BlockSpec((tt, D_MODEL), tok),
            pl.BlockSpec((tt, LANES), tok),
            pl.BlockSpec((None, None, D_MODEL, EXPERT_FF), expert),
            pl.BlockSpec((None, None, D_MODEL, EXPERT_FF), expert),
            pl.BlockSpec((None, None, EXPERT_FF, D_MODEL), expert),
            pl.BlockSpec((None, 1, D_MODEL), lay3),
            pl.BlockSpec((None, 1, D_MODEL), lay3),
        ],
        out_specs=pl.BlockSpec((tt, D_MODEL), tok),
        scratch_shapes=[pltpu.VMEM((tt, D_MODEL), BF16), pltpu.VMEM((tt, D_MODEL), F32)],
    )
    return pl.pallas_call(
        _moe_kernel,
        grid_spec=grid_spec,
        out_shape=jax.ShapeDtypeStruct((n, D_MODEL), F32),
        compiler_params=_compiler_params(2),
        name="moe",
    )(layer, x1, gates, p["w_gate"], p["w_up"], p["w_down"], p["ln2_g"], p["ln2_b"])


def _rope_tables(max_len):
    t = jnp.arange(max_len, dtype=jnp.int32)
    row = (t // GRID_W).astype(F32)
    col = (t % GRID_W).astype(F32)
    n_pairs_axis = HEAD_DIM // 4
    inv_freq = 1.0 / (ROPE_THETA ** (jnp.arange(n_pairs_axis, dtype=F32) / n_pairs_axis))
    ang = jnp.concatenate([row[:, None] * inv_freq, col[:, None] * inv_freq], -1)
    cos, sin = jnp.cos(ang), jnp.sin(ang)
    reps = LANES // HEAD_DIM
    cos128 = jnp.tile(jnp.concatenate([cos, cos], -1), (1, reps))
    sin128 = jnp.tile(jnp.concatenate([-sin, sin], -1), (1, reps))
    return cos128, sin128


def _prepare_params(max_len, w_in, q_norm, k_norm, gmlp_ln_g, gmlp_ln_b, w_spatial, b_spatial,
                    attn_out_norm, gmlp_out_norm, w_out, ln1_g, ln1_b, w_router, router_bias,
                    w_gate, w_up, w_down, ln2_g, ln2_b):
    depth = w_in.shape[0]
    perm64 = jnp.concatenate([jnp.arange(0, HEAD_DIM, 2), jnp.arange(1, HEAD_DIM, 2)])
    n_rot_heads = N_Q_HEADS + N_KV_HEADS
    rot_cols = (jnp.arange(n_rot_heads)[:, None] * HEAD_DIM + perm64[None, :]).reshape(-1)
    cols = jnp.concatenate([rot_cols, jnp.arange(n_rot_heads * HEAD_DIM, IN_WIDTH)])
    cos128, sin128 = _rope_tables(max_len)
    head_of_lane = jnp.arange(ATTN_WIDTH) // HEAD_DIM
    row3 = lambda a: a.reshape(depth, 1, -1).astype(F32)
    wr = jnp.zeros((D_MODEL, LANES), F32).at[:, :N_EXPERTS].set(w_router.astype(F32))
    wr_hi = wr.astype(BF16)
    max_offset = (1.01 * LOG2_E * HEAD_DIM ** 0.5
                  * jnp.max(jnp.abs(q_norm), axis=1) * jnp.max(jnp.abs(k_norm), axis=1))
    return {
        "attn_needs_exact_max": max_offset > MAX_SAFE_OFFSET,
        "w_in": w_in[:, :, cols].astype(BF16),
        "cos": cos128,
        "sin": sin128,
        "seg": (head_of_lane[:, None] == head_of_lane[None, :]).astype(BF16),
        "q_norm": row3(jnp.tile(q_norm[:, perm64], (1, N_Q_HEADS))),
        "k_norm": row3(jnp.tile(k_norm[:, perm64], (1, N_KV_HEADS))),
        "gmlp_ln_g": row3(gmlp_ln_g),
        "gmlp_ln_b": row3(gmlp_ln_b),
        "w_cat": w_spatial.transpose(0, 2, 1, 3).reshape(depth, CHUNK, N_GMLP_HEADS * CHUNK).astype(BF16),
        "b_sp": jnp.repeat(b_spatial.transpose(0, 2, 1), GMLP_HEAD_DIM, axis=2).astype(F32),
        "attn_out_norm": row3(attn_out_norm),
        "gmlp_out_norm": row3(gmlp_out_norm),
        "w_out": w_out.astype(BF16),
        "ln1_g": row3(ln1_g),
        "ln1_b": row3(ln1_b),
        "wr_hi": wr_hi,
        "wr_lo": (wr - wr_hi.astype(F32)).astype(BF16),
        "router_bias": jnp.zeros((1, LANES), F32).at[0, :N_EXPERTS].set(router_bias.astype(F32)),
        "w_gate": w_gate.astype(BF16),
        "w_up": w_up.astype(BF16),
        "w_down": w_down.astype(BF16),
        "ln2_g": row3(ln2_g),
        "ln2_b": row3(ln2_b),
    }


def _trunk(x_tokens, groups, p, depth, tt, tt_moe, tq, tk):
    (_, n_seq0, s0), = groups[:1]
    n_prompt_blocks = (n_seq0 * s0) // tt
    prompt_period = s0 // tt
    sample_period = (groups[1][2] if len(groups) > 1 else s0) // tt
    periods = (n_prompt_blocks, prompt_period, sample_period)

    def layer_body(l, x):
        layer = jnp.reshape(l, (1,)).astype(jnp.int32)
        q, k, vt, gm, kstat = _proj_call(layer, x, p, tt, periods)

        def attention(exact_max):
            outs = [_attn_call(q, k, vt, kstat, start, n_seq, s_len, min(tq, s_len),
                               min(tk, s_len), tt, exact_max)
                    for (start, n_seq, s_len) in groups]
            return outs[0] if len(outs) == 1 else jnp.concatenate(outs, axis=0)

        attn = lax.cond(p["attn_needs_exact_max"][l],
                        functools.partial(attention, True), functools.partial(attention, False))
        x1, gates = _post_call(layer, x, attn, gm, p, tt)
        return _moe_call(layer, x1, gates, p, tt_moe)

    return lax.fori_loop(0, depth, layer_body, x_tokens)


def kernel(x_prompt, x_sample, w_in, q_norm, k_norm, gmlp_ln_g, gmlp_ln_b, w_spatial, b_spatial,
           attn_out_norm, gmlp_out_norm, w_out, ln1_g, ln1_b, w_router, router_bias,
           w_gate, w_up, w_down, ln2_g, ln2_b):
    bp, sp, d = x_prompt.shape
    bs, ss, _ = x_sample.shape
    assert sp % ss == 0 and (bp * sp) % ss == 0, "sample sequences must tile the prompt stream"
    p = _prepare_params(max(sp, ss), w_in, q_norm, k_norm, gmlp_ln_g, gmlp_ln_b, w_spatial,
                        b_spatial, attn_out_norm, gmlp_out_norm, w_out, ln1_g, ln1_b, w_router,
                        router_bias, w_gate, w_up, w_down, ln2_g, ln2_b)
    x = jnp.concatenate([x_prompt.reshape(bp * sp, d), x_sample.reshape(bs * ss, d)], axis=0)
    groups = ((0, bp, sp), (bp * sp, bs, ss))
    y = _trunk(x.astype(F32), groups, p, w_in.shape[0], tt=256, tt_moe=512, tq=256, tk=512)
    return (y[:bp * sp].reshape(bp, sp, d), y[bp * sp:].reshape(bs, ss, d))
```

```python
import functools

import jax
import jax.numpy as jnp
from jax import lax
from jax.experimental import pallas as pl
from jax.experimental.pallas import tpu as pltpu

D_MODEL = 1024
DEPTH = 4
HEAD_DIM = 64
N_Q_HEADS = 8
N_KV_HEADS = 2
ATTN_WIDTH = N_Q_HEADS * HEAD_DIM
KV_WIDTH = N_KV_HEADS * HEAD_DIM
GMLP_WIDTH = 512
N_GMLP_HEADS = 8
GMLP_HEAD_DIM = 64
CHUNK = 128
IN_WIDTH = ATTN_WIDTH + 2 * KV_WIDTH + 2 * GMLP_WIDTH
N_EXPERTS = 16
EXPERTS_PER_GROUP = 4
EXPERT_FF = 512
ROPE_THETA = 10000.0
GRID_W = 64
EPS = 1e-6
DEEPNORM_ALPHA = float((2 * DEPTH) ** 0.25)
LOG2_E = 1.4426950408889634
STAT_ROWS = 8
MAX_SAFE_OFFSET = 60.0

LANES = 128
F32 = jnp.float32
BF16 = jnp.bfloat16
VMEM_LIMIT_BYTES = 48 * 1024 * 1024


def _compiler_params(n_grid_dims):
    return pltpu.CompilerParams(
        dimension_semantics=("arbitrary",) * n_grid_dims,
        vmem_limit_bytes=VMEM_LIMIT_BYTES,
    )


def _dot(a, b):
    return jnp.dot(a, b, preferred_element_type=F32)


def _layer_norm(x, g, b):
    mu = jnp.mean(x, axis=-1, keepdims=True)
    xc = x - mu
    var = jnp.mean(xc * xc, axis=-1, keepdims=True)
    return xc * lax.rsqrt(var + EPS) * g + b


def _rms_norm(x, g):
    return x * lax.rsqrt(jnp.mean(x * x, axis=-1, keepdims=True) + EPS) * g


def _head_rms_norm(x, seg, g):
    ssq = _dot((x * x).astype(BF16), seg)
    return x * lax.rsqrt(ssq * (1.0 / HEAD_DIM) + EPS) * g


def _rope_half_split(x, cos, sin_signed):
    first_half = (lax.broadcasted_iota(jnp.int32, (1, LANES), 1) % HEAD_DIM) < (HEAD_DIM // 2)
    cols = []
    for c in range(x.shape[1] // LANES):
        xc = x[:, c * LANES:(c + 1) * LANES]
        partner = jnp.where(first_half,
                            pltpu.roll(xc, LANES - HEAD_DIM // 2, 1),
                            pltpu.roll(xc, HEAD_DIM // 2, 1))
        cols.append(xc * cos + partner * sin_signed)
    return cols[0] if len(cols) == 1 else jnp.concatenate(cols, axis=1)


def _proj_kernel(l_ref, x_ref, w_in_ref, cos_ref, sin_ref, seg_ref, qn_ref, kn_ref,
                 lng_ref, lnb_ref, wcat_ref, bsp_ref, gon_ref,
                 q_ref, k_ref, vt_ref, gm_ref, kstat_ref):
    del l_ref
    tt = x_ref.shape[0]
    h = _dot(x_ref[...].astype(BF16), w_in_ref[...])
    cos = cos_ref[...]
    sin_signed = sin_ref[...]

    q = _head_rms_norm(h[:, :ATTN_WIDTH], seg_ref[...], qn_ref[...])
    q = _rope_half_split(q, cos, sin_signed) * (LOG2_E * HEAD_DIM ** -0.5)
    q_ref[...] = q.astype(BF16)

    k0 = ATTN_WIDTH
    seg_kv = seg_ref[:KV_WIDTH, :KV_WIDTH]
    k = _head_rms_norm(h[:, k0:k0 + KV_WIDTH], seg_kv, kn_ref[...])
    k = _rope_half_split(k, cos, sin_signed).astype(BF16)
    kf = k.astype(F32)
    k_sq = _dot((kf * kf).astype(BF16), seg_kv)
    kstat_ref[...] = jnp.broadcast_to(jnp.max(k_sq, axis=0, keepdims=True), kstat_ref.shape)
    for hk in range(N_KV_HEADS):
        k_ref[hk] = k[:, hk * HEAD_DIM:(hk + 1) * HEAD_DIM]

    v0 = k0 + KV_WIDTH
    vt_ref[...] = h[:, v0:v0 + KV_WIDTH].T.astype(BF16)

    u0 = v0 + KV_WIDTH
    u = jax.nn.gelu(h[:, u0:u0 + GMLP_WIDTH])
    vg = _layer_norm(jax.nn.gelu(h[:, u0 + GMLP_WIDTH:]), lng_ref[...], lnb_ref[...])
    lane_head = lax.broadcasted_iota(jnp.int32, (1, GMLP_WIDTH), 1) // GMLP_HEAD_DIM
    mixed = []
    for c in range(tt // CHUNK):
        vc = vg[c * CHUNK:(c + 1) * CHUNK].astype(BF16)
        stack = jnp.concatenate(
            [jnp.where(lane_head == hh, vc, jnp.zeros_like(vc)) for hh in range(N_GMLP_HEADS)],
            axis=0)
        mixed.append(_dot(wcat_ref[...], stack) + bsp_ref[...])
    mixed = mixed[0] if len(mixed) == 1 else jnp.concatenate(mixed, axis=0)
    gm_ref[...] = _rms_norm(u * mixed, gon_ref[...]).astype(BF16)


def _proj_call(layer, x, p, tt, rope_period_blocks):
    n = x.shape[0]
    n_prompt_blocks, prompt_period, sample_period = rope_period_blocks

    def tok(i, l):
        return (i, 0)

    def rope_idx(i, l):
        return (jnp.where(i < n_prompt_blocks, i % prompt_period, i % sample_period), 0)

    def lay3(i, l):
        return (l[0], 0, 0)

    def const2(i, l):
        return (0, 0)

    grid_spec = pltpu.PrefetchScalarGridSpec(
        num_scalar_prefetch=1,
        grid=(n // tt,),
        in_specs=[
            pl.BlockSpec((tt, D_MODEL), tok),
            pl.BlockSpec((None, D_MODEL, IN_WIDTH), lay3),
            pl.BlockSpec((tt, LANES), rope_idx),
            pl.BlockSpec((tt, LANES), rope_idx),
            pl.BlockSpec((ATTN_WIDTH, ATTN_WIDTH), const2),
            pl.BlockSpec((None, 1, ATTN_WIDTH), lay3),
            pl.BlockSpec((None, 1, KV_WIDTH), lay3),
            pl.BlockSpec((None, 1, GMLP_WIDTH), lay3),
            pl.BlockSpec((None, 1, GMLP_WIDTH), lay3),
            pl.BlockSpec((None, CHUNK, N_GMLP_HEADS * CHUNK), lay3),
            pl.BlockSpec((None, CHUNK, GMLP_WIDTH), lay3),
            pl.BlockSpec((None, 1, GMLP_WIDTH), lay3),
        ],
        out_specs=[
            pl.BlockSpec((tt, ATTN_WIDTH), tok),
            pl.BlockSpec((N_KV_HEADS, tt, HEAD_DIM), lambda i, l: (0, i, 0)),
            pl.BlockSpec((KV_WIDTH, tt), lambda i, l: (0, i)),
            pl.BlockSpec((tt, GMLP_WIDTH), tok),
            pl.BlockSpec((STAT_ROWS, KV_WIDTH), tok),
        ],
    )
    return pl.pallas_call(
        _proj_kernel,
        grid_spec=grid_spec,
        out_shape=[
            jax.ShapeDtypeStruct((n, ATTN_WIDTH), BF16),
            jax.ShapeDtypeStruct((N_KV_HEADS, n, HEAD_DIM), BF16),
            jax.ShapeDtypeStruct((KV_WIDTH, n), BF16),
            jax.ShapeDtypeStruct((n, GMLP_WIDTH), BF16),
            jax.ShapeDtypeStruct((n // tt * STAT_ROWS, KV_WIDTH), F32),
        ],
        compiler_params=_compiler_params(1),
        name="proj",
    )(layer, x, p["w_in"], p["cos"], p["sin"], p["seg"], p["q_norm"], p["k_norm"],
      p["gmlp_ln_g"], p["gmlp_ln_b"], p["w_cat"], p["b_sp"], p["gmlp_out_norm"])


def _scores(k_blk, q):
    return lax.dot_general(k_blk, q, (((1,), (1,)), ((), ())), preferred_element_type=F32)


def _attn_kernel(q_ref, k_ref, vt_ref, kstat_ref, o_ref, *, tk, exact_max):
    tq = q_ref.shape[0]
    s_len = k_ref.shape[0]
    heads = q_ref.shape[1] // HEAD_DIM
    q_all = q_ref[...]
    qs = [q_all[:, g * HEAD_DIM:(g + 1) * HEAD_DIM] for g in range(heads)]
    if exact_max:
        def max_body(i, ms):
            k_blk = k_ref[pl.ds(pl.multiple_of(i * tk, tk), tk), :]
            return tuple(jnp.maximum(ms[g], jnp.max(_scores(k_blk, qs[g]), axis=0, keepdims=True))
                         for g in range(heads))
        offs = lax.fori_loop(0, s_len // tk, max_body,
                             tuple(jnp.full((1, tq), -jnp.inf, F32) for _ in range(heads)))
    else:
        hk = pl.program_id(1)
        lane = lax.broadcasted_iota(jnp.int32, (1, LANES), 1)
        kstat = jnp.max(kstat_ref[...], axis=0, keepdims=True)
        kmax2 = jnp.max(jnp.where(lane // HEAD_DIM == hk, kstat, 0.0), axis=1, keepdims=True)
        ones = jnp.ones((8, HEAD_DIM), BF16)
        offs = []
        for g in range(heads):
            qf = qs[g].astype(F32)
            qq = _scores(ones, (qf * qf).astype(BF16))
            offs.append(jnp.sqrt(qq[0:1] * kmax2))

    l8 = [jnp.zeros((8, tq), F32) for _ in range(heads)]
    acc = [jnp.zeros((HEAD_DIM, tq), F32) for _ in range(heads)]
    for c in range(s_len // tk):
        k_blk = k_ref[c * tk:(c + 1) * tk, :]
        vt_blk = vt_ref[:, c * tk:(c + 1) * tk]
        for g in range(heads):
            p = jnp.exp2(_scores(k_blk, qs[g]) - offs[g])
            l8[g] = l8[g] + jnp.sum(p.reshape(tk // 8, 8, tq), axis=0)
            acc[g] = acc[g] + _dot(vt_blk, p.astype(BF16))
    out_t = jnp.concatenate(
        [acc[g] / jnp.sum(l8[g], axis=0, keepdims=True) for g in range(heads)], axis=0)
    o_ref[...] = out_t.T.astype(o_ref.dtype)


def _attn_call(q, k, vt, kstat, token_start, n_seq, s_len, tq, tk, tt, exact_max):
    heads_per_step = 2
    q_cols = heads_per_step * HEAD_DIM
    steps_per_kv = (N_Q_HEADS // N_KV_HEADS) // heads_per_step
    q_blk0 = token_start // tq
    seq0 = token_start // s_len
    nq = s_len // tq
    stat_rows = s_len // tt * STAT_ROWS

    def q_idx(b, hk, gp, j):
        return (q_blk0 + b * nq + j, hk * steps_per_kv + gp)

    return pl.pallas_call(
        functools.partial(_attn_kernel, tk=tk, exact_max=exact_max),
        grid=(n_seq, N_KV_HEADS, steps_per_kv, nq),
        in_specs=[
            pl.BlockSpec((tq, q_cols), q_idx),
            pl.BlockSpec((None, s_len, HEAD_DIM), lambda b, hk, gp, j: (hk, seq0 + b, 0)),
            pl.BlockSpec((HEAD_DIM, s_len), lambda b, hk, gp, j: (hk, seq0 + b)),
            pl.BlockSpec((stat_rows, KV_WIDTH), lambda b, hk, gp, j: (seq0 + b, 0)),
        ],
        out_specs=pl.BlockSpec((tq, q_cols),
                               lambda b, hk, gp, j: (b * nq + j, hk * steps_per_kv + gp)),
        out_shape=jax.ShapeDtypeStruct((n_seq * s_len, ATTN_WIDTH), BF16),
        compiler_params=_compiler_params(4),
        name="attn_exact_max" if exact_max else "attn",
    )(q, k, vt, kstat)


def _group_partner(x, lane, d, group):
    unit = group // 4
    pos = (lane % group) // unit
    return jnp.where(pos + d < 4,
                     pltpu.roll(x, LANES - d * unit, 1),
                     pltpu.roll(x, (4 - d) * unit, 1))


def _route(scores, bias, lane):
    valid = lane < N_EXPERTS
    biased = scores + bias
    rank = jnp.zeros(biased.shape, jnp.int32)
    pos = lane % EXPERTS_PER_GROUP
    for d in range(1, EXPERTS_PER_GROUP):
        other = _group_partner(biased, lane, d, EXPERTS_PER_GROUP)
        other_is_lower = (pos + d) >= EXPERTS_PER_GROUP
        beats = (other > biased) | ((other == biased) & other_is_lower)
        rank = rank + beats.astype(jnp.int32)
    top2 = rank < 2
    contrib = jnp.where(top2, biased, 0.0)
    group_score = contrib
    for d in range(1, EXPERTS_PER_GROUP):
        group_score = group_score + _group_partner(contrib, lane, d, EXPERTS_PER_GROUP)
    n_groups = N_EXPERTS // EXPERTS_PER_GROUP
    gpos = lane // EXPERTS_PER_GROUP
    losses = jnp.zeros(biased.shape, jnp.int32)
    for d in range(1, n_groups):
        other = _group_partner(group_score, lane, d, N_EXPERTS)
        other_is_lower = (gpos + d) >= n_groups
        beats = (other > group_score) | ((other == group_score) & other_is_lower)
        losses = losses + beats.astype(jnp.int32)
    selected = top2 & (losses == 0) & valid
    w = jnp.where(selected, scores, 0.0)
    return w / jnp.sum(w, axis=-1, keepdims=True), selected


INFO_EXPERT, INFO_RANK, INFO_GATE = 0, 2, 4


def _post_kernel(l_ref, x_ref, a_ref, gm_ref, aon_ref, wo_ref, g1_ref, b1_ref,
                 wr_hi_ref, wr_lo_ref, rb_ref, x1_ref, info_ref, cnt_ref):
    del l_ref
    tt = x_ref.shape[0]
    a = _rms_norm(a_ref[...].astype(F32), aon_ref[...]).astype(BF16)
    mix = _dot(a, wo_ref[:ATTN_WIDTH, :]) + _dot(gm_ref[...], wo_ref[ATTN_WIDTH:, :])
    x1 = _layer_norm(DEEPNORM_ALPHA * x_ref[...] + mix, g1_ref[...], b1_ref[...])
    x1_ref[...] = x1
    x_hi = x1.astype(BF16)
    x_lo = (x1 - x_hi.astype(F32)).astype(BF16)
    logits = (_dot(x_hi, wr_hi_ref[...]) + _dot(x_lo, wr_hi_ref[...])
              + _dot(x_hi, wr_lo_ref[...]))
    lane = lax.broadcasted_iota(jnp.int32, (1, LANES), 1)
    gates, selected = _route(jax.nn.sigmoid(logits), rb_ref[...], lane)

    sel = jnp.where(selected, 1.0, 0.0)
    tri = jnp.where(lax.broadcasted_iota(jnp.int32, (tt, tt), 0)
                    >= lax.broadcasted_iota(jnp.int32, (tt, tt), 1), 1.0, 0.0).astype(BF16)
    incl = _dot(tri, sel.astype(BF16))
    rank = incl - sel
    cnt_ref[...] = jnp.broadcast_to(incl[tt - 1:tt, :], cnt_ref.shape)

    lane_f = lane.astype(F32)
    e_lo = jnp.min(jnp.where(selected, lane_f, float(LANES)), axis=-1, keepdims=True)
    e_hi = jnp.max(jnp.where(selected, lane_f, -1.0), axis=-1, keepdims=True)

    def pick(mat, e):
        return jnp.sum(jnp.where(lane_f == e, mat, 0.0), axis=-1, keepdims=True)

    fields = {INFO_EXPERT: e_lo, INFO_EXPERT + 1: e_hi,
              INFO_RANK: pick(rank, e_lo), INFO_RANK + 1: pick(rank, e_hi),
              INFO_GATE: pick(gates, e_lo), INFO_GATE + 1: pick(gates, e_hi)}
    info = jnp.zeros((tt, LANES), F32)
    for col, val in fields.items():
        info = jnp.where(lane == col, val, info)
    info_ref[...] = info


def _post_call(layer, x, attn, gm, p, tt):
    n = x.shape[0]

    def tok(i, l):
        return (i, 0)

    def lay3(i, l):
        return (l[0], 0, 0)

    def const2(i, l):
        return (0, 0)

    grid_spec = pltpu.PrefetchScalarGridSpec(
        num_scalar_prefetch=1,
        grid=(n // tt,),
        in_specs=[
            pl.BlockSpec((tt, D_MODEL), tok),
            pl.BlockSpec((tt, ATTN_WIDTH), tok),
            pl.BlockSpec((tt, GMLP_WIDTH), tok),
            pl.BlockSpec((None, 1, ATTN_WIDTH), lay3),
            pl.BlockSpec((None, D_MODEL, D_MODEL), lay3),
            pl.BlockSpec((None, 1, D_MODEL), lay3),
            pl.BlockSpec((None, 1, D_MODEL), lay3),
            pl.BlockSpec((D_MODEL, LANES), const2),
            pl.BlockSpec((D_MODEL, LANES), const2),
            pl.BlockSpec((1, LANES), const2),
        ],
        out_specs=[pl.BlockSpec((tt, D_MODEL), tok), pl.BlockSpec((tt, LANES), tok),
                   pl.BlockSpec((STAT_ROWS, LANES), tok)],
    )
    return pl.pallas_call(
        _post_kernel,
        grid_spec=grid_spec,
        out_shape=[jax.ShapeDtypeStruct((n, D_MODEL), F32),
                   jax.ShapeDtypeStruct((n, LANES), F32),
                   jax.ShapeDtypeStruct((n // tt * STAT_ROWS, LANES), F32)],
        compiler_params=_compiler_params(1),
        name="post",
    )(layer, x, attn, gm, p["attn_out_norm"], p["w_out"], p["ln1_g"], p["ln1_b"],
      p["wr_hi"], p["wr_lo"], p["router_bias"])


def _plan_routes(info, cnt, tt, tm):
    n = info.shape[0]
    n_tiles = n // tt
    expert = info[:, INFO_EXPERT:INFO_EXPERT + 2].astype(jnp.int32)
    rank = info[:, INFO_RANK:INFO_RANK + 2].astype(jnp.int32)
    counts = cnt.reshape(n_tiles, STAT_ROWS, LANES)[:, 0, :N_EXPERTS].astype(jnp.int32)
    totals = jnp.sum(counts, axis=0)
    seg_tiles = (totals + tm - 1) // tm
    seg_end = jnp.cumsum(seg_tiles)
    expert_row0 = (seg_end - seg_tiles) * tm
    tile_base = expert_row0[None, :] + jnp.cumsum(counts, axis=0) - counts
    pos = jnp.take_along_axis(tile_base[jnp.arange(n) // tt], expert, axis=1) + rank
    n_row_tiles = 2 * n // tm + N_EXPERTS
    used = seg_end[-1]
    t_idx = jnp.arange(n_row_tiles, dtype=jnp.int32)
    tile_blk = jnp.minimum(t_idx, used - 1)
    tile_expert = jnp.minimum(jnp.searchsorted(seg_end, tile_blk, side="right"), N_EXPERTS - 1)
    return {
        "pos": pos.reshape(n_tiles, 1, 2 * tt).astype(jnp.int32),
        "pads": jnp.stack([jnp.append(expert_row0 + totals, used),
                           jnp.append(seg_tiles * tm - totals, n_row_tiles - used)]).astype(jnp.int32),
        "tile_expert": tile_expert.astype(jnp.int32),
        "tile_valid": (t_idx < used).astype(jnp.int32),
        "tile_blk": tile_blk.astype(jnp.int32),
        "n_rows": n_row_tiles * tm,
    }


ROW_SLAB = 8
assert ROW_SLAB * LANES == D_MODEL


def _slab_copy(src_ref, src_row, dst_ref, dst_row, sem):
    src = src_ref.at[pl.ds(pl.multiple_of(src_row * ROW_SLAB, ROW_SLAB), ROW_SLAB)]
    dst = dst_ref.at[pl.ds(pl.multiple_of(dst_row * ROW_SLAB, ROW_SLAB), ROW_SLAB)]
    return pltpu.make_async_copy(src, dst, sem)


def _rows_to_slabs(x, slab_ref):
    rows = x.shape[0]
    for s in range(ROW_SLAB):
        slab_ref[pl.ds(s, rows, stride=ROW_SLAB), :] = x[:, s * LANES:(s + 1) * LANES]


def _slabs_to_rows(slab_ref, rows):
    return jnp.concatenate(
        [slab_ref[pl.ds(s, rows, stride=ROW_SLAB), :] for s in range(ROW_SLAB)], axis=1)


DMA_UNROLL = 8


def _scatter_kernel(pads_ref, pos_ref, x_ref, xs_ref, slab_ref, zero_ref, sem):
    tt = x_ref.shape[0]

    @pl.when(pl.program_id(0) == 0)
    def _():
        zero_ref[...] = jnp.zeros_like(zero_ref)
        for e in range(N_EXPERTS):
            start, length = pads_ref[0, e], pads_ref[1, e]

            def fill(j, c, start=start):
                _slab_copy(zero_ref, 0, xs_ref, start + j, sem).start()
                return c

            def fill_wait(j, c):
                _slab_copy(zero_ref, 0, xs_ref, 0, sem).wait()
                return c

            lax.fori_loop(0, length, fill, 0)
            lax.fori_loop(0, length, fill_wait, 0)

        tile_slabs = zero_ref.shape[0]
        first_tail_tile, n_tail_tiles = pads_ref[0, N_EXPERTS], pads_ref[1, N_EXPERTS]

        def tail_copy(j):
            row0 = pl.multiple_of((first_tail_tile + j) * tile_slabs, tile_slabs)
            return pltpu.make_async_copy(zero_ref, xs_ref.at[pl.ds(row0, tile_slabs)], sem)

        def tail_fill(j, c):
            tail_copy(j).start()
            return c

        def tail_wait(j, c):
            tail_copy(j).wait()
            return c

        lax.fori_loop(0, n_tail_tiles, tail_fill, 0)
        lax.fori_loop(0, n_tail_tiles, tail_wait, 0)

    _rows_to_slabs(x_ref[...], slab_ref)

    def issue(r, c):
        for s in range(2):
            _slab_copy(slab_ref, r, xs_ref, pos_ref[0, 2 * r + s], sem).start()
        return c

    def drain(r, c):
        for s in range(2):
            _slab_copy(slab_ref, 0, xs_ref, 0, sem).wait()
        return c

    lax.fori_loop(0, tt, issue, 0, unroll=DMA_UNROLL)
    lax.fori_loop(0, tt, drain, 0, unroll=DMA_UNROLL)


def _scatter_call(x1, plan, tt, tm):
    n = x1.shape[0]
    grid_spec = pltpu.PrefetchScalarGridSpec(
        num_scalar_prefetch=1,
        grid=(n // tt,),
        in_specs=[
            pl.BlockSpec((None, 1, 2 * tt), lambda i, pads: (i, 0, 0), memory_space=pltpu.SMEM),
            pl.BlockSpec((tt, D_MODEL), lambda i, pads: (i, 0)),
        ],
        out_specs=pl.BlockSpec(memory_space=pl.ANY),
        scratch_shapes=[pltpu.VMEM((tt * ROW_SLAB, LANES), F32),
                        pltpu.VMEM((tm * ROW_SLAB, LANES), F32), pltpu.SemaphoreType.DMA],
    )
    return pl.pallas_call(
        _scatter_kernel,
        grid_spec=grid_spec,
        out_shape=jax.ShapeDtypeStruct((plan["n_rows"] * ROW_SLAB, LANES), F32),
        compiler_params=_compiler_params(1),
        name="scatter",
    )(plan["pads"], plan["pos"], x1)


def _expert_kernel(l_ref, te_ref, tv_ref, tb_ref, xs_ref, wg_ref, wu_ref, wd_ref, o_ref):
    del l_ref, te_ref, tb_ref
    tm = xs_ref.shape[0] // ROW_SLAB

    valid = tv_ref[pl.program_id(0)] != 0

    @pl.when(valid)
    def _():
        xb = _slabs_to_rows(xs_ref, tm).astype(BF16)
        hidden = jax.nn.silu(_dot(xb, wg_ref[...])) * _dot(xb, wu_ref[...])
        _rows_to_slabs(_dot(hidden.astype(BF16), wd_ref[...]), o_ref)

    @pl.when(jnp.logical_not(valid))
    def _():
        o_ref[...] = jnp.zeros_like(o_ref)


def _expert_call(layer, xs, plan, p, tm):
    n_rows = xs.shape[0] // ROW_SLAB

    def rows(t, l, te, tv, tb):
        return (tb[t], 0)

    def weights(t, l, te, tv, tb):
        return (l[0], te[t], 0, 0)

    grid_spec = pltpu.PrefetchScalarGridSpec(
        num_scalar_prefetch=4,
        grid=(n_rows // tm,),
        in_specs=[
            pl.BlockSpec((tm * ROW_SLAB, LANES), rows),
            pl.BlockSpec((None, None, D_MODEL, EXPERT_FF), weights),
            pl.BlockSpec((None, None, D_MODEL, EXPERT_FF), weights),
            pl.BlockSpec((None, None, EXPERT_FF, D_MODEL), weights),
        ],
        out_specs=pl.BlockSpec((tm * ROW_SLAB, LANES), lambda t, l, te, tv, tb: (t, 0)),
    )
    return pl.pallas_call(
        _expert_kernel,
        grid_spec=grid_spec,
        out_shape=jax.ShapeDtypeStruct(xs.shape, F32),
        compiler_params=_compiler_params(1),
        name="experts",
    )(layer, plan["tile_expert"], plan["tile_valid"], plan["tile_blk"], xs,
      p["w_gate"], p["w_up"], p["w_down"])


def _combine_kernel(l_ref, pos_ref, x_ref, info_ref, os_ref, g2_ref, b2_ref, o_ref, buf_ref, sem):
    del l_ref
    tt = x_ref.shape[0]

    def issue(r, c):
        for s in range(2):
            _slab_copy(os_ref, pos_ref[0, 2 * r + s], buf_ref.at[s], r, sem).start()
        return c

    def drain(r, c):
        for s in range(2):
            _slab_copy(os_ref, 0, buf_ref.at[s], 0, sem).wait()
        return c

    lax.fori_loop(0, tt, issue, 0, unroll=DMA_UNROLL)
    lax.fori_loop(0, tt, drain, 0, unroll=DMA_UNROLL)
    info = info_ref[...]
    y = (info[:, INFO_GATE:INFO_GATE + 1] * _slabs_to_rows(buf_ref.at[0], tt)
         + info[:, INFO_GATE + 1:INFO_GATE + 2] * _slabs_to_rows(buf_ref.at[1], tt))
    o_ref[...] = _layer_norm(DEEPNORM_ALPHA * x_ref[...] + y, g2_ref[...], b2_ref[...])


def _combine_call(layer, x1, info, out_sorted, plan, p, tt):
    n = x1.shape[0]

    def tok(i, l):
        return (i, 0)

    def lay3(i, l):
        return (l[0], 0, 0)

    grid_spec = pltpu.PrefetchScalarGridSpec(
        num_scalar_prefetch=1,
        grid=(n // tt,),
        in_specs=[
            pl.BlockSpec((None, 1, 2 * tt), lambda i, l: (i, 0, 0), memory_space=pltpu.SMEM),
            pl.BlockSpec((tt, D_MODEL), tok),
            pl.BlockSpec((tt, LANES), tok),
            pl.BlockSpec(memory_space=pl.ANY),
            pl.BlockSpec((None, 1, D_MODEL), lay3),
            pl.BlockSpec((None, 1, D_MODEL), lay3),
        ],
        out_specs=pl.BlockSpec((tt, D_MODEL), tok),
        scratch_shapes=[pltpu.VMEM((2, tt * ROW_SLAB, LANES), F32), pltpu.SemaphoreType.DMA],
    )
    return pl.pallas_call(
        _combine_kernel,
        grid_spec=grid_spec,
        out_shape=jax.ShapeDtypeStruct((n, D_MODEL), F32),
        compiler_params=_compiler_params(1),
        name="combine",
    )(layer, plan["pos"], x1, info, out_sorted, p["ln2_g"], p["ln2_b"])


def _rope_tables(max_len):
    t = jnp.arange(max_len, dtype=jnp.int32)
    row = (t // GRID_W).astype(F32)
    col = (t % GRID_W).astype(F32)
    n_pairs_axis = HEAD_DIM // 4
    inv_freq = 1.0 / (ROPE_THETA ** (jnp.arange(n_pairs_axis, dtype=F32) / n_pairs_axis))
    ang = jnp.concatenate([row[:, None] * inv_freq, col[:, None] * inv_freq], -1)
    cos, sin = jnp.cos(ang), jnp.sin(ang)
    reps = LANES // HEAD_DIM
    cos128 = jnp.tile(jnp.concatenate([cos, cos], -1), (1, reps))
    sin128 = jnp.tile(jnp.concatenate([-sin, sin], -1), (1, reps))
    return cos128, sin128


def _prepare_params(max_len, w_in, q_norm, k_norm, gmlp_ln_g, gmlp_ln_b, w_spatial, b_spatial,
                    attn_out_norm, gmlp_out_norm, w_out, ln1_g, ln1_b, w_router, router_bias,
                    w_gate, w_up, w_down, ln2_g, ln2_b):
    depth = w_in.shape[0]
    perm64 = jnp.concatenate([jnp.arange(0, HEAD_DIM, 2), jnp.arange(1, HEAD_DIM, 2)])
    n_rot_heads = N_Q_HEADS + N_KV_HEADS
    rot_cols = (jnp.arange(n_rot_heads)[:, None] * HEAD_DIM + perm64[None, :]).reshape(-1)
    cols = jnp.concatenate([rot_cols, jnp.arange(n_rot_heads * HEAD_DIM, IN_WIDTH)])
    cos128, sin128 = _rope_tables(max_len)
    head_of_lane = jnp.arange(ATTN_WIDTH) // HEAD_DIM
    row3 = lambda a: a.reshape(depth, 1, -1).astype(F32)
    wr = jnp.zeros((D_MODEL, LANES), F32).at[:, :N_EXPERTS].set(w_router.astype(F32))
    wr_hi = wr.astype(BF16)
    max_offset = (1.01 * LOG2_E * HEAD_DIM ** 0.5
                  * jnp.max(jnp.abs(q_norm), axis=1) * jnp.max(jnp.abs(k_norm), axis=1))
    return {
        "attn_needs_exact_max": max_offset > MAX_SAFE_OFFSET,
        "w_in": w_in[:, :, cols].astype(BF16),
        "cos": cos128,
        "sin": sin128,
        "seg": (head_of_lane[:, None] == head_of_lane[None, :]).astype(BF16),
        "q_norm": row3(jnp.tile(q_norm[:, perm64], (1, N_Q_HEADS))),
        "k_norm": row3(jnp.tile(k_norm[:, perm64], (1, N_KV_HEADS))),
        "gmlp_ln_g": row3(gmlp_ln_g),
        "gmlp_ln_b": row3(gmlp_ln_b),
        "w_cat": w_spatial.transpose(0, 2, 1, 3).reshape(depth, CHUNK, N_GMLP_HEADS * CHUNK).astype(BF16),
        "b_sp": jnp.repeat(b_spatial.transpose(0, 2, 1), GMLP_HEAD_DIM, axis=2).astype(F32),
        "attn_out_norm": row3(attn_out_norm),
        "gmlp_out_norm": row3(gmlp_out_norm),
        "w_out": w_out.astype(BF16),
        "ln1_g": row3(ln1_g),
        "ln1_b": row3(ln1_b),
        "wr_hi": wr_hi,
        "wr_lo": (wr - wr_hi.astype(F32)).astype(BF16),
        "router_bias": jnp.zeros((1, LANES), F32).at[0, :N_EXPERTS].set(router_bias.astype(F32)),
        "w_gate": w_gate.astype(BF16),
        "w_up": w_up.astype(BF16),
        "w_down": w_down.astype(BF16),
        "ln2_g": row3(ln2_g),
        "ln2_b": row3(ln2_b),
    }


def _trunk(x_tokens, groups, p, depth, tt, tm, tq, tk):
    (_, n_seq0, s0), = groups[:1]
    n_prompt_blocks = (n_seq0 * s0) // tt
    prompt_period = s0 // tt
    sample_period = (groups[1][2] if len(groups) > 1 else s0) // tt
    periods = (n_prompt_blocks, prompt_period, sample_period)

    def layer_body(l, x):
        layer = jnp.reshape(l, (1,)).astype(jnp.int32)
        q, k, vt, gm, kstat = _proj_call(layer, x, p, tt, periods)

        def attention(exact_max):
            outs = [_attn_call(q, k, vt, kstat, start, n_seq, s_len, min(tq, s_len),
                               min(tk, s_len), tt, exact_max)
                    for (start, n_seq, s_len) in groups]
            return outs[0] if len(outs) == 1 else jnp.concatenate(outs, axis=0)

        attn = lax.cond(p["attn_needs_exact_max"][l],
                        functools.partial(attention, True), functools.partial(attention, False))
        x1, info, cnt = _post_call(layer, x, attn, gm, p, tt)
        plan = _plan_routes(info, cnt, tt, tm)
        xs = _scatter_call(x1, plan, tt, tm)
        out_sorted = _expert_call(layer, xs, plan, p, tm)
        return _combine_call(layer, x1, info, out_sorted, plan, p, tt)

    return lax.fori_loop(0, depth, layer_body, x_tokens)


def kernel(x_prompt, x_sample, w_in, q_norm, k_norm, gmlp_ln_g, gmlp_ln_b, w_spatial, b_spatial,
           attn_out_norm, gmlp_out_norm, w_out, ln1_g, ln1_b, w_router, router_bias,
           w_gate, w_up, w_down, ln2_g, ln2_b):
    bp, sp, d = x_prompt.shape
    bs, ss, _ = x_sample.shape
    assert sp % ss == 0 and (bp * sp) % ss == 0, "sample sequences must tile the prompt stream"
    p = _prepare_params(max(sp, ss), w_in, q_norm, k_norm, gmlp_ln_g, gmlp_ln_b, w_spatial,
                        b_spatial, attn_out_norm, gmlp_out_norm, w_out, ln1_g, ln1_b, w_router,
                        router_bias, w_gate, w_up, w_down, ln2_g, ln2_b)
    x = jnp.concatenate([x_prompt.reshape(bp * sp, d), x_sample.reshape(bs * ss, d)], axis=0)
    groups = ((0, bp, sp), (bp * sp, bs, ss))
    y = _trunk(x.astype(F32), groups, p, w_in.shape[0], tt=256, tm=512, tq=256, tk=512)
    return (y[:bp * sp].reshape(bp, sp, d), y[bp * sp:].reshape(bs, ss, d))
```

```python
import functools

import jax
import jax.numpy as jnp
from jax import lax
from jax.experimental import pallas as pl
from jax.experimental.pallas import tpu as pltpu

D_MODEL = 1024
DEPTH = 4
HEAD_DIM = 64
N_Q_HEADS = 8
N_KV_HEADS = 2
ATTN_WIDTH = N_Q_HEADS * HEAD_DIM
KV_WIDTH = N_KV_HEADS * HEAD_DIM
GMLP_WIDTH = 512
N_GMLP_HEADS = 8
GMLP_HEAD_DIM = 64
CHUNK = 128
IN_WIDTH = ATTN_WIDTH + 2 * KV_WIDTH + 2 * GMLP_WIDTH
N_EXPERTS = 16
EXPERTS_PER_GROUP = 4
EXPERT_FF = 512
ROPE_THETA = 10000.0
GRID_W = 64
EPS = 1e-6
DEEPNORM_ALPHA = float((2 * DEPTH) ** 0.25)
LOG2_E = 1.4426950408889634
STAT_ROWS = 8
MAX_SAFE_OFFSET = 60.0

LANES = 128
F32 = jnp.float32
BF16 = jnp.bfloat16
VMEM_LIMIT_BYTES = 48 * 1024 * 1024


def _compiler_params(n_grid_dims):
    return pltpu.CompilerParams(
        dimension_semantics=("arbitrary",) * n_grid_dims,
        vmem_limit_bytes=VMEM_LIMIT_BYTES,
    )


def _dot(a, b):
    return jnp.dot(a, b, preferred_element_type=F32)


def _layer_norm(x, g, b):
    mu = jnp.mean(x, axis=-1, keepdims=True)
    xc = x - mu
    var = jnp.mean(xc * xc, axis=-1, keepdims=True)
    return xc * lax.rsqrt(var + EPS) * g + b


def _rms_norm(x, g):
    return x * lax.rsqrt(jnp.mean(x * x, axis=-1, keepdims=True) + EPS) * g


def _head_rms_norm(x, seg, g):
    ssq = _dot((x * x).astype(BF16), seg)
    return x * lax.rsqrt(ssq * (1.0 / HEAD_DIM) + EPS) * g


def _rope_half_split(x, cos, sin_signed):
    first_half = (lax.broadcasted_iota(jnp.int32, (1, LANES), 1) % HEAD_DIM) < (HEAD_DIM // 2)
    cols = []
    for c in range(x.shape[1] // LANES):
        xc = x[:, c * LANES:(c + 1) * LANES]
        partner = jnp.where(first_half,
                            pltpu.roll(xc, LANES - HEAD_DIM // 2, 1),
                            pltpu.roll(xc, HEAD_DIM // 2, 1))
        cols.append(xc * cos + partner * sin_signed)
    return cols[0] if len(cols) == 1 else jnp.concatenate(cols, axis=1)


def _proj_kernel(l_ref, x_ref, w_in_ref, cos_ref, sin_ref, seg_ref, qn_ref, kn_ref,
                 lng_ref, lnb_ref, wcat_ref, bsp_ref, gon_ref,
                 q_ref, k_ref, vt_ref, gm_ref, kstat_ref):
    del l_ref
    tt = x_ref.shape[0]
    h = _dot(x_ref[...].astype(BF16), w_in_ref[...])
    cos = cos_ref[...]
    sin_signed = sin_ref[...]

    q = _head_rms_norm(h[:, :ATTN_WIDTH], seg_ref[...], qn_ref[...])
    q = _rope_half_split(q, cos, sin_signed) * (LOG2_E * HEAD_DIM ** -0.5)
    q_ref[...] = q.astype(BF16)

    k0 = ATTN_WIDTH
    seg_kv = seg_ref[:KV_WIDTH, :KV_WIDTH]
    k = _head_rms_norm(h[:, k0:k0 + KV_WIDTH], seg_kv, kn_ref[...])
    k = _rope_half_split(k, cos, sin_signed).astype(BF16)
    kf = k.astype(F32)
    k_sq = _dot((kf * kf).astype(BF16), seg_kv)
    kstat_ref[...] = jnp.broadcast_to(jnp.max(k_sq, axis=0, keepdims=True), kstat_ref.shape)
    for hk in range(N_KV_HEADS):
        k_ref[hk] = k[:, hk * HEAD_DIM:(hk + 1) * HEAD_DIM]

    v0 = k0 + KV_WIDTH
    vt_ref[...] = h[:, v0:v0 + KV_WIDTH].T.astype(BF16)

    u0 = v0 + KV_WIDTH
    u = jax.nn.gelu(h[:, u0:u0 + GMLP_WIDTH])
    vg = _layer_norm(jax.nn.gelu(h[:, u0 + GMLP_WIDTH:]), lng_ref[...], lnb_ref[...])
    lane_head = lax.broadcasted_iota(jnp.int32, (1, GMLP_WIDTH), 1) // GMLP_HEAD_DIM
    mixed = []
    for c in range(tt // CHUNK):
        vc = vg[c * CHUNK:(c + 1) * CHUNK].astype(BF16)
        stack = jnp.concatenate(
            [jnp.where(lane_head == hh, vc, jnp.zeros_like(vc)) for hh in range(N_GMLP_HEADS)],
            axis=0)
        mixed.append(_dot(wcat_ref[...], stack) + bsp_ref[...])
    mixed = mixed[0] if len(mixed) == 1 else jnp.concatenate(mixed, axis=0)
    gm_ref[...] = _rms_norm(u * mixed, gon_ref[...]).astype(BF16)


def _proj_call(layer, x, p, tt, rope_period_blocks):
    n = x.shape[0]
    n_prompt_blocks, prompt_period, sample_period = rope_period_blocks

    def tok(i, l):
        return (i, 0)

    def rope_idx(i, l):
        return (jnp.where(i < n_prompt_blocks, i % prompt_period, i % sample_period), 0)

    def lay3(i, l):
        return (l[0], 0, 0)

    def const2(i, l):
        return (0, 0)

    grid_spec = pltpu.PrefetchScalarGridSpec(
        num_scalar_prefetch=1,
        grid=(n // tt,),
        in_specs=[
            pl.BlockSpec((tt, D_MODEL), tok),
            pl.BlockSpec((None, D_MODEL, IN_WIDTH), lay3),
            pl.BlockSpec((tt, LANES), rope_idx),
            pl.BlockSpec((tt, LANES), rope_idx),
            pl.BlockSpec((ATTN_WIDTH, ATTN_WIDTH), const2),
            pl.BlockSpec((None, 1, ATTN_WIDTH), lay3),
            pl.BlockSpec((None, 1, KV_WIDTH), lay3),
            pl.BlockSpec((None, 1, GMLP_WIDTH), lay3),
            pl.BlockSpec((None, 1, GMLP_WIDTH), lay3),
            pl.BlockSpec((None, CHUNK, N_GMLP_HEADS * CHUNK), lay3),
            pl.BlockSpec((None, CHUNK, GMLP_WIDTH), lay3),
            pl.BlockSpec((None, 1, GMLP_WIDTH), lay3),
        ],
        out_specs=[
            pl.BlockSpec((tt, ATTN_WIDTH), tok),
            pl.BlockSpec((N_KV_HEADS, tt, HEAD_DIM), lambda i, l: (0, i, 0)),
            pl.BlockSpec((KV_WIDTH, tt), lambda i, l: (0, i)),
            pl.BlockSpec((tt, GMLP_WIDTH), tok),
            pl.BlockSpec((STAT_ROWS, KV_WIDTH), tok),
        ],
    )
    return pl.pallas_call(
        _proj_kernel,
        grid_spec=grid_spec,
        out_shape=[
            jax.ShapeDtypeStruct((n, ATTN_WIDTH), BF16),
            jax.ShapeDtypeStruct((N_KV_HEADS, n, HEAD_DIM), BF16),
            jax.ShapeDtypeStruct((KV_WIDTH, n), BF16),
            jax.ShapeDtypeStruct((n, GMLP_WIDTH), BF16),
            jax.ShapeDtypeStruct((n // tt * STAT_ROWS, KV_WIDTH), F32),
        ],
        compiler_params=_compiler_params(1),
        name="proj",
    )(layer, x, p["w_in"], p["cos"], p["sin"], p["seg"], p["q_norm"], p["k_norm"],
      p["gmlp_ln_g"], p["gmlp_ln_b"], p["w_cat"], p["b_sp"], p["gmlp_out_norm"])


def _scores(k_blk, q):
    return lax.dot_general(k_blk, q, (((1,), (1,)), ((), ())), preferred_element_type=F32)


def _attn_kernel(q_ref, k_ref, vt_ref, kstat_ref, o_ref, *, tk, exact_max):
    tq = q_ref.shape[0]
    s_len = k_ref.shape[0]
    heads = q_ref.shape[1] // HEAD_DIM
    q_all = q_ref[...]
    q = jnp.concatenate([q_all[:, g * HEAD_DIM:(g + 1) * HEAD_DIM] for g in range(heads)], axis=0)
    nq = heads * tq
    if exact_max:
        def max_body(i, m):
            k_blk = k_ref[pl.ds(pl.multiple_of(i * tk, tk), tk), :]
            return jnp.maximum(m, jnp.max(_scores(k_blk, q), axis=0, keepdims=True))
        off = lax.fori_loop(0, s_len // tk, max_body, jnp.full((1, nq), -jnp.inf, F32))
    else:
        hk = pl.program_id(1)
        lane = lax.broadcasted_iota(jnp.int32, (1, LANES), 1)
        kstat = jnp.max(kstat_ref[...], axis=0, keepdims=True)
        kmax2 = jnp.max(jnp.where(lane // HEAD_DIM == hk, kstat, 0.0), axis=1, keepdims=True)
        qf = q.astype(F32)
        qq = _scores(jnp.ones((8, HEAD_DIM), BF16), (qf * qf).astype(BF16))
        off = jnp.sqrt(qq[0:1] * kmax2)

    l8 = jnp.zeros((8, nq), F32)
    acc = jnp.zeros((HEAD_DIM, nq), F32)
    for c in range(s_len // tk):
        k_blk = k_ref[c * tk:(c + 1) * tk, :]
        vt_blk = vt_ref[:, c * tk:(c + 1) * tk]
        p = jnp.exp2(_scores(k_blk, q) - off)
        l8 = l8 + jnp.sum(p.reshape(tk // 8, 8, nq), axis=0)
        acc = acc + _dot(vt_blk, p.astype(BF16))
    out = acc / jnp.sum(l8, axis=0, keepdims=True)
    out_t = jnp.concatenate([out[:, g * tq:(g + 1) * tq] for g in range(heads)], axis=0)
    o_ref[...] = out_t.T.astype(o_ref.dtype)


def _attn_call(q, k, vt, kstat, token_start, n_seq, s_len, tq, tk, tt, exact_max):
    heads_per_step = 4
    q_cols = heads_per_step * HEAD_DIM
    steps_per_kv = (N_Q_HEADS // N_KV_HEADS) // heads_per_step
    q_blk0 = token_start // tq
    seq0 = token_start // s_len
    nq = s_len // tq
    stat_rows = s_len // tt * STAT_ROWS

    def q_idx(b, hk, gp, j):
        return (q_blk0 + b * nq + j, hk * steps_per_kv + gp)

    return pl.pallas_call(
        functools.partial(_attn_kernel, tk=tk, exact_max=exact_max),
        grid=(n_seq, N_KV_HEADS, steps_per_kv, nq),
        in_specs=[
            pl.BlockSpec((tq, q_cols), q_idx),
            pl.BlockSpec((None, s_len, HEAD_DIM), lambda b, hk, gp, j: (hk, seq0 + b, 0)),
            pl.BlockSpec((HEAD_DIM, s_len), lambda b, hk, gp, j: (hk, seq0 + b)),
            pl.BlockSpec((stat_rows, KV_WIDTH), lambda b, hk, gp, j: (seq0 + b, 0)),
        ],
        out_specs=pl.BlockSpec((tq, q_cols),
                               lambda b, hk, gp, j: (b * nq + j, hk * steps_per_kv + gp)),
        out_shape=jax.ShapeDtypeStruct((n_seq * s_len, ATTN_WIDTH), BF16),
        compiler_params=_compiler_params(4),
        name="attn_exact_max" if exact_max else "attn",
    )(q, k, vt, kstat)


def _group_partner(x, lane, d, group):
    unit = group // 4
    pos = (lane % group) // unit
    return jnp.where(pos + d < 4,
                     pltpu.roll(x, LANES - d * unit, 1),
                     pltpu.roll(x, (4 - d) * unit, 1))


def _route(scores, bias, lane):
    valid = lane < N_EXPERTS
    biased = scores + bias
    rank = jnp.zeros(biased.shape, jnp.int32)
    pos = lane % EXPERTS_PER_GROUP
    for d in range(1, EXPERTS_PER_GROUP):
        other = _group_partner(biased, lane, d, EXPERTS_PER_GROUP)
        other_is_lower = (pos + d) >= EXPERTS_PER_GROUP
        beats = (other > biased) | ((other == biased) & other_is_lower)
        rank = rank + beats.astype(jnp.int32)
    top2 = rank < 2
    contrib = jnp.where(top2, biased, 0.0)
    group_score = contrib
    for d in range(1, EXPERTS_PER_GROUP):
        group_score = group_score + _group_partner(contrib, lane, d, EXPERTS_PER_GROUP)
    n_groups = N_EXPERTS // EXPERTS_PER_GROUP
    gpos = lane // EXPERTS_PER_GROUP
    losses = jnp.zeros(biased.shape, jnp.int32)
    for d in range(1, n_groups):
        other = _group_partner(group_score, lane, d, N_EXPERTS)
        other_is_lower = (gpos + d) >= n_groups
        beats = (other > group_score) | ((other == group_score) & other_is_lower)
        losses = losses + beats.astype(jnp.int32)
    selected = top2 & (losses == 0) & valid
    w = jnp.where(selected, scores, 0.0)
    return w / jnp.sum(w, axis=-1, keepdims=True), selected


INFO_EXPERT, INFO_RANK, INFO_GATE = 0, 2, 4


def _post_kernel(l_ref, x_ref, a_ref, gm_ref, aon_ref, wo_ref, g1_ref, b1_ref,
                 wr_hi_ref, wr_lo_ref, rb_ref, x1_ref, info_ref, cnt_ref):
    del l_ref
    tt = x_ref.shape[0]
    a = _rms_norm(a_ref[...].astype(F32), aon_ref[...]).astype(BF16)
    mix = _dot(a, wo_ref[:ATTN_WIDTH, :]) + _dot(gm_ref[...], wo_ref[ATTN_WIDTH:, :])
    x1 = _layer_norm(DEEPNORM_ALPHA * x_ref[...] + mix, g1_ref[...], b1_ref[...])
    x1_ref[...] = x1
    x_hi = x1.astype(BF16)
    x_lo = (x1 - x_hi.astype(F32)).astype(BF16)
    logits = (_dot(x_hi, wr_hi_ref[...]) + _dot(x_lo, wr_hi_ref[...])
              + _dot(x_hi, wr_lo_ref[...]))
    lane = lax.broadcasted_iota(jnp.int32, (1, LANES), 1)
    gates, selected = _route(jax.nn.sigmoid(logits), rb_ref[...], lane)

    sel = jnp.where(selected, 1.0, 0.0)
    tri = jnp.where(lax.broadcasted_iota(jnp.int32, (tt, tt), 0)
                    >= lax.broadcasted_iota(jnp.int32, (tt, tt), 1), 1.0, 0.0).astype(BF16)
    incl = _dot(tri, sel.astype(BF16))
    rank = incl - sel
    cnt_ref[...] = jnp.broadcast_to(incl[tt - 1:tt, :], cnt_ref.shape)

    lane_f = lane.astype(F32)
    e_lo = jnp.min(jnp.where(selected, lane_f, float(LANES)), axis=-1, keepdims=True)
    e_hi = jnp.max(jnp.where(selected, lane_f, -1.0), axis=-1, keepdims=True)

    def pick(mat, e):
        return jnp.sum(jnp.where(lane_f == e, mat, 0.0), axis=-1, keepdims=True)

    fields = {INFO_EXPERT: e_lo, INFO_EXPERT + 1: e_hi,
              INFO_RANK: pick(rank, e_lo), INFO_RANK + 1: pick(rank, e_hi),
              INFO_GATE: pick(gates, e_lo), INFO_GATE + 1: pick(gates, e_hi)}
    info = jnp.zeros((tt, LANES), F32)
    for col, val in fields.items():
        info = jnp.where(lane == col, val, info)
    info_ref[...] = info


def _post_call(layer, x, attn, gm, p, tt):
    n = x.shape[0]

    def tok(i, l):
        return (i, 0)

    def lay3(i, l):
        return (l[0], 0, 0)

    def const2(i, l):
        return (0, 0)

    grid_spec = pltpu.PrefetchScalarGridSpec(
        num_scalar_prefetch=1,
        grid=(n // tt,),
        in_specs=[
            pl.BlockSpec((tt, D_MODEL), tok),
            pl.BlockSpec((tt, ATTN_WIDTH), tok),
            pl.BlockSpec((tt, GMLP_WIDTH), tok),
            pl.BlockSpec((None, 1, ATTN_WIDTH), lay3),
            pl.BlockSpec((None, D_MODEL, D_MODEL), lay3),
            pl.BlockSpec((None, 1, D_MODEL), lay3),
            pl.BlockSpec((None, 1, D_MODEL), lay3),
            pl.BlockSpec((D_MODEL, LANES), const2),
            pl.BlockSpec((D_MODEL, LANES), const2),
            pl.BlockSpec((1, LANES), const2),
        ],
        out_specs=[pl.BlockSpec((tt, D_MODEL), tok), pl.BlockSpec((tt, LANES), tok),
                   pl.BlockSpec((STAT_ROWS, LANES), tok)],
    )
    return pl.pallas_call(
        _post_kernel,
        grid_spec=grid_spec,
        out_shape=[jax.ShapeDtypeStruct((n, D_MODEL), F32),
                   jax.ShapeDtypeStruct((n, LANES), F32),
                   jax.ShapeDtypeStruct((n // tt * STAT_ROWS, LANES), F32)],
        compiler_params=_compiler_params(1),
        name="post",
    )(layer, x, attn, gm, p["attn_out_norm"], p["w_out"], p["ln1_g"], p["ln1_b"],
      p["wr_hi"], p["wr_lo"], p["router_bias"])


def _plan_routes(info, cnt, tt, tm):
    n = info.shape[0]
    n_tiles = n // tt
    expert = info[:, INFO_EXPERT:INFO_EXPERT + 2].astype(jnp.int32)
    rank = info[:, INFO_RANK:INFO_RANK + 2].astype(jnp.int32)
    counts = cnt.reshape(n_tiles, STAT_ROWS, LANES)[:, 0, :N_EXPERTS].astype(jnp.int32)
    totals = jnp.sum(counts, axis=0)
    seg_tiles = (totals + tm - 1) // tm
    seg_end = jnp.cumsum(seg_tiles)
    expert_row0 = (seg_end - seg_tiles) * tm
    tile_base = expert_row0[None, :] + jnp.cumsum(counts, axis=0) - counts
    pos = jnp.take_along_axis(tile_base[jnp.arange(n) // tt], expert, axis=1) + rank
    n_row_tiles = 2 * n // tm + N_EXPERTS
    used = seg_end[-1]
    t_idx = jnp.arange(n_row_tiles, dtype=jnp.int32)
    tile_blk = jnp.minimum(t_idx, used - 1)
    tile_expert = jnp.minimum(jnp.searchsorted(seg_end, tile_blk, side="right"), N_EXPERTS - 1)
    return {
        "pos": pos.reshape(n_tiles, 1, 2 * tt).astype(jnp.int32),
        "pads": jnp.stack([jnp.append(expert_row0 + totals, used),
                           jnp.append(seg_tiles * tm - totals, n_row_tiles - used)]).astype(jnp.int32),
        "tile_expert": tile_expert.astype(jnp.int32),
        "tile_valid": (t_idx < used).astype(jnp.int32),
        "tile_blk": tile_blk.astype(jnp.int32),
        "n_rows": n_row_tiles * tm,
    }


ROW_SLAB = 8
assert ROW_SLAB * LANES == D_MODEL


def _slab_copy(src_ref, src_row, dst_ref, dst_row, sem):
    src = src_ref.at[pl.ds(pl.multiple_of(src_row * ROW_SLAB, ROW_SLAB), ROW_SLAB)]
    dst = dst_ref.at[pl.ds(pl.multiple_of(dst_row * ROW_SLAB, ROW_SLAB), ROW_SLAB)]
    return pltpu.make_async_copy(src, dst, sem)


def _rows_to_slabs(x, slab_ref):
    rows = x.shape[0]
    for s in range(ROW_SLAB):
        slab_ref[pl.ds(s, rows, stride=ROW_SLAB), :] = x[:, s * LANES:(s + 1) * LANES]


def _slabs_to_rows(slab_ref, rows):
    return jnp.concatenate(
        [slab_ref[pl.ds(s, rows, stride=ROW_SLAB), :] for s in range(ROW_SLAB)], axis=1)


DMA_UNROLL = 8


def _scatter_kernel(pads_ref, pos_ref, x_ref, xs_ref, slab_ref, zero_ref, sem):
    tt = x_ref.shape[0]

    @pl.when(pl.program_id(0) == 0)
    def _():
        zero_ref[...] = jnp.zeros_like(zero_ref)
        for e in range(N_EXPERTS):
            start, length = pads_ref[0, e], pads_ref[1, e]

            def fill(j, c, start=start):
                _slab_copy(zero_ref, 0, xs_ref, start + j, sem).start()
                return c

            def fill_wait(j, c):
                _slab_copy(zero_ref, 0, xs_ref, 0, sem).wait()
                return c

            lax.fori_loop(0, length, fill, 0)
            lax.fori_loop(0, length, fill_wait, 0)

        tile_slabs = zero_ref.shape[0]
        first_tail_tile, n_tail_tiles = pads_ref[0, N_EXPERTS], pads_ref[1, N_EXPERTS]

        def tail_copy(j):
            row0 = pl.multiple_of((first_tail_tile + j) * tile_slabs, tile_slabs)
            return pltpu.make_async_copy(zero_ref, xs_ref.at[pl.ds(row0, tile_slabs)], sem)

        def tail_fill(j, c):
            tail_copy(j).start()
            return c

        def tail_wait(j, c):
            tail_copy(j).wait()
            return c

        lax.fori_loop(0, n_tail_tiles, tail_fill, 0)
        lax.fori_loop(0, n_tail_tiles, tail_wait, 0)

    _rows_to_slabs(x_ref[...], slab_ref)

    def issue(r, c):
        for s in range(2):
            _slab_copy(slab_ref, r, xs_ref, pos_ref[0, 2 * r + s], sem).start()
        return c

    def drain(r, c):
        for s in range(2):
            _slab_copy(slab_ref, 0, xs_ref, 0, sem).wait()
        return c

    lax.fori_loop(0, tt, issue, 0, unroll=DMA_UNROLL)
    lax.fori_loop(0, tt, drain, 0, unroll=DMA_UNROLL)


def _scatter_call(x1, plan, tt, tm):
    n = x1.shape[0]
    grid_spec = pltpu.PrefetchScalarGridSpec(
        num_scalar_prefetch=1,
        grid=(n // tt,),
        in_specs=[
            pl.BlockSpec((None, 1, 2 * tt), lambda i, pads: (i, 0, 0), memory_space=pltpu.SMEM),
            pl.BlockSpec((tt, D_MODEL), lambda i, pads: (i, 0)),
        ],
        out_specs=pl.BlockSpec(memory_space=pl.ANY),
        scratch_shapes=[pltpu.VMEM((tt * ROW_SLAB, LANES), F32),
                        pltpu.VMEM((tm * ROW_SLAB, LANES), F32), pltpu.SemaphoreType.DMA],
    )
    return pl.pallas_call(
        _scatter_kernel,
        grid_spec=grid_spec,
        out_shape=jax.ShapeDtypeStruct((plan["n_rows"] * ROW_SLAB, LANES), F32),
        compiler_params=_compiler_params(1),
        name="scatter",
    )(plan["pads"], plan["pos"], x1)


def _expert_kernel(l_ref, te_ref, tv_ref, tb_ref, xs_ref, wg_ref, wu_ref, wd_ref, o_ref):
    del l_ref, te_ref, tb_ref
    tm = xs_ref.shape[0] // ROW_SLAB

    valid = tv_ref[pl.program_id(0)] != 0

    @pl.when(valid)
    def _():
        xb = _slabs_to_rows(xs_ref, tm).astype(BF16)
        hidden = jax.nn.silu(_dot(xb, wg_ref[...])) * _dot(xb, wu_ref[...])
        _rows_to_slabs(_dot(hidden.astype(BF16), wd_ref[...]), o_ref)

    @pl.when(jnp.logical_not(valid))
    def _():
        o_ref[...] = jnp.zeros_like(o_ref)


def _expert_call(layer, xs, plan, p, tm):
    n_rows = xs.shape[0] // ROW_SLAB

    def rows(t, l, te, tv, tb):
        return (tb[t], 0)

    def weights(t, l, te, tv, tb):
        return (l[0], te[t], 0, 0)

    grid_spec = pltpu.PrefetchScalarGridSpec(
        num_scalar_prefetch=4,
        grid=(n_rows // tm,),
        in_specs=[
            pl.BlockSpec((tm * ROW_SLAB, LANES), rows),
            pl.BlockSpec((None, None, D_MODEL, EXPERT_FF), weights),
            pl.BlockSpec((None, None, D_MODEL, EXPERT_FF), weights),
            pl.BlockSpec((None, None, EXPERT_FF, D_MODEL), weights),
        ],
        out_specs=pl.BlockSpec((tm * ROW_SLAB, LANES), lambda t, l, te, tv, tb: (t, 0)),
    )
    return pl.pallas_call(
        _expert_kernel,
        grid_spec=grid_spec,
        out_shape=jax.ShapeDtypeStruct(xs.shape, F32),
        compiler_params=_compiler_params(1),
        name="experts",
    )(layer, plan["tile_expert"], plan["tile_valid"], plan["tile_blk"], xs,
      p["w_gate"], p["w_up"], p["w_down"])


def _combine_kernel(l_ref, pos_ref, x_ref, info_ref, os_ref, g2_ref, b2_ref, o_ref, buf_ref, sem):
    del l_ref
    tt = x_ref.shape[0]

    def issue(r, c):
        for s in range(2):
            _slab_copy(os_ref, pos_ref[0, 2 * r + s], buf_ref.at[s], r, sem).start()
        return c

    def drain(r, c):
        for s in range(2):
            _slab_copy(os_ref, 0, buf_ref.at[s], 0, sem).wait()
        return c

    lax.fori_loop(0, tt, issue, 0, unroll=DMA_UNROLL)
    lax.fori_loop(0, tt, drain, 0, unroll=DMA_UNROLL)
    info = info_ref[...]
    y = (info[:, INFO_GATE:INFO_GATE + 1] * _slabs_to_rows(buf_ref.at[0], tt)
         + info[:, INFO_GATE + 1:INFO_GATE + 2] * _slabs_to_rows(buf_ref.at[1], tt))
    o_ref[...] = _layer_norm(DEEPNORM_ALPHA * x_ref[...] + y, g2_ref[...], b2_ref[...])


def _combine_call(layer, x1, info, out_sorted, plan, p, tt):
    n = x1.shape[0]

    def tok(i, l):
        return (i, 0)

    def lay3(i, l):
        return (l[0], 0, 0)

    grid_spec = pltpu.PrefetchScalarGridSpec(
        num_scalar_prefetch=1,
        grid=(n // tt,),
        in_specs=[
            pl.BlockSpec((None, 1, 2 * tt), lambda i, l: (i, 0, 0), memory_space=pltpu.SMEM),
            pl.BlockSpec((tt, D_MODEL), tok),
            pl.BlockSpec((tt, LANES), tok),
            pl.BlockSpec(memory_space=pl.ANY),
            pl.BlockSpec((None, 1, D_MODEL), lay3),
            pl.BlockSpec((None, 1, D_MODEL), lay3),
        ],
        out_specs=pl.BlockSpec((tt, D_MODEL), tok),
        scratch_shapes=[pltpu.VMEM((2, tt * ROW_SLAB, LANES), F32), pltpu.SemaphoreType.DMA],
    )
    return pl.pallas_call(
        _combine_kernel,
        grid_spec=grid_spec,
        out_shape=jax.ShapeDtypeStruct((n, D_MODEL), F32),
        compiler_params=_compiler_params(1),
        name="combine",
    )(layer, plan["pos"], x1, info, out_sorted, p["ln2_g"], p["ln2_b"])


def _rope_tables(max_len):
    t = jnp.arange(max_len, dtype=jnp.int32)
    row = (t // GRID_W).astype(F32)
    col = (t % GRID_W).astype(F32)
    n_pairs_axis = HEAD_DIM // 4
    inv_freq = 1.0 / (ROPE_THETA ** (jnp.arange(n_pairs_axis, dtype=F32) / n_pairs_axis))
    ang = jnp.concatenate([row[:, None] * inv_freq, col[:, None] * inv_freq], -1)
    cos, sin = jnp.cos(ang), jnp.sin(ang)
    reps = LANES // HEAD_DIM
    cos128 = jnp.tile(jnp.concatenate([cos, cos], -1), (1, reps))
    sin128 = jnp.tile(jnp.concatenate([-sin, sin], -1), (1, reps))
    return cos128, sin128


def _prepare_params(max_len, w_in, q_norm, k_norm, gmlp_ln_g, gmlp_ln_b, w_spatial, b_spatial,
                    attn_out_norm, gmlp_out_norm, w_out, ln1_g, ln1_b, w_router, router_bias,
                    w_gate, w_up, w_down, ln2_g, ln2_b):
    depth = w_in.shape[0]
    perm64 = jnp.concatenate([jnp.arange(0, HEAD_DIM, 2), jnp.arange(1, HEAD_DIM, 2)])
    n_rot_heads = N_Q_HEADS + N_KV_HEADS
    rot_cols = (jnp.arange(n_rot_heads)[:, None] * HEAD_DIM + perm64[None, :]).reshape(-1)
    cols = jnp.concatenate([rot_cols, jnp.arange(n_rot_heads * HEAD_DIM, IN_WIDTH)])
    cos128, sin128 = _rope_tables(max_len)
    head_of_lane = jnp.arange(ATTN_WIDTH) // HEAD_DIM
    row3 = lambda a: a.reshape(depth, 1, -1).astype(F32)
    wr = jnp.zeros((D_MODEL, LANES), F32).at[:, :N_EXPERTS].set(w_router.astype(F32))
    wr_hi = wr.astype(BF16)
    max_offset = (1.01 * LOG2_E * HEAD_DIM ** 0.5
                  * jnp.max(jnp.abs(q_norm), axis=1) * jnp.max(jnp.abs(k_norm), axis=1))
    return {
        "attn_needs_exact_max": max_offset > MAX_SAFE_OFFSET,
        "w_in": w_in[:, :, cols].astype(BF16),
        "cos": cos128,
        "sin": sin128,
        "seg": (head_of_lane[:, None] == head_of_lane[None, :]).astype(BF16),
        "q_norm": row3(jnp.tile(q_norm[:, perm64], (1, N_Q_HEADS))),
        "k_norm": row3(jnp.tile(k_norm[:, perm64], (1, N_KV_HEADS))),
        "gmlp_ln_g": row3(gmlp_ln_g),
        "gmlp_ln_b": row3(gmlp_ln_b),
        "w_cat": w_spatial.transpose(0, 2, 1, 3).reshape(depth, CHUNK, N_GMLP_HEADS * CHUNK).astype(BF16),
        "b_sp": jnp.repeat(b_spatial.transpose(0, 2, 1), GMLP_HEAD_DIM, axis=2).astype(F32),
        "attn_out_norm": row3(attn_out_norm),
        "gmlp_out_norm": row3(gmlp_out_norm),
        "w_out": w_out.astype(BF16),
        "ln1_g": row3(ln1_g),
        "ln1_b": row3(ln1_b),
        "wr_hi": wr_hi,
        "wr_lo": (wr - wr_hi.astype(F32)).astype(BF16),
        "router_bias": jnp.zeros((1, LANES), F32).at[0, :N_EXPERTS].set(router_bias.astype(F32)),
        "w_gate": w_gate.astype(BF16),
        "w_up": w_up.astype(BF16),
        "w_down": w_down.astype(BF16),
        "ln2_g": row3(ln2_g),
        "ln2_b": row3(ln2_b),
    }


def _trunk(x_tokens, groups, p, depth, tt, tm, tq, tk):
    (_, n_seq0, s0), = groups[:1]
    n_prompt_blocks = (n_seq0 * s0) // tt
    prompt_period = s0 // tt
    sample_period = (groups[1][2] if len(groups) > 1 else s0) // tt
    periods = (n_prompt_blocks, prompt_period, sample_period)

    def layer_body(l, x):
        layer = jnp.reshape(l, (1,)).astype(jnp.int32)
        q, k, vt, gm, kstat = _proj_call(layer, x, p, tt, periods)

        def attention(exact_max):
            outs = [_attn_call(q, k, vt, kstat, start, n_seq, s_len, min(tq, s_len),
                               min(tk, s_len), tt, exact_max)
                    for (start, n_seq, s_len) in groups]
            return outs[0] if len(outs) == 1 else jnp.concatenate(outs, axis=0)

        attn = lax.cond(p["attn_needs_exact_max"][l],
                        functools.partial(attention, True), functools.partial(attention, False))
        x1, info, cnt = _post_call(layer, x, attn, gm, p, tt)
        plan = _plan_routes(info, cnt, tt, tm)
        xs = _scatter_call(x1, plan, tt, tm)
        out_sorted = _expert_call(layer, xs, plan, p, tm)
        return _combine_call(layer, x1, info, out_sorted, plan, p, tt)

    return lax.fori_loop(0, depth, layer_body, x_tokens)


def kernel(x_prompt, x_sample, w_in, q_norm, k_norm, gmlp_ln_g, gmlp_ln_b, w_spatial, b_spatial,
           attn_out_norm, gmlp_out_norm, w_out, ln1_g, ln1_b, w_router, router_bias,
           w_gate, w_up, w_down, ln2_g, ln2_b):
    bp, sp, d = x_prompt.shape
    bs, ss, _ = x_sample.shape
    assert sp % ss == 0 and (bp * sp) % ss == 0, "sample sequences must tile the prompt stream"
    p = _prepare_params(max(sp, ss), w_in, q_norm, k_norm, gmlp_ln_g, gmlp_ln_b, w_spatial,
                        b_spatial, attn_out_norm, gmlp_out_norm, w_out, ln1_g, ln1_b, w_router,
                        router_bias, w_gate, w_up, w_down, ln2_g, ln2_b)
    x = jnp.concatenate([x_prompt.reshape(bp * sp, d), x_sample.reshape(bs * ss, d)], axis=0)
    groups = ((0, bp, sp), (bp * sp, bs, ss))
    y = _trunk(x.astype(F32), groups, p, w_in.shape[0], tt=256, tm=512, tq=256, tk=512)
    return (y[:bp * sp].reshape(bp, sp, d), y[bp * sp:].reshape(bs, ss, d))
```

```python
import functools

import jax
import jax.numpy as jnp
from jax import lax
from jax.experimental import pallas as pl
from jax.experimental.pallas import tpu as pltpu

D_MODEL = 1024
DEPTH = 4
HEAD_DIM = 64
N_Q_HEADS = 8
N_KV_HEADS = 2
ATTN_WIDTH = N_Q_HEADS * HEAD_DIM
KV_WIDTH = N_KV_HEADS * HEAD_DIM
GMLP_WIDTH = 512
N_GMLP_HEADS = 8
GMLP_HEAD_DIM = 64
CHUNK = 128
IN_WIDTH = ATTN_WIDTH + 2 * KV_WIDTH + 2 * GMLP_WIDTH
N_EXPERTS = 16
EXPERTS_PER_GROUP = 4
EXPERT_FF = 512
ROPE_THETA = 10000.0
GRID_W = 64
EPS = 1e-6
DEEPNORM_ALPHA = float((2 * DEPTH) ** 0.25)
LOG2_E = 1.4426950408889634
STAT_ROWS = 8
MAX_SAFE_OFFSET = 60.0

LANES = 128
F32 = jnp.float32
BF16 = jnp.bfloat16
VMEM_LIMIT_BYTES = 48 * 1024 * 1024


def _compiler_params(n_grid_dims):
    return pltpu.CompilerParams(
        dimension_semantics=("arbitrary",) * n_grid_dims,
        vmem_limit_bytes=VMEM_LIMIT_BYTES,
    )


def _dot(a, b):
    return jnp.dot(a, b, preferred_element_type=F32)


def _layer_norm(x, g, b):
    mu = jnp.mean(x, axis=-1, keepdims=True)
    xc = x - mu
    var = jnp.mean(xc * xc, axis=-1, keepdims=True)
    return xc * lax.rsqrt(var + EPS) * g + b


def _rms_norm(x, g):
    return x * lax.rsqrt(jnp.mean(x * x, axis=-1, keepdims=True) + EPS) * g


def _head_rms_norm(x, seg, g):
    ssq = _dot((x * x).astype(BF16), seg)
    return x * lax.rsqrt(ssq * (1.0 / HEAD_DIM) + EPS) * g


def _rope_half_split(x, cos, sin_signed):
    first_half = (lax.broadcasted_iota(jnp.int32, (1, LANES), 1) % HEAD_DIM) < (HEAD_DIM // 2)
    cols = []
    for c in range(x.shape[1] // LANES):
        xc = x[:, c * LANES:(c + 1) * LANES]
        partner = jnp.where(first_half,
                            pltpu.roll(xc, LANES - HEAD_DIM // 2, 1),
                            pltpu.roll(xc, HEAD_DIM // 2, 1))
        cols.append(xc * cos + partner * sin_signed)
    return cols[0] if len(cols) == 1 else jnp.concatenate(cols, axis=1)


def _proj_kernel(l_ref, x_ref, w_in_ref, cos_ref, sin_ref, seg_ref, qn_ref, kn_ref,
                 lng_ref, lnb_ref, wcat_ref, bsp_ref, gon_ref,
                 q_ref, k_ref, vt_ref, gm_ref, kstat_ref):
    del l_ref
    tt = x_ref.shape[0]
    h = _dot(x_ref[...].astype(BF16), w_in_ref[...])
    cos = cos_ref[...]
    sin_signed = sin_ref[...]

    q = _head_rms_norm(h[:, :ATTN_WIDTH], seg_ref[...], qn_ref[...])
    q = _rope_half_split(q, cos, sin_signed) * (LOG2_E * HEAD_DIM ** -0.5)
    q_ref[...] = q.astype(BF16)

    k0 = ATTN_WIDTH
    seg_kv = seg_ref[:KV_WIDTH, :KV_WIDTH]
    k = _head_rms_norm(h[:, k0:k0 + KV_WIDTH], seg_kv, kn_ref[...])
    k = _rope_half_split(k, cos, sin_signed).astype(BF16)
    kf = k.astype(F32)
    k_sq = _dot((kf * kf).astype(BF16), seg_kv)
    kstat_ref[...] = jnp.broadcast_to(jnp.max(k_sq, axis=0, keepdims=True), kstat_ref.shape)
    for hk in range(N_KV_HEADS):
        k_ref[hk] = k[:, hk * HEAD_DIM:(hk + 1) * HEAD_DIM]

    v0 = k0 + KV_WIDTH
    vt_ref[...] = h[:, v0:v0 + KV_WIDTH].T.astype(BF16)

    u0 = v0 + KV_WIDTH
    u = jax.nn.gelu(h[:, u0:u0 + GMLP_WIDTH])
    vg = _layer_norm(jax.nn.gelu(h[:, u0 + GMLP_WIDTH:]), lng_ref[...], lnb_ref[...])
    lane_head = lax.broadcasted_iota(jnp.int32, (1, GMLP_WIDTH), 1) // GMLP_HEAD_DIM
    mixed = []
    for c in range(tt // CHUNK):
        vc = vg[c * CHUNK:(c + 1) * CHUNK].astype(BF16)
        stack = jnp.concatenate(
            [jnp.where(lane_head == hh, vc, jnp.zeros_like(vc)) for hh in range(N_GMLP_HEADS)],
            axis=0)
        mixed.append(_dot(wcat_ref[...], stack) + bsp_ref[...])
    mixed = mixed[0] if len(mixed) == 1 else jnp.concatenate(mixed, axis=0)
    gm_ref[...] = _rms_norm(u * mixed, gon_ref[...]).astype(BF16)


def _proj_call(layer, x, p, tt, rope_period_blocks):
    n = x.shape[0]
    n_prompt_blocks, prompt_period, sample_period = rope_period_blocks

    def tok(i, l):
        return (i, 0)

    def rope_idx(i, l):
        return (jnp.where(i < n_prompt_blocks, i % prompt_period, i % sample_period), 0)

    def lay3(i, l):
        return (l[0], 0, 0)

    def const2(i, l):
        return (0, 0)

    grid_spec = pltpu.PrefetchScalarGridSpec(
        num_scalar_prefetch=1,
        grid=(n // tt,),
        in_specs=[
            pl.BlockSpec((tt, D_MODEL), tok),
            pl.BlockSpec((None, D_MODEL, IN_WIDTH), lay3),
            pl.BlockSpec((tt, LANES), rope_idx),
            pl.BlockSpec((tt, LANES), rope_idx),
            pl.BlockSpec((ATTN_WIDTH, ATTN_WIDTH), const2),
            pl.BlockSpec((None, 1, ATTN_WIDTH), lay3),
            pl.BlockSpec((None, 1, KV_WIDTH), lay3),
            pl.BlockSpec((None, 1, GMLP_WIDTH), lay3),
            pl.BlockSpec((None, 1, GMLP_WIDTH), lay3),
            pl.BlockSpec((None, CHUNK, N_GMLP_HEADS * CHUNK), lay3),
            pl.BlockSpec((None, CHUNK, GMLP_WIDTH), lay3),
            pl.BlockSpec((None, 1, GMLP_WIDTH), lay3),
        ],
        out_specs=[
            pl.BlockSpec((tt, ATTN_WIDTH), tok),
            pl.BlockSpec((N_KV_HEADS, tt, HEAD_DIM), lambda i, l: (0, i, 0)),
            pl.BlockSpec((KV_WIDTH, tt), lambda i, l: (0, i)),
            pl.BlockSpec((tt, GMLP_WIDTH), tok),
            pl.BlockSpec((STAT_ROWS, KV_WIDTH), tok),
        ],
    )
    return pl.pallas_call(
        _proj_kernel,
        grid_spec=grid_spec,
        out_shape=[
            jax.ShapeDtypeStruct((n, ATTN_WIDTH), BF16),
            jax.ShapeDtypeStruct((N_KV_HEADS, n, HEAD_DIM), BF16),
            jax.ShapeDtypeStruct((KV_WIDTH, n), BF16),
            jax.ShapeDtypeStruct((n, GMLP_WIDTH), BF16),
            jax.ShapeDtypeStruct((n // tt * STAT_ROWS, KV_WIDTH), F32),
        ],
        compiler_params=_compiler_params(1),
        name="proj",
    )(layer, x, p["w_in"], p["cos"], p["sin"], p["seg"], p["q_norm"], p["k_norm"],
      p["gmlp_ln_g"], p["gmlp_ln_b"], p["w_cat"], p["b_sp"], p["gmlp_out_norm"])


def _scores(k_blk, q):
    return lax.dot_general(k_blk, q, (((1,), (1,)), ((), ())), preferred_element_type=F32)


def _attn_kernel(q_ref, k_ref, vt_ref, kstat_ref, o_ref, *, tk, exact_max):
    tq = q_ref.shape[0]
    s_len = k_ref.shape[0]
    heads = q_ref.shape[1] // HEAD_DIM
    q_all = q_ref[...]
    q = jnp.concatenate([q_all[:, g * HEAD_DIM:(g + 1) * HEAD_DIM] for g in range(heads)], axis=0)
    nq = heads * tq
    if exact_max:
        def max_body(i, m):
            k_blk = k_ref[pl.ds(pl.multiple_of(i * tk, tk), tk), :]
            return jnp.maximum(m, jnp.max(_scores(k_blk, q), axis=0, keepdims=True))
        off = lax.fori_loop(0, s_len // tk, max_body, jnp.full((1, nq), -jnp.inf, F32))
    else:
        hk = pl.program_id(1)
        lane = lax.broadcasted_iota(jnp.int32, (1, LANES), 1)
        kstat = jnp.max(kstat_ref[...], axis=0, keepdims=True)
        kmax2 = jnp.max(jnp.where(lane // HEAD_DIM == hk, kstat, 0.0), axis=1, keepdims=True)
        qf = q.astype(F32)
        qq = _scores(jnp.ones((8, HEAD_DIM), BF16), (qf * qf).astype(BF16))
        off = jnp.sqrt(qq[0:1] * kmax2)

    l8 = jnp.zeros((8, nq), F32)
    acc = jnp.zeros((HEAD_DIM, nq), F32)
    for c in range(s_len // tk):
        k_blk = k_ref[c * tk:(c + 1) * tk, :]
        vt_blk = vt_ref[:, c * tk:(c + 1) * tk]
        p = jnp.exp2(_scores(k_blk, q) - off)
        l8 = l8 + jnp.sum(p.reshape(tk // 8, 8, nq), axis=0)
        acc = acc + _dot(vt_blk, p.astype(BF16))
    out = acc / jnp.sum(l8, axis=0, keepdims=True)
    out_t = jnp.concatenate([out[:, g * tq:(g + 1) * tq] for g in range(heads)], axis=0)
    o_ref[...] = out_t.T.astype(o_ref.dtype)


def _attn_call(q, k, vt, kstat, token_start, n_seq, s_len, tq, tk, tt, exact_max):
    heads_per_step = 4
    q_cols = heads_per_step * HEAD_DIM
    steps_per_kv = (N_Q_HEADS // N_KV_HEADS) // heads_per_step
    q_blk0 = token_start // tq
    seq0 = token_start // s_len
    nq = s_len // tq
    stat_rows = s_len // tt * STAT_ROWS

    def q_idx(b, hk, gp, j):
        return (q_blk0 + b * nq + j, hk * steps_per_kv + gp)

    return pl.pallas_call(
        functools.partial(_attn_kernel, tk=tk, exact_max=exact_max),
        grid=(n_seq, N_KV_HEADS, steps_per_kv, nq),
        in_specs=[
            pl.BlockSpec((tq, q_cols), q_idx),
            pl.BlockSpec((None, s_len, HEAD_DIM), lambda b, hk, gp, j: (hk, seq0 + b, 0)),
            pl.BlockSpec((HEAD_DIM, s_len), lambda b, hk, gp, j: (hk, seq0 + b)),
            pl.BlockSpec((stat_rows, KV_WIDTH), lambda b, hk, gp, j: (seq0 + b, 0)),
        ],
        out_specs=pl.BlockSpec((tq, q_cols),
                               lambda b, hk, gp, j: (b * nq + j, hk * steps_per_kv + gp)),
        out_shape=jax.ShapeDtypeStruct((n_seq * s_len, ATTN_WIDTH), BF16),
        compiler_params=_compiler_params(4),
        name="attn_exact_max" if exact_max else "attn",
    )(q, k, vt, kstat)


def _group_partner(x, lane, d, group):
    unit = group // 4
    pos = (lane % group) // unit
    return jnp.where(pos + d < 4,
                     pltpu.roll(x, LANES - d * unit, 1),
                     pltpu.roll(x, (4 - d) * unit, 1))


def _route(scores, bias, lane):
    valid = lane < N_EXPERTS
    biased = scores + bias
    rank = jnp.zeros(biased.shape, jnp.int32)
    pos = lane % EXPERTS_PER_GROUP
    for d in range(1, EXPERTS_PER_GROUP):
        other = _group_partner(biased, lane, d, EXPERTS_PER_GROUP)
        other_is_lower = (pos + d) >= EXPERTS_PER_GROUP
        beats = (other > biased) | ((other == biased) & other_is_lower)
        rank = rank + beats.astype(jnp.int32)
    top2 = rank < 2
    contrib = jnp.where(top2, biased, 0.0)
    group_score = contrib
    for d in range(1, EXPERTS_PER_GROUP):
        group_score = group_score + _group_partner(contrib, lane, d, EXPERTS_PER_GROUP)
    n_groups = N_EXPERTS // EXPERTS_PER_GROUP
    gpos = lane // EXPERTS_PER_GROUP
    losses = jnp.zeros(biased.shape, jnp.int32)
    for d in range(1, n_groups):
        other = _group_partner(group_score, lane, d, N_EXPERTS)
        other_is_lower = (gpos + d) >= n_groups
        beats = (other > group_score) | ((other == group_score) & other_is_lower)
        losses = losses + beats.astype(jnp.int32)
    selected = top2 & (losses == 0) & valid
    w = jnp.where(selected, scores, 0.0)
    return w / jnp.sum(w, axis=-1, keepdims=True), selected


INFO_EXPERT, INFO_RANK, INFO_GATE = 0, 2, 4


def _post_kernel(l_ref, x_ref, a_ref, gm_ref, aon_ref, wo_ref, g1_ref, b1_ref,
                 wr_hi_ref, wr_lo_ref, rb_ref, x1_ref, info_ref, cnt_ref):
    del l_ref
    tt = x_ref.shape[0]
    a = _rms_norm(a_ref[...].astype(F32), aon_ref[...]).astype(BF16)
    mix = _dot(a, wo_ref[:ATTN_WIDTH, :]) + _dot(gm_ref[...], wo_ref[ATTN_WIDTH:, :])
    x1 = _layer_norm(DEEPNORM_ALPHA * x_ref[...] + mix, g1_ref[...], b1_ref[...])
    x1_ref[...] = x1
    x_hi = x1.astype(BF16)
    x_lo = (x1 - x_hi.astype(F32)).astype(BF16)
    logits = (_dot(x_hi, wr_hi_ref[...]) + _dot(x_lo, wr_hi_ref[...])
              + _dot(x_hi, wr_lo_ref[...]))
    lane = lax.broadcasted_iota(jnp.int32, (1, LANES), 1)
    gates, selected = _route(jax.nn.sigmoid(logits), rb_ref[...], lane)

    sel = jnp.where(selected, 1.0, 0.0)
    tri = jnp.where(lax.broadcasted_iota(jnp.int32, (tt, tt), 0)
                    >= lax.broadcasted_iota(jnp.int32, (tt, tt), 1), 1.0, 0.0).astype(BF16)
    incl = _dot(tri, sel.astype(BF16))
    rank = incl - sel
    cnt_ref[...] = jnp.broadcast_to(incl[tt - 1:tt, :], cnt_ref.shape)

    lane_f = lane.astype(F32)
    e_lo = jnp.min(jnp.where(selected, lane_f, float(LANES)), axis=-1, keepdims=True)
    e_hi = jnp.max(jnp.where(selected, lane_f, -1.0), axis=-1, keepdims=True)

    def pick(mat, e):
        return jnp.sum(jnp.where(lane_f == e, mat, 0.0), axis=-1, keepdims=True)

    fields = {INFO_EXPERT: e_lo, INFO_EXPERT + 1: e_hi,
              INFO_RANK: pick(rank, e_lo), INFO_RANK + 1: pick(rank, e_hi),
              INFO_GATE: pick(gates, e_lo), INFO_GATE + 1: pick(gates, e_hi)}
    info = jnp.zeros((tt, LANES), F32)
    for col, val in fields.items():
        info = jnp.where(lane == col, val, info)
    info_ref[...] = info


def _post_call(layer, x, attn, gm, p, tt):
    n = x.shape[0]

    def tok(i, l):
        return (i, 0)

    def lay3(i, l):
        return (l[0], 0, 0)

    def const2(i, l):
        return (0, 0)

    grid_spec = pltpu.PrefetchScalarGridSpec(
        num_scalar_prefetch=1,
        grid=(n // tt,),
        in_specs=[
            pl.BlockSpec((tt, D_MODEL), tok),
            pl.BlockSpec((tt, ATTN_WIDTH), tok),
            pl.BlockSpec((tt, GMLP_WIDTH), tok),
            pl.BlockSpec((None, 1, ATTN_WIDTH), lay3),
            pl.BlockSpec((None, D_MODEL, D_MODEL), lay3),
            pl.BlockSpec((None, 1, D_MODEL), lay3),
            pl.BlockSpec((None, 1, D_MODEL), lay3),
            pl.BlockSpec((D_MODEL, LANES), const2),
            pl.BlockSpec((D_MODEL, LANES), const2),
            pl.BlockSpec((1, LANES), const2),
        ],
        out_specs=[pl.BlockSpec((tt, D_MODEL), tok), pl.BlockSpec((tt, LANES), tok),
                   pl.BlockSpec((STAT_ROWS, LANES), tok)],
    )
    return pl.pallas_call(
        _post_kernel,
        grid_spec=grid_spec,
        out_shape=[jax.ShapeDtypeStruct((n, D_MODEL), F32),
                   jax.ShapeDtypeStruct((n, LANES), F32),
                   jax.ShapeDtypeStruct((n // tt * STAT_ROWS, LANES), F32)],
        compiler_params=_compiler_params(1),
        name="post",
    )(layer, x, attn, gm, p["attn_out_norm"], p["w_out"], p["ln1_g"], p["ln1_b"],
      p["wr_hi"], p["wr_lo"], p["router_bias"])


def _plan_routes(info, cnt, tt, tm):
    n = info.shape[0]
    n_tiles = n // tt
    expert = info[:, INFO_EXPERT:INFO_EXPERT + 2].astype(jnp.int32)
    rank = info[:, INFO_RANK:INFO_RANK + 2].astype(jnp.int32)
    counts = cnt.reshape(n_tiles, STAT_ROWS, LANES)[:, 0, :N_EXPERTS].astype(jnp.int32)
    totals = jnp.sum(counts, axis=0)
    seg_tiles = (totals + tm - 1) // tm
    seg_end = jnp.cumsum(seg_tiles)
    expert_row0 = (seg_end - seg_tiles) * tm
    tile_base = expert_row0[None, :] + jnp.cumsum(counts, axis=0) - counts
    is_expert = expert.reshape(n_tiles, tt, 2, 1) == jnp.arange(N_EXPERTS, dtype=jnp.int32)
    base = jnp.sum(jnp.where(is_expert, tile_base[:, None, None, :], 0), axis=-1)
    pos = base.reshape(n, 2) + rank
    n_row_tiles = 2 * n // tm + N_EXPERTS
    used = seg_end[-1]
    t_idx = jnp.arange(n_row_tiles, dtype=jnp.int32)
    tile_blk = jnp.minimum(t_idx, used - 1)
    tile_expert = jnp.minimum(
        jnp.sum((tile_blk[:, None] >= seg_end[None, :]).astype(jnp.int32), axis=1), N_EXPERTS - 1)
    return {
        "pos": pos.reshape(n_tiles, 1, 2 * tt).astype(jnp.int32),
        "pads": jnp.stack([jnp.append(expert_row0 + totals, used),
                           jnp.append(seg_tiles * tm - totals, n_row_tiles - used)]).astype(jnp.int32),
        "tile_expert": tile_expert.astype(jnp.int32),
        "tile_valid": (t_idx < used).astype(jnp.int32),
        "tile_blk": tile_blk.astype(jnp.int32),
        "n_rows": n_row_tiles * tm,
    }


ROW_SLAB = 8
assert ROW_SLAB * LANES == D_MODEL


def _slab_copy(src_ref, src_row, dst_ref, dst_row, sem):
    src = src_ref.at[pl.ds(pl.multiple_of(src_row * ROW_SLAB, ROW_SLAB), ROW_SLAB)]
    dst = dst_ref.at[pl.ds(pl.multiple_of(dst_row * ROW_SLAB, ROW_SLAB), ROW_SLAB)]
    return pltpu.make_async_copy(src, dst, sem)


def _rows_to_slabs(x, slab_ref):
    rows = x.shape[0]
    for s in range(ROW_SLAB):
        slab_ref[pl.ds(s, rows, stride=ROW_SLAB), :] = x[:, s * LANES:(s + 1) * LANES]


def _slabs_to_rows(slab_ref, rows):
    return jnp.concatenate(
        [slab_ref[pl.ds(s, rows, stride=ROW_SLAB), :] for s in range(ROW_SLAB)], axis=1)


DMA_UNROLL = 8


def _scatter_kernel(pads_ref, pos_ref, x_ref, xs_ref, slab_ref, zero_ref, sems):
    tt = x_ref.shape[0]
    i = pl.program_id(0)
    last = pl.num_programs(0) - 1
    slot = i % 2
    sem = sems.at[2]

    @pl.when(i == 0)
    def _():
        zero_ref[...] = jnp.zeros_like(zero_ref)
        for e in range(N_EXPERTS):
            start, length = pads_ref[0, e], pads_ref[1, e]

            def fill(j, c, start=start):
                _slab_copy(zero_ref, 0, xs_ref, start + j, sem).start()
                return c

            def fill_wait(j, c):
                _slab_copy(zero_ref, 0, xs_ref, 0, sem).wait()
                return c

            lax.fori_loop(0, length, fill, 0)
            lax.fori_loop(0, length, fill_wait, 0)

        tile_slabs = zero_ref.shape[0]
        first_tail_tile, n_tail_tiles = pads_ref[0, N_EXPERTS], pads_ref[1, N_EXPERTS]

        def tail_copy(j):
            row0 = pl.multiple_of((first_tail_tile + j) * tile_slabs, tile_slabs)
            return pltpu.make_async_copy(zero_ref, xs_ref.at[pl.ds(row0, tile_slabs)], sem)

        def tail_fill(j, c):
            tail_copy(j).start()
            return c

        def tail_wait(j, c):
            tail_copy(j).wait()
            return c

        lax.fori_loop(0, n_tail_tiles, tail_fill, 0)
        lax.fori_loop(0, n_tail_tiles, tail_wait, 0)

    _rows_to_slabs(x_ref[...], slab_ref.at[slot])

    def issue(r, c):
        for s in range(2):
            _slab_copy(slab_ref.at[slot], r, xs_ref, pos_ref[0, 2 * r + s], sems.at[slot]).start()
        return c

    def drain(which):
        def body(r, c):
            for s in range(2):
                _slab_copy(slab_ref.at[which], 0, xs_ref, 0, sems.at[which]).wait()
            return c
        lax.fori_loop(0, tt, body, 0, unroll=DMA_UNROLL)

    lax.fori_loop(0, tt, issue, 0, unroll=DMA_UNROLL)

    @pl.when(i > 0)
    def _():
        drain(1 - slot)

    @pl.when(i == last)
    def _():
        drain(slot)


def _scatter_call(x1, plan, tt, tm):
    n = x1.shape[0]
    grid_spec = pltpu.PrefetchScalarGridSpec(
        num_scalar_prefetch=1,
        grid=(n // tt,),
        in_specs=[
            pl.BlockSpec((None, 1, 2 * tt), lambda i, pads: (i, 0, 0), memory_space=pltpu.SMEM),
            pl.BlockSpec((tt, D_MODEL), lambda i, pads: (i, 0)),
        ],
        out_specs=pl.BlockSpec(memory_space=pl.ANY),
        scratch_shapes=[pltpu.VMEM((2, tt * ROW_SLAB, LANES), F32),
                        pltpu.VMEM((tm * ROW_SLAB, LANES), F32), pltpu.SemaphoreType.DMA((3,))],
    )
    return pl.pallas_call(
        _scatter_kernel,
        grid_spec=grid_spec,
        out_shape=jax.ShapeDtypeStruct((plan["n_rows"] * ROW_SLAB, LANES), F32),
        compiler_params=_compiler_params(1),
        name="scatter",
    )(plan["pads"], plan["pos"], x1)


def _expert_kernel(l_ref, te_ref, tv_ref, tb_ref, xs_ref, wg_ref, wu_ref, wd_ref, o_ref):
    del l_ref, te_ref, tb_ref
    tm = xs_ref.shape[0] // ROW_SLAB

    valid = tv_ref[pl.program_id(0)] != 0

    @pl.when(valid)
    def _():
        xb = _slabs_to_rows(xs_ref, tm).astype(BF16)
        hidden = jax.nn.silu(_dot(xb, wg_ref[...])) * _dot(xb, wu_ref[...])
        _rows_to_slabs(_dot(hidden.astype(BF16), wd_ref[...]), o_ref)

    @pl.when(jnp.logical_not(valid))
    def _():
        o_ref[...] = jnp.zeros_like(o_ref)


def _expert_call(layer, xs, plan, p, tm):
    n_rows = xs.shape[0] // ROW_SLAB

    def rows(t, l, te, tv, tb):
        return (tb[t], 0)

    def weights(t, l, te, tv, tb):
        return (l[0], te[t], 0, 0)

    grid_spec = pltpu.PrefetchScalarGridSpec(
        num_scalar_prefetch=4,
        grid=(n_rows // tm,),
        in_specs=[
            pl.BlockSpec((tm * ROW_SLAB, LANES), rows),
            pl.BlockSpec((None, None, D_MODEL, EXPERT_FF), weights),
            pl.BlockSpec((None, None, D_MODEL, EXPERT_FF), weights),
            pl.BlockSpec((None, None, EXPERT_FF, D_MODEL), weights),
        ],
        out_specs=pl.BlockSpec((tm * ROW_SLAB, LANES), lambda t, l, te, tv, tb: (t, 0)),
    )
    return pl.pallas_call(
        _expert_kernel,
        grid_spec=grid_spec,
        out_shape=jax.ShapeDtypeStruct(xs.shape, F32),
        compiler_params=_compiler_params(1),
        name="experts",
    )(layer, plan["tile_expert"], plan["tile_valid"], plan["tile_blk"], xs,
      p["w_gate"], p["w_up"], p["w_down"])


def _combine_kernel(l_ref, pos_ref, pos_next_ref, x_ref, info_ref, os_ref, g2_ref, b2_ref,
                    o_ref, buf_ref, sems):
    del l_ref
    tt = x_ref.shape[0]
    i = pl.program_id(0)
    slot = i % 2

    def gather(p_ref, which):
        def body(r, c):
            for s in range(2):
                _slab_copy(os_ref, p_ref[0, 2 * r + s], buf_ref.at[which, s], r,
                           sems.at[which]).start()
            return c
        lax.fori_loop(0, tt, body, 0, unroll=DMA_UNROLL)

    @pl.when(i == 0)
    def _():
        gather(pos_ref, 0)

    @pl.when(i + 1 < pl.num_programs(0))
    def _():
        gather(pos_next_ref, 1 - slot)

    def drain(r, c):
        for s in range(2):
            _slab_copy(os_ref, 0, buf_ref.at[slot, s], 0, sems.at[slot]).wait()
        return c

    lax.fori_loop(0, tt, drain, 0, unroll=DMA_UNROLL)
    info = info_ref[...]
    y = (info[:, INFO_GATE:INFO_GATE + 1] * _slabs_to_rows(buf_ref.at[slot, 0], tt)
         + info[:, INFO_GATE + 1:INFO_GATE + 2] * _slabs_to_rows(buf_ref.at[slot, 1], tt))
    o_ref[...] = _layer_norm(DEEPNORM_ALPHA * x_ref[...] + y, g2_ref[...], b2_ref[...])


def _combine_call(layer, x1, info, out_sorted, plan, p, tt):
    n = x1.shape[0]

    def tok(i, l):
        return (i, 0)

    def lay3(i, l):
        return (l[0], 0, 0)

    n_tiles = n // tt
    grid_spec = pltpu.PrefetchScalarGridSpec(
        num_scalar_prefetch=1,
        grid=(n_tiles,),
        in_specs=[
            pl.BlockSpec((None, 1, 2 * tt), lambda i, l: (i, 0, 0), memory_space=pltpu.SMEM),
            pl.BlockSpec((None, 1, 2 * tt), lambda i, l: (jnp.minimum(i + 1, n_tiles - 1), 0, 0),
                         memory_space=pltpu.SMEM),
            pl.BlockSpec((tt, D_MODEL), tok),
            pl.BlockSpec((tt, LANES), tok),
            pl.BlockSpec(memory_space=pl.ANY),
            pl.BlockSpec((None, 1, D_MODEL), lay3),
            pl.BlockSpec((None, 1, D_MODEL), lay3),
        ],
        out_specs=pl.BlockSpec((tt, D_MODEL), tok),
        scratch_shapes=[pltpu.VMEM((2, 2, tt * ROW_SLAB, LANES), F32),
                        pltpu.SemaphoreType.DMA((2,))],
    )
    return pl.pallas_call(
        _combine_kernel,
        grid_spec=grid_spec,
        out_shape=jax.ShapeDtypeStruct((n, D_MODEL), F32),
        compiler_params=_compiler_params(1),
        name="combine",
    )(layer, plan["pos"], plan["pos"], x1, info, out_sorted, p["ln2_g"], p["ln2_b"])


def _rope_tables(max_len):
    t = jnp.arange(max_len, dtype=jnp.int32)
    row = (t // GRID_W).astype(F32)
    col = (t % GRID_W).astype(F32)
    n_pairs_axis = HEAD_DIM // 4
    inv_freq = 1.0 / (ROPE_THETA ** (jnp.arange(n_pairs_axis, dtype=F32) / n_pairs_axis))
    ang = jnp.concatenate([row[:, None] * inv_freq, col[:, None] * inv_freq], -1)
    cos, sin = jnp.cos(ang), jnp.sin(ang)
    reps = LANES // HEAD_DIM
    cos128 = jnp.tile(jnp.concatenate([cos, cos], -1), (1, reps))
    sin128 = jnp.tile(jnp.concatenate([-sin, sin], -1), (1, reps))
    return cos128, sin128


def _prepare_params(max_len, w_in, q_norm, k_norm, gmlp_ln_g, gmlp_ln_b, w_spatial, b_spatial,
                    attn_out_norm, gmlp_out_norm, w_out, ln1_g, ln1_b, w_router, router_bias,
                    w_gate, w_up, w_down, ln2_g, ln2_b):
    depth = w_in.shape[0]
    perm64 = jnp.concatenate([jnp.arange(0, HEAD_DIM, 2), jnp.arange(1, HEAD_DIM, 2)])
    n_rot_heads = N_Q_HEADS + N_KV_HEADS
    rot_cols = (jnp.arange(n_rot_heads)[:, None] * HEAD_DIM + perm64[None, :]).reshape(-1)
    cols = jnp.concatenate([rot_cols, jnp.arange(n_rot_heads * HEAD_DIM, IN_WIDTH)])
    cos128, sin128 = _rope_tables(max_len)
    head_of_lane = jnp.arange(ATTN_WIDTH) // HEAD_DIM
    row3 = lambda a: a.reshape(depth, 1, -1).astype(F32)
    wr = jnp.zeros((D_MODEL, LANES), F32).at[:, :N_EXPERTS].set(w_router.astype(F32))
    wr_hi = wr.astype(BF16)
    max_offset = (1.01 * LOG2_E * HEAD_DIM ** 0.5
                  * jnp.max(jnp.abs(q_norm), axis=1) * jnp.max(jnp.abs(k_norm), axis=1))
    return {
        "attn_needs_exact_max": max_offset > MAX_SAFE_OFFSET,
        "w_in": w_in[:, :, cols].astype(BF16),
        "cos": cos128,
        "sin": sin128,
        "seg": (head_of_lane[:, None] == head_of_lane[None, :]).astype(BF16),
        "q_norm": row3(jnp.tile(q_norm[:, perm64], (1, N_Q_HEADS))),
        "k_norm": row3(jnp.tile(k_norm[:, perm64], (1, N_KV_HEADS))),
        "gmlp_ln_g": row3(gmlp_ln_g),
        "gmlp_ln_b": row3(gmlp_ln_b),
        "w_cat": w_spatial.transpose(0, 2, 1, 3).reshape(depth, CHUNK, N_GMLP_HEADS * CHUNK).astype(BF16),
        "b_sp": jnp.repeat(b_spatial.transpose(0, 2, 1), GMLP_HEAD_DIM, axis=2).astype(F32),
        "attn_out_norm": row3(attn_out_norm),
        "gmlp_out_norm": row3(gmlp_out_norm),
        "w_out": w_out.astype(BF16),
        "ln1_g": row3(ln1_g),
        "ln1_b": row3(ln1_b),
        "wr_hi": wr_hi,
        "wr_lo": (wr - wr_hi.astype(F32)).astype(BF16),
        "router_bias": jnp.zeros((1, LANES), F32).at[0, :N_EXPERTS].set(router_bias.astype(F32)),
        "w_gate": w_gate.astype(BF16),
        "w_up": w_up.astype(BF16),
        "w_down": w_down.astype(BF16),
        "ln2_g": row3(ln2_g),
        "ln2_b": row3(ln2_b),
    }


def _trunk(x_tokens, groups, p, depth, tt, tm, tq, tk):
    (_, n_seq0, s0), = groups[:1]
    n_prompt_blocks = (n_seq0 * s0) // tt
    prompt_period = s0 // tt
    sample_period = (groups[1][2] if len(groups) > 1 else s0) // tt
    periods = (n_prompt_blocks, prompt_period, sample_period)

    def layer_body(l, x):
        layer = jnp.reshape(l, (1,)).astype(jnp.int32)
        q, k, vt, gm, kstat = _proj_call(layer, x, p, tt, periods)

        def attention(exact_max):
            outs = [_attn_call(q, k, vt, kstat, start, n_seq, s_len, min(tq, s_len),
                               min(tk, s_len), tt, exact_max)
                    for (start, n_seq, s_len) in groups]
            return outs[0] if len(outs) == 1 else jnp.concatenate(outs, axis=0)

        attn = lax.cond(p["attn_needs_exact_max"][l],
                        functools.partial(attention, True), functools.partial(attention, False))
        x1, info, cnt = _post_call(layer, x, attn, gm, p, tt)
        plan = _plan_routes(info, cnt, tt, tm)
        xs = _scatter_call(x1, plan, tt, tm)
        out_sorted = _expert_call(layer, xs, plan, p, tm)
        return _combine_call(layer, x1, info, out_sorted, plan, p, tt)

    return lax.fori_loop(0, depth, layer_body, x_tokens)


def kernel(x_prompt, x_sample, w_in, q_norm, k_norm, gmlp_ln_g, gmlp_ln_b, w_spatial, b_spatial,
           attn_out_norm, gmlp_out_norm, w_out, ln1_g, ln1_b, w_router, router_bias,
           w_gate, w_up, w_down, ln2_g, ln2_b):
    bp, sp, d = x_prompt.shape
    bs, ss, _ = x_sample.shape
    assert sp % ss == 0 and (bp * sp) % ss == 0, "sample sequences must tile the prompt stream"
    p = _prepare_params(max(sp, ss), w_in, q_norm, k_norm, gmlp_ln_g, gmlp_ln_b, w_spatial,
                        b_spatial, attn_out_norm, gmlp_out_norm, w_out, ln1_g, ln1_b, w_router,
                        router_bias, w_gate, w_up, w_down, ln2_g, ln2_b)
    x = jnp.concatenate([x_prompt.reshape(bp * sp, d), x_sample.reshape(bs * ss, d)], axis=0)
    groups = ((0, bp, sp), (bp * sp, bs, ss))
    y = _trunk(x.astype(F32), groups, p, w_in.shape[0], tt=256, tm=512, tq=256, tk=512)
    return (y[:bp * sp].reshape(bp, sp, d), y[bp * sp:].reshape(bs, ss, d))
```

```python
import functools

import jax
import jax.numpy as jnp
from jax import lax
from jax.experimental import pallas as pl
from jax.experimental.pallas import tpu as pltpu

D_MODEL = 1024
DEPTH = 4
HEAD_DIM = 64
N_Q_HEADS = 8
N_KV_HEADS = 2
ATTN_WIDTH = N_Q_HEADS * HEAD_DIM
KV_WIDTH = N_KV_HEADS * HEAD_DIM
GMLP_WIDTH = 512
N_GMLP_HEADS = 8
GMLP_HEAD_DIM = 64
CHUNK = 128
IN_WIDTH = ATTN_WIDTH + 2 * KV_WIDTH + 2 * GMLP_WIDTH
N_EXPERTS = 16
EXPERTS_PER_GROUP = 4
EXPERT_FF = 512
ROPE_THETA = 10000.0
GRID_W = 64
EPS = 1e-6
DEEPNORM_ALPHA = float((2 * DEPTH) ** 0.25)
LOG2_E = 1.4426950408889634
STAT_ROWS = 8
MAX_SAFE_OFFSET = 60.0

LANES = 128
F32 = jnp.float32
BF16 = jnp.bfloat16
VMEM_LIMIT_BYTES = 48 * 1024 * 1024


def _compiler_params(n_grid_dims):
    return pltpu.CompilerParams(
        dimension_semantics=("arbitrary",) * n_grid_dims,
        vmem_limit_bytes=VMEM_LIMIT_BYTES,
    )


def _dot(a, b):
    return jnp.dot(a, b, preferred_element_type=F32)


def _layer_norm(x, g, b):
    mu = jnp.mean(x, axis=-1, keepdims=True)
    xc = x - mu
    var = jnp.mean(xc * xc, axis=-1, keepdims=True)
    return xc * lax.rsqrt(var + EPS) * g + b


def _rms_norm(x, g):
    return x * lax.rsqrt(jnp.mean(x * x, axis=-1, keepdims=True) + EPS) * g


def _head_rms_norm(x, seg, g):
    ssq = _dot((x * x).astype(BF16), seg)
    return x * lax.rsqrt(ssq * (1.0 / HEAD_DIM) + EPS) * g


def _rope_half_split(x, cos, sin_signed):
    first_half = (lax.broadcasted_iota(jnp.int32, (1, LANES), 1) % HEAD_DIM) < (HEAD_DIM // 2)
    cols = []
    for c in range(x.shape[1] // LANES):
        xc = x[:, c * LANES:(c + 1) * LANES]
        partner = jnp.where(first_half,
                            pltpu.roll(xc, LANES - HEAD_DIM // 2, 1),
                            pltpu.roll(xc, HEAD_DIM // 2, 1))
        cols.append(xc * cos + partner * sin_signed)
    return cols[0] if len(cols) == 1 else jnp.concatenate(cols, axis=1)


def _proj_kernel(l_ref, x_ref, w_in_ref, cos_ref, sin_ref, seg_ref, qn_ref, kn_ref,
                 lng_ref, lnb_ref, wcat_ref, bsp_ref, gon_ref,
                 q_ref, k_ref, vt_ref, gm_ref, kstat_ref):
    del l_ref
    tt = x_ref.shape[0]
    h = _dot(x_ref[...].astype(BF16), w_in_ref[...])
    cos = cos_ref[...]
    sin_signed = sin_ref[...]

    q = _head_rms_norm(h[:, :ATTN_WIDTH], seg_ref[...], qn_ref[...])
    q = _rope_half_split(q, cos, sin_signed) * (LOG2_E * HEAD_DIM ** -0.5)
    q_ref[...] = q.astype(BF16)

    k0 = ATTN_WIDTH
    seg_kv = seg_ref[:KV_WIDTH, :KV_WIDTH]
    k = _head_rms_norm(h[:, k0:k0 + KV_WIDTH], seg_kv, kn_ref[...])
    k = _rope_half_split(k, cos, sin_signed).astype(BF16)
    kf = k.astype(F32)
    k_sq = _dot((kf * kf).astype(BF16), seg_kv)
    kstat_ref[...] = jnp.broadcast_to(jnp.max(k_sq, axis=0, keepdims=True), kstat_ref.shape)
    for hk in range(N_KV_HEADS):
        k_ref[hk] = k[:, hk * HEAD_DIM:(hk + 1) * HEAD_DIM]

    v0 = k0 + KV_WIDTH
    vt_ref[...] = h[:, v0:v0 + KV_WIDTH].T.astype(BF16)

    u0 = v0 + KV_WIDTH
    u = jax.nn.gelu(h[:, u0:u0 + GMLP_WIDTH])
    vg = _layer_norm(jax.nn.gelu(h[:, u0 + GMLP_WIDTH:]), lng_ref[...], lnb_ref[...])
    lane_head = lax.broadcasted_iota(jnp.int32, (1, GMLP_WIDTH), 1) // GMLP_HEAD_DIM
    mixed = []
    for c in range(tt // CHUNK):
        vc = vg[c * CHUNK:(c + 1) * CHUNK].astype(BF16)
        stack = jnp.concatenate(
            [jnp.where(lane_head == hh, vc, jnp.zeros_like(vc)) for hh in range(N_GMLP_HEADS)],
            axis=0)
        mixed.append(_dot(wcat_ref[...], stack) + bsp_ref[...])
    mixed = mixed[0] if len(mixed) == 1 else jnp.concatenate(mixed, axis=0)
    gm_ref[...] = _rms_norm(u * mixed, gon_ref[...]).astype(BF16)


def _proj_call(layer, x, p, tt, rope_period_blocks):
    n = x.shape[0]
    n_prompt_blocks, prompt_period, sample_period = rope_period_blocks

    def tok(i, l):
        return (i, 0)

    def rope_idx(i, l):
        return (jnp.where(i < n_prompt_blocks, i % prompt_period, i % sample_period), 0)

    def lay3(i, l):
        return (l[0], 0, 0)

    def const2(i, l):
        return (0, 0)

    grid_spec = pltpu.PrefetchScalarGridSpec(
        num_scalar_prefetch=1,
        grid=(n // tt,),
        in_specs=[
            pl.BlockSpec((tt, D_MODEL), tok),
            pl.BlockSpec((None, D_MODEL, IN_WIDTH), lay3),
            pl.BlockSpec((tt, LANES), rope_idx),
            pl.BlockSpec((tt, LANES), rope_idx),
            pl.BlockSpec((ATTN_WIDTH, ATTN_WIDTH), const2),
            pl.BlockSpec((None, 1, ATTN_WIDTH), lay3),
            pl.BlockSpec((None, 1, KV_WIDTH), lay3),
            pl.BlockSpec((None, 1, GMLP_WIDTH), lay3),
            pl.BlockSpec((None, 1, GMLP_WIDTH), lay3),
            pl.BlockSpec((None, CHUNK, N_GMLP_HEADS * CHUNK), lay3),
            pl.BlockSpec((None, CHUNK, GMLP_WIDTH), lay3),
            pl.BlockSpec((None, 1, GMLP_WIDTH), lay3),
        ],
        out_specs=[
            pl.BlockSpec((tt, ATTN_WIDTH), tok),
            pl.BlockSpec((N_KV_HEADS, tt, HEAD_DIM), lambda i, l: (0, i, 0)),
            pl.BlockSpec((KV_WIDTH, tt), lambda i, l: (0, i)),
            pl.BlockSpec((tt, GMLP_WIDTH), tok),
            pl.BlockSpec((STAT_ROWS, KV_WIDTH), tok),
        ],
    )
    return pl.pallas_call(
        _proj_kernel,
        grid_spec=grid_spec,
        out_shape=[
            jax.ShapeDtypeStruct((n, ATTN_WIDTH), BF16),
            jax.ShapeDtypeStruct((N_KV_HEADS, n, HEAD_DIM), BF16),
            jax.ShapeDtypeStruct((KV_WIDTH, n), BF16),
            jax.ShapeDtypeStruct((n, GMLP_WIDTH), BF16),
            jax.ShapeDtypeStruct((n // tt * STAT_ROWS, KV_WIDTH), F32),
        ],
        compiler_params=_compiler_params(1),
        name="proj",
    )(layer, x, p["w_in"], p["cos"], p["sin"], p["seg"], p["q_norm"], p["k_norm"],
      p["gmlp_ln_g"], p["gmlp_ln_b"], p["w_cat"], p["b_sp"], p["gmlp_out_norm"])


def _scores(k_blk, q):
    return lax.dot_general(k_blk, q, (((1,), (1,)), ((), ())), preferred_element_type=F32)


def _attn_kernel(q_ref, k_ref, vt_ref, kstat_ref, o_ref, *, tk, exact_max):
    tq = q_ref.shape[0]
    s_len = k_ref.shape[0]
    heads = q_ref.shape[1] // HEAD_DIM
    q_all = q_ref[...]
    q = jnp.concatenate([q_all[:, g * HEAD_DIM:(g + 1) * HEAD_DIM] for g in range(heads)], axis=0)
    nq = heads * tq
    if exact_max:
        def max_body(i, m):
            k_blk = k_ref[pl.ds(pl.multiple_of(i * tk, tk), tk), :]
            return jnp.maximum(m, jnp.max(_scores(k_blk, q), axis=0, keepdims=True))
        off = lax.fori_loop(0, s_len // tk, max_body, jnp.full((1, nq), -jnp.inf, F32))
    else:
        hk = pl.program_id(1)
        lane = lax.broadcasted_iota(jnp.int32, (1, LANES), 1)
        kstat = jnp.max(kstat_ref[...], axis=0, keepdims=True)
        kmax2 = jnp.max(jnp.where(lane // HEAD_DIM == hk, kstat, 0.0), axis=1, keepdims=True)
        qf = q.astype(F32)
        qq = _scores(jnp.ones((8, HEAD_DIM), BF16), (qf * qf).astype(BF16))
        off = jnp.sqrt(qq[0:1] * kmax2)

    l8 = jnp.zeros((8, nq), F32)
    acc = jnp.zeros((HEAD_DIM, nq), F32)
    for c in range(s_len // tk):
        k_blk = k_ref[c * tk:(c + 1) * tk, :]
        vt_blk = vt_ref[:, c * tk:(c + 1) * tk]
        p = jnp.exp2(_scores(k_blk, q) - off)
        l8 = l8 + jnp.sum(p.reshape(tk // 8, 8, nq), axis=0)
        acc = acc + _dot(vt_blk, p.astype(BF16))
    out = acc / jnp.sum(l8, axis=0, keepdims=True)
    out_t = jnp.concatenate([out[:, g * tq:(g + 1) * tq] for g in range(heads)], axis=0)
    o_ref[...] = out_t.T.astype(o_ref.dtype)


def _attn_call(q, k, vt, kstat, token_start, n_seq, s_len, tq, tk, tt, exact_max):
    heads_per_step = 4
    q_cols = heads_per_step * HEAD_DIM
    steps_per_kv = (N_Q_HEADS // N_KV_HEADS) // heads_per_step
    q_blk0 = token_start // tq
    seq0 = token_start // s_len
    nq = s_len // tq
    stat_rows = s_len // tt * STAT_ROWS

    def q_idx(b, hk, gp, j):
        return (q_blk0 + b * nq + j, hk * steps_per_kv + gp)

    return pl.pallas_call(
        functools.partial(_attn_kernel, tk=tk, exact_max=exact_max),
        grid=(n_seq, N_KV_HEADS, steps_per_kv, nq),
        in_specs=[
            pl.BlockSpec((tq, q_cols), q_idx),
            pl.BlockSpec((None, s_len, HEAD_DIM), lambda b, hk, gp, j: (hk, seq0 + b, 0)),
            pl.BlockSpec((HEAD_DIM, s_len), lambda b, hk, gp, j: (hk, seq0 + b)),
            pl.BlockSpec((stat_rows, KV_WIDTH), lambda b, hk, gp, j: (seq0 + b, 0)),
        ],
        out_specs=pl.BlockSpec((tq, q_cols),
                               lambda b, hk, gp, j: (b * nq + j, hk * steps_per_kv + gp)),
        out_shape=jax.ShapeDtypeStruct((n_seq * s_len, ATTN_WIDTH), BF16),
        compiler_params=_compiler_params(4),
        name="attn_exact_max" if exact_max else "attn",
    )(q, k, vt, kstat)


def _group_partner(x, lane, d, group):
    unit = group // 4
    pos = (lane % group) // unit
    return jnp.where(pos + d < 4,
                     pltpu.roll(x, LANES - d * unit, 1),
                     pltpu.roll(x, (4 - d) * unit, 1))


def _route(scores, bias, lane):
    valid = lane < N_EXPERTS
    biased = scores + bias
    rank = jnp.zeros(biased.shape, jnp.int32)
    pos = lane % EXPERTS_PER_GROUP
    for d in range(1, EXPERTS_PER_GROUP):
        other = _group_partner(biased, lane, d, EXPERTS_PER_GROUP)
        other_is_lower = (pos + d) >= EXPERTS_PER_GROUP
        beats = (other > biased) | ((other == biased) & other_is_lower)
        rank = rank + beats.astype(jnp.int32)
    top2 = rank < 2
    contrib = jnp.where(top2, biased, 0.0)
    group_score = contrib
    for d in range(1, EXPERTS_PER_GROUP):
        group_score = group_score + _group_partner(contrib, lane, d, EXPERTS_PER_GROUP)
    n_groups = N_EXPERTS // EXPERTS_PER_GROUP
    gpos = lane // EXPERTS_PER_GROUP
    losses = jnp.zeros(biased.shape, jnp.int32)
    for d in range(1, n_groups):
        other = _group_partner(group_score, lane, d, N_EXPERTS)
        other_is_lower = (gpos + d) >= n_groups
        beats = (other > group_score) | ((other == group_score) & other_is_lower)
        losses = losses + beats.astype(jnp.int32)
    selected = top2 & (losses == 0) & valid
    w = jnp.where(selected, scores, 0.0)
    return w / jnp.sum(w, axis=-1, keepdims=True), selected


INFO_EXPERT, INFO_RANK, INFO_GATE = 0, 2, 4


def _post_kernel(l_ref, x_ref, a_ref, gm_ref, aon_ref, wo_ref, g1_ref, b1_ref,
                 wr_hi_ref, wr_lo_ref, rb_ref, x1_ref, info_ref, cnt_ref):
    del l_ref
    tt = x_ref.shape[0]
    a = _rms_norm(a_ref[...].astype(F32), aon_ref[...]).astype(BF16)
    mix = _dot(a, wo_ref[:ATTN_WIDTH, :]) + _dot(gm_ref[...], wo_ref[ATTN_WIDTH:, :])
    x1 = _layer_norm(DEEPNORM_ALPHA * x_ref[...] + mix, g1_ref[...], b1_ref[...])
    x1_ref[...] = x1
    x_hi = x1.astype(BF16)
    x_lo = (x1 - x_hi.astype(F32)).astype(BF16)
    logits = (_dot(x_hi, wr_hi_ref[...]) + _dot(x_lo, wr_hi_ref[...])
              + _dot(x_hi, wr_lo_ref[...]))
    lane = lax.broadcasted_iota(jnp.int32, (1, LANES), 1)
    gates, selected = _route(jax.nn.sigmoid(logits), rb_ref[...], lane)

    sel = jnp.where(selected, 1.0, 0.0)
    tri = jnp.where(lax.broadcasted_iota(jnp.int32, (tt, tt), 0)
                    >= lax.broadcasted_iota(jnp.int32, (tt, tt), 1), 1.0, 0.0).astype(BF16)
    incl = _dot(tri, sel.astype(BF16))
    rank = incl - sel
    cnt_ref[...] = jnp.broadcast_to(incl[tt - 1:tt, :], cnt_ref.shape)

    lane_f = lane.astype(F32)
    e_lo = jnp.min(jnp.where(selected, lane_f, float(LANES)), axis=-1, keepdims=True)
    e_hi = jnp.max(jnp.where(selected, lane_f, -1.0), axis=-1, keepdims=True)

    def pick(mat, e):
        return jnp.sum(jnp.where(lane_f == e, mat, 0.0), axis=-1, keepdims=True)

    fields = {INFO_EXPERT: e_lo, INFO_EXPERT + 1: e_hi,
              INFO_RANK: pick(rank, e_lo), INFO_RANK + 1: pick(rank, e_hi),
              INFO_GATE: pick(gates, e_lo), INFO_GATE + 1: pick(gates, e_hi)}
    info = jnp.zeros((tt, LANES), F32)
    for col, val in fields.items():
        info = jnp.where(lane == col, val, info)
    info_ref[...] = info


def _post_call(layer, x, attn, gm, p, tt):
    n = x.shape[0]

    def tok(i, l):
        return (i, 0)

    def lay3(i, l):
        return (l[0], 0, 0)

    def const2(i, l):
        return (0, 0)

    grid_spec = pltpu.PrefetchScalarGridSpec(
        num_scalar_prefetch=1,
        grid=(n // tt,),
        in_specs=[
            pl.BlockSpec((tt, D_MODEL), tok),
            pl.BlockSpec((tt, ATTN_WIDTH), tok),
            pl.BlockSpec((tt, GMLP_WIDTH), tok),
            pl.BlockSpec((None, 1, ATTN_WIDTH), lay3),
            pl.BlockSpec((None, D_MODEL, D_MODEL), lay3),
            pl.BlockSpec((None, 1, D_MODEL), lay3),
            pl.BlockSpec((None, 1, D_MODEL), lay3),
            pl.BlockSpec((D_MODEL, LANES), const2),
            pl.BlockSpec((D_MODEL, LANES), const2),
            pl.BlockSpec((1, LANES), const2),
        ],
        out_specs=[pl.BlockSpec((tt, D_MODEL), tok), pl.BlockSpec((tt, LANES), tok),
                   pl.BlockSpec((STAT_ROWS, LANES), tok)],
    )
    return pl.pallas_call(
        _post_kernel,
        grid_spec=grid_spec,
        out_shape=[jax.ShapeDtypeStruct((n, D_MODEL), F32),
                   jax.ShapeDtypeStruct((n, LANES), F32),
                   jax.ShapeDtypeStruct((n // tt * STAT_ROWS, LANES), F32)],
        compiler_params=_compiler_params(1),
        name="post",
    )(layer, x, attn, gm, p["attn_out_norm"], p["w_out"], p["ln1_g"], p["ln1_b"],
      p["wr_hi"], p["wr_lo"], p["router_bias"])


def _plan_routes(info, cnt, tt, tm):
    n = info.shape[0]
    n_tiles = n // tt
    expert = info[:, INFO_EXPERT:INFO_EXPERT + 2].astype(jnp.int32)
    rank = info[:, INFO_RANK:INFO_RANK + 2].astype(jnp.int32)
    counts = cnt.reshape(n_tiles, STAT_ROWS, LANES)[:, 0, :N_EXPERTS].astype(jnp.int32)
    totals = jnp.sum(counts, axis=0)
    seg_tiles = (totals + tm - 1) // tm
    seg_end = jnp.cumsum(seg_tiles)
    expert_row0 = (seg_end - seg_tiles) * tm
    tile_base = expert_row0[None, :] + jnp.cumsum(counts, axis=0) - counts
    is_expert = expert.reshape(n_tiles, tt, 2, 1) == jnp.arange(N_EXPERTS, dtype=jnp.int32)
    base = jnp.sum(jnp.where(is_expert, tile_base[:, None, None, :], 0), axis=-1)
    pos = base.reshape(n, 2) + rank
    n_row_tiles = 2 * n // tm + N_EXPERTS
    used = seg_end[-1]
    t_idx = jnp.arange(n_row_tiles, dtype=jnp.int32)
    tile_blk = jnp.minimum(t_idx, used - 1)
    tile_expert = jnp.minimum(
        jnp.sum((tile_blk[:, None] >= seg_end[None, :]).astype(jnp.int32), axis=1), N_EXPERTS - 1)
    return {
        "pos": pos.reshape(n_tiles, 1, 2 * tt).astype(jnp.int32),
        "pads": jnp.stack([jnp.append(expert_row0 + totals, used),
                           jnp.append(seg_tiles * tm - totals, n_row_tiles - used)]).astype(jnp.int32),
        "tile_expert": tile_expert.astype(jnp.int32),
        "tile_valid": (t_idx < used).astype(jnp.int32),
        "tile_blk": tile_blk.astype(jnp.int32),
        "n_rows": n_row_tiles * tm,
    }


ROW_SLAB = 8
assert ROW_SLAB * LANES == D_MODEL


def _slab_copy(src_ref, src_row, dst_ref, dst_row, sem):
    src = src_ref.at[pl.ds(pl.multiple_of(src_row * ROW_SLAB, ROW_SLAB), ROW_SLAB)]
    dst = dst_ref.at[pl.ds(pl.multiple_of(dst_row * ROW_SLAB, ROW_SLAB), ROW_SLAB)]
    return pltpu.make_async_copy(src, dst, sem)


def _rows_to_slabs(x, slab_ref):
    rows = x.shape[0]
    for s in range(ROW_SLAB):
        slab_ref[pl.ds(s, rows, stride=ROW_SLAB), :] = x[:, s * LANES:(s + 1) * LANES]


def _slabs_to_rows(slab_ref, rows):
    return jnp.concatenate(
        [slab_ref[pl.ds(s, rows, stride=ROW_SLAB), :] for s in range(ROW_SLAB)], axis=1)


DMA_UNROLL = 8


def _scatter_kernel(pads_ref, pos_ref, x_ref, xs_ref, slab_ref, zero_ref, sems):
    tt = x_ref.shape[0]
    i = pl.program_id(0)
    last = pl.num_programs(0) - 1
    slot = i % 2
    sem = sems.at[2]

    @pl.when(i == 0)
    def _():
        zero_ref[...] = jnp.zeros_like(zero_ref)
        for e in range(N_EXPERTS):
            start, length = pads_ref[0, e], pads_ref[1, e]

            def fill(j, c, start=start):
                _slab_copy(zero_ref, 0, xs_ref, start + j, sem).start()
                return c

            def fill_wait(j, c):
                _slab_copy(zero_ref, 0, xs_ref, 0, sem).wait()
                return c

            lax.fori_loop(0, length, fill, 0)
            lax.fori_loop(0, length, fill_wait, 0)

        tile_slabs = zero_ref.shape[0]
        first_tail_tile, n_tail_tiles = pads_ref[0, N_EXPERTS], pads_ref[1, N_EXPERTS]

        def tail_copy(j):
            row0 = pl.multiple_of((first_tail_tile + j) * tile_slabs, tile_slabs)
            return pltpu.make_async_copy(zero_ref, xs_ref.at[pl.ds(row0, tile_slabs)], sem)

        def tail_fill(j, c):
            tail_copy(j).start()
            return c

        def tail_wait(j, c):
            tail_copy(j).wait()
            return c

        lax.fori_loop(0, n_tail_tiles, tail_fill, 0)
        lax.fori_loop(0, n_tail_tiles, tail_wait, 0)

    _rows_to_slabs(x_ref[...], slab_ref.at[slot])

    def issue(r, c):
        for s in range(2):
            _slab_copy(slab_ref.at[slot], r, xs_ref, pos_ref[0, 2 * r + s],
                       sems.at[slot]).start(priority=s)
        return c

    def drain(which):
        def body(r, c):
            for s in range(2):
                _slab_copy(slab_ref.at[which], 0, xs_ref, 0, sems.at[which]).wait()
            return c
        lax.fori_loop(0, tt, body, 0, unroll=DMA_UNROLL)

    lax.fori_loop(0, tt, issue, 0, unroll=DMA_UNROLL)

    @pl.when(i > 0)
    def _():
        drain(1 - slot)

    @pl.when(i == last)
    def _():
        drain(slot)


def _scatter_call(x1, plan, tt, tm):
    n = x1.shape[0]
    grid_spec = pltpu.PrefetchScalarGridSpec(
        num_scalar_prefetch=1,
        grid=(n // tt,),
        in_specs=[
            pl.BlockSpec((None, 1, 2 * tt), lambda i, pads: (i, 0, 0), memory_space=pltpu.SMEM),
            pl.BlockSpec((tt, D_MODEL), lambda i, pads: (i, 0)),
        ],
        out_specs=pl.BlockSpec(memory_space=pl.ANY),
        scratch_shapes=[pltpu.VMEM((2, tt * ROW_SLAB, LANES), F32),
                        pltpu.VMEM((tm * ROW_SLAB, LANES), F32), pltpu.SemaphoreType.DMA((3,))],
    )
    return pl.pallas_call(
        _scatter_kernel,
        grid_spec=grid_spec,
        out_shape=jax.ShapeDtypeStruct((plan["n_rows"] * ROW_SLAB, LANES), F32),
        compiler_params=_compiler_params(1),
        name="scatter",
    )(plan["pads"], plan["pos"], x1)


def _expert_kernel(l_ref, te_ref, tv_ref, tb_ref, xs_ref, wg_ref, wu_ref, wd_ref, o_ref):
    del l_ref, te_ref, tb_ref
    tm = xs_ref.shape[0] // ROW_SLAB

    valid = tv_ref[pl.program_id(0)] != 0

    @pl.when(valid)
    def _():
        xb = _slabs_to_rows(xs_ref, tm).astype(BF16)
        hidden = jax.nn.silu(_dot(xb, wg_ref[...])) * _dot(xb, wu_ref[...])
        _rows_to_slabs(_dot(hidden.astype(BF16), wd_ref[...]), o_ref)

    @pl.when(jnp.logical_not(valid))
    def _():
        o_ref[...] = jnp.zeros_like(o_ref)


def _expert_call(layer, xs, plan, p, tm):
    n_rows = xs.shape[0] // ROW_SLAB

    def rows(t, l, te, tv, tb):
        return (tb[t], 0)

    def weights(t, l, te, tv, tb):
        return (l[0], te[t], 0, 0)

    grid_spec = pltpu.PrefetchScalarGridSpec(
        num_scalar_prefetch=4,
        grid=(n_rows // tm,),
        in_specs=[
            pl.BlockSpec((tm * ROW_SLAB, LANES), rows),
            pl.BlockSpec((None, None, D_MODEL, EXPERT_FF), weights),
            pl.BlockSpec((None, None, D_MODEL, EXPERT_FF), weights),
            pl.BlockSpec((None, None, EXPERT_FF, D_MODEL), weights),
        ],
        out_specs=pl.BlockSpec((tm * ROW_SLAB, LANES), lambda t, l, te, tv, tb: (t, 0)),
    )
    return pl.pallas_call(
        _expert_kernel,
        grid_spec=grid_spec,
        out_shape=jax.ShapeDtypeStruct(xs.shape, F32),
        compiler_params=_compiler_params(1),
        name="experts",
    )(layer, plan["tile_expert"], plan["tile_valid"], plan["tile_blk"], xs,
      p["w_gate"], p["w_up"], p["w_down"])


def _combine_kernel(l_ref, pos_ref, pos_next_ref, x_ref, info_ref, os_ref, g2_ref, b2_ref,
                    o_ref, buf_ref, sems):
    del l_ref
    tt = x_ref.shape[0]
    i = pl.program_id(0)
    slot = i % 2

    def gather(p_ref, which):
        def body(r, c):
            for s in range(2):
                _slab_copy(os_ref, p_ref[0, 2 * r + s], buf_ref.at[which, s], r,
                           sems.at[which]).start(priority=s)
            return c
        lax.fori_loop(0, tt, body, 0, unroll=DMA_UNROLL)

    @pl.when(i == 0)
    def _():
        gather(pos_ref, 0)

    @pl.when(i + 1 < pl.num_programs(0))
    def _():
        gather(pos_next_ref, 1 - slot)

    def drain(r, c):
        for s in range(2):
            _slab_copy(os_ref, 0, buf_ref.at[slot, s], 0, sems.at[slot]).wait()
        return c

    lax.fori_loop(0, tt, drain, 0, unroll=DMA_UNROLL)
    info = info_ref[...]
    y = (info[:, INFO_GATE:INFO_GATE + 1] * _slabs_to_rows(buf_ref.at[slot, 0], tt)
         + info[:, INFO_GATE + 1:INFO_GATE + 2] * _slabs_to_rows(buf_ref.at[slot, 1], tt))
    o_ref[...] = _layer_norm(DEEPNORM_ALPHA * x_ref[...] + y, g2_ref[...], b2_ref[...])


def _combine_call(layer, x1, info, out_sorted, plan, p, tt):
    n = x1.shape[0]

    def tok(i, l):
        return (i, 0)

    def lay3(i, l):
        return (l[0], 0, 0)

    n_tiles = n // tt
    grid_spec = pltpu.PrefetchScalarGridSpec(
        num_scalar_prefetch=1,
        grid=(n_tiles,),
        in_specs=[
            pl.BlockSpec((None, 1, 2 * tt), lambda i, l: (i, 0, 0), memory_space=pltpu.SMEM),
            pl.BlockSpec((None, 1, 2 * tt), lambda i, l: (jnp.minimum(i + 1, n_tiles - 1), 0, 0),
                         memory_space=pltpu.SMEM),
            pl.BlockSpec((tt, D_MODEL), tok),
            pl.BlockSpec((tt, LANES), tok),
            pl.BlockSpec(memory_space=pl.ANY),
            pl.BlockSpec((None, 1, D_MODEL), lay3),
            pl.BlockSpec((None, 1, D_MODEL), lay3),
        ],
        out_specs=pl.BlockSpec((tt, D_MODEL), tok),
        scratch_shapes=[pltpu.VMEM((2, 2, tt * ROW_SLAB, LANES), F32),
                        pltpu.SemaphoreType.DMA((2,))],
    )
    return pl.pallas_call(
        _combine_kernel,
        grid_spec=grid_spec,
        out_shape=jax.ShapeDtypeStruct((n, D_MODEL), F32),
        compiler_params=_compiler_params(1),
        name="combine",
    )(layer, plan["pos"], plan["pos"], x1, info, out_sorted, p["ln2_g"], p["ln2_b"])


def _rope_tables(max_len):
    t = jnp.arange(max_len, dtype=jnp.int32)
    row = (t // GRID_W).astype(F32)
    col = (t % GRID_W).astype(F32)
    n_pairs_axis = HEAD_DIM // 4
    inv_freq = 1.0 / (ROPE_THETA ** (jnp.arange(n_pairs_axis, dtype=F32) / n_pairs_axis))
    ang = jnp.concatenate([row[:, None] * inv_freq, col[:, None] * inv_freq], -1)
    cos, sin = jnp.cos(ang), jnp.sin(ang)
    reps = LANES // HEAD_DIM
    cos128 = jnp.tile(jnp.concatenate([cos, cos], -1), (1, reps))
    sin128 = jnp.tile(jnp.concatenate([-sin, sin], -1), (1, reps))
    return cos128, sin128


def _prepare_params(max_len, w_in, q_norm, k_norm, gmlp_ln_g, gmlp_ln_b, w_spatial, b_spatial,
                    attn_out_norm, gmlp_out_norm, w_out, ln1_g, ln1_b, w_router, router_bias,
                    w_gate, w_up, w_down, ln2_g, ln2_b):
    depth = w_in.shape[0]
    perm64 = jnp.concatenate([jnp.arange(0, HEAD_DIM, 2), jnp.arange(1, HEAD_DIM, 2)])
    n_rot_heads = N_Q_HEADS + N_KV_HEADS
    rot_cols = (jnp.arange(n_rot_heads)[:, None] * HEAD_DIM + perm64[None, :]).reshape(-1)
    cols = jnp.concatenate([rot_cols, jnp.arange(n_rot_heads * HEAD_DIM, IN_WIDTH)])
    cos128, sin128 = _rope_tables(max_len)
    head_of_lane = jnp.arange(ATTN_WIDTH) // HEAD_DIM
    row3 = lambda a: a.reshape(depth, 1, -1).astype(F32)
    wr = jnp.zeros((D_MODEL, LANES), F32).at[:, :N_EXPERTS].set(w_router.astype(F32))
    wr_hi = wr.astype(BF16)
    max_offset = (1.01 * LOG2_E * HEAD_DIM ** 0.5
                  * jnp.max(jnp.abs(q_norm), axis=1) * jnp.max(jnp.abs(k_norm), axis=1))
    return {
        "attn_needs_exact_max": max_offset > MAX_SAFE_OFFSET,
        "w_in": w_in[:, :, cols].astype(BF16),
        "cos": cos128,
        "sin": sin128,
        "seg": (head_of_lane[:, None] == head_of_lane[None, :]).astype(BF16),
        "q_norm": row3(jnp.tile(q_norm[:, perm64], (1, N_Q_HEADS))),
        "k_norm": row3(jnp.tile(k_norm[:, perm64], (1, N_KV_HEADS))),
        "gmlp_ln_g": row3(gmlp_ln_g),
        "gmlp_ln_b": row3(gmlp_ln_b),
        "w_cat": w_spatial.transpose(0, 2, 1, 3).reshape(depth, CHUNK, N_GMLP_HEADS * CHUNK).astype(BF16),
        "b_sp": jnp.repeat(b_spatial.transpose(0, 2, 1), GMLP_HEAD_DIM, axis=2).astype(F32),
        "attn_out_norm": row3(attn_out_norm),
        "gmlp_out_norm": row3(gmlp_out_norm),
        "w_out": w_out.astype(BF16),
        "ln1_g": row3(ln1_g),
        "ln1_b": row3(ln1_b),
        "wr_hi": wr_hi,
        "wr_lo": (wr - wr_hi.astype(F32)).astype(BF16),
        "router_bias": jnp.zeros((1, LANES), F32).at[0, :N_EXPERTS].set(router_bias.astype(F32)),
        "w_gate": w_gate.astype(BF16),
        "w_up": w_up.astype(BF16),
        "w_down": w_down.astype(BF16),
        "ln2_g": row3(ln2_g),
        "ln2_b": row3(ln2_b),
    }


def _trunk(x_tokens, groups, p, depth, tp, tt, tm, tq, tk):
    (_, n_seq0, s0), = groups[:1]
    n_prompt_blocks = (n_seq0 * s0) // tp
    prompt_period = s0 // tp
    sample_period = (groups[1][2] if len(groups) > 1 else s0) // tp
    periods = (n_prompt_blocks, prompt_period, sample_period)

    def layer_body(l, x):
        layer = jnp.reshape(l, (1,)).astype(jnp.int32)
        q, k, vt, gm, kstat = _proj_call(layer, x, p, tp, periods)

        def attention(exact_max):
            outs = [_attn_call(q, k, vt, kstat, start, n_seq, s_len, min(tq, s_len),
                               min(tk, s_len), tp, exact_max)
                    for (start, n_seq, s_len) in groups]
            return outs[0] if len(outs) == 1 else jnp.concatenate(outs, axis=0)

        attn = lax.cond(p["attn_needs_exact_max"][l],
                        functools.partial(attention, True), functools.partial(attention, False))
        x1, info, cnt = _post_call(layer, x, attn, gm, p, tt)
        plan = _plan_routes(info, cnt, tt, tm)
        xs = _scatter_call(x1, plan, tt, tm)
        out_sorted = _expert_call(layer, xs, plan, p, tm)
        return _combine_call(layer, x1, info, out_sorted, plan, p, tt)

    return lax.fori_loop(0, depth, layer_body, x_tokens)


def kernel(x_prompt, x_sample, w_in, q_norm, k_norm, gmlp_ln_g, gmlp_ln_b, w_spatial, b_spatial,
           attn_out_norm, gmlp_out_norm, w_out, ln1_g, ln1_b, w_router, router_bias,
           w_gate, w_up, w_down, ln2_g, ln2_b):
    bp, sp, d = x_prompt.shape
    bs, ss, _ = x_sample.shape
    assert sp % ss == 0 and (bp * sp) % ss == 0, "sample sequences must tile the prompt stream"
    p = _prepare_params(max(sp, ss), w_in, q_norm, k_norm, gmlp_ln_g, gmlp_ln_b, w_spatial,
                        b_spatial, attn_out_norm, gmlp_out_norm, w_out, ln1_g, ln1_b, w_router,
                        router_bias, w_gate, w_up, w_down, ln2_g, ln2_b)
    x = jnp.concatenate([x_prompt.reshape(bp * sp, d), x_sample.reshape(bs * ss, d)], axis=0)
    groups = ((0, bp, sp), (bp * sp, bs, ss))
    y = _trunk(x.astype(F32), groups, p, w_in.shape[0], tp=min(512, ss), tt=256, tm=512,
               tq=256, tk=512)
    return (y[:bp * sp].reshape(bp, sp, d), y[bp * sp:].reshape(bs, ss, d))
```

```python
import functools

import jax
import jax.numpy as jnp
from jax import lax
from jax.experimental import pallas as pl
from jax.experimental.pallas import tpu as pltpu

D_MODEL = 1024
DEPTH = 4
HEAD_DIM = 64
N_Q_HEADS = 8
N_KV_HEADS = 2
ATTN_WIDTH = N_Q_HEADS * HEAD_DIM
KV_WIDTH = N_KV_HEADS * HEAD_DIM
GMLP_WIDTH = 512
N_GMLP_HEADS = 8
GMLP_HEAD_DIM = 64
CHUNK = 128
IN_WIDTH = ATTN_WIDTH + 2 * KV_WIDTH + 2 * GMLP_WIDTH
N_EXPERTS = 16
EXPERTS_PER_GROUP = 4
EXPERT_FF = 512
ROPE_THETA = 10000.0
GRID_W = 64
EPS = 1e-6
DEEPNORM_ALPHA = float((2 * DEPTH) ** 0.25)
LOG2_E = 1.4426950408889634
STAT_ROWS = 8
MAX_SAFE_OFFSET = 60.0

LANES = 128
F32 = jnp.float32
BF16 = jnp.bfloat16
VMEM_LIMIT_BYTES = 48 * 1024 * 1024


def _compiler_params(n_grid_dims):
    return pltpu.CompilerParams(
        dimension_semantics=("arbitrary",) * n_grid_dims,
        vmem_limit_bytes=VMEM_LIMIT_BYTES,
    )


def _dot(a, b):
    return jnp.dot(a, b, preferred_element_type=F32)


def _layer_norm(x, g, b):
    mu = jnp.mean(x, axis=-1, keepdims=True)
    xc = x - mu
    var = jnp.mean(xc * xc, axis=-1, keepdims=True)
    return xc * lax.rsqrt(var + EPS) * g + b


def _rms_norm(x, g):
    return x * lax.rsqrt(jnp.mean(x * x, axis=-1, keepdims=True) + EPS) * g


def _head_rms_norm(x, seg, g):
    ssq = _dot((x * x).astype(BF16), seg)
    return x * lax.rsqrt(ssq * (1.0 / HEAD_DIM) + EPS) * g


def _rope_half_split(x, cos, sin_signed):
    first_half = (lax.broadcasted_iota(jnp.int32, (1, LANES), 1) % HEAD_DIM) < (HEAD_DIM // 2)
    cols = []
    for c in range(x.shape[1] // LANES):
        xc = x[:, c * LANES:(c + 1) * LANES]
        partner = jnp.where(first_half,
                            pltpu.roll(xc, LANES - HEAD_DIM // 2, 1),
                            pltpu.roll(xc, HEAD_DIM // 2, 1))
        cols.append(xc * cos + partner * sin_signed)
    return cols[0] if len(cols) == 1 else jnp.concatenate(cols, axis=1)


def _proj_kernel(l_ref, x_ref, w_in_ref, cos_ref, sin_ref, seg_ref, qn_ref, kn_ref,
                 lng_ref, lnb_ref, wcat_ref, bsp_ref, gon_ref,
                 q_ref, k_ref, vt_ref, gm_ref, kstat_ref):
    del l_ref
    tt = x_ref.shape[0]
    h = _dot(x_ref[...].astype(BF16), w_in_ref[...])
    cos = cos_ref[...]
    sin_signed = sin_ref[...]

    q = _head_rms_norm(h[:, :ATTN_WIDTH], seg_ref[...], qn_ref[...])
    q = _rope_half_split(q, cos, sin_signed) * (LOG2_E * HEAD_DIM ** -0.5)
    q_ref[...] = q.astype(BF16)

    k0 = ATTN_WIDTH
    seg_kv = seg_ref[:KV_WIDTH, :KV_WIDTH]
    k = _head_rms_norm(h[:, k0:k0 + KV_WIDTH], seg_kv, kn_ref[...])
    k = _rope_half_split(k, cos, sin_signed).astype(BF16)
    kf = k.astype(F32)
    k_sq = _dot((kf * kf).astype(BF16), seg_kv)
    kstat_ref[...] = jnp.broadcast_to(jnp.max(k_sq, axis=0, keepdims=True), kstat_ref.shape)
    for hk in range(N_KV_HEADS):
        k_ref[hk] = k[:, hk * HEAD_DIM:(hk + 1) * HEAD_DIM]

    v0 = k0 + KV_WIDTH
    vt_ref[...] = h[:, v0:v0 + KV_WIDTH].T.astype(BF16)

    u0 = v0 + KV_WIDTH
    u = jax.nn.gelu(h[:, u0:u0 + GMLP_WIDTH])
    vg = _layer_norm(jax.nn.gelu(h[:, u0 + GMLP_WIDTH:]), lng_ref[...], lnb_ref[...])
    lane_head = lax.broadcasted_iota(jnp.int32, (1, GMLP_WIDTH), 1) // GMLP_HEAD_DIM
    mixed = []
    for c in range(tt // CHUNK):
        vc = vg[c * CHUNK:(c + 1) * CHUNK].astype(BF16)
        stack = jnp.concatenate(
            [jnp.where(lane_head == hh, vc, jnp.zeros_like(vc)) for hh in range(N_GMLP_HEADS)],
            axis=0)
        mixed.append(_dot(wcat_ref[...], stack) + bsp_ref[...])
    mixed = mixed[0] if len(mixed) == 1 else jnp.concatenate(mixed, axis=0)
    gm_ref[...] = _rms_norm(u * mixed, gon_ref[...]).astype(BF16)


def _proj_call(layer, x, p, tt, rope_period_blocks):
    n = x.shape[0]
    n_prompt_blocks, prompt_period, sample_period = rope_period_blocks

    def tok(i, l):
        return (i, 0)

    def rope_idx(i, l):
        return (jnp.where(i < n_prompt_blocks, i % prompt_period, i % sample_period), 0)

    def lay3(i, l):
        return (l[0], 0, 0)

    def const2(i, l):
        return (0, 0)

    grid_spec = pltpu.PrefetchScalarGridSpec(
        num_scalar_prefetch=1,
        grid=(n // tt,),
        in_specs=[
            pl.BlockSpec((tt, D_MODEL), tok),
            pl.BlockSpec((None, D_MODEL, IN_WIDTH), lay3),
            pl.BlockSpec((tt, LANES), rope_idx),
            pl.BlockSpec((tt, LANES), rope_idx),
            pl.BlockSpec((ATTN_WIDTH, ATTN_WIDTH), const2),
            pl.BlockSpec((None, 1, ATTN_WIDTH), lay3),
            pl.BlockSpec((None, 1, KV_WIDTH), lay3),
            pl.BlockSpec((None, 1, GMLP_WIDTH), lay3),
            pl.BlockSpec((None, 1, GMLP_WIDTH), lay3),
            pl.BlockSpec((None, CHUNK, N_GMLP_HEADS * CHUNK), lay3),
            pl.BlockSpec((None, CHUNK, GMLP_WIDTH), lay3),
            pl.BlockSpec((None, 1, GMLP_WIDTH), lay3),
        ],
        out_specs=[
            pl.BlockSpec((tt, ATTN_WIDTH), tok),
            pl.BlockSpec((N_KV_HEADS, tt, HEAD_DIM), lambda i, l: (0, i, 0)),
            pl.BlockSpec((KV_WIDTH, tt), lambda i, l: (0, i)),
            pl.BlockSpec((tt, GMLP_WIDTH), tok),
            pl.BlockSpec((STAT_ROWS, KV_WIDTH), tok),
        ],
    )
    return pl.pallas_call(
        _proj_kernel,
        grid_spec=grid_spec,
        out_shape=[
            jax.ShapeDtypeStruct((n, ATTN_WIDTH), BF16),
            jax.ShapeDtypeStruct((N_KV_HEADS, n, HEAD_DIM), BF16),
            jax.ShapeDtypeStruct((KV_WIDTH, n), BF16),
            jax.ShapeDtypeStruct((n, GMLP_WIDTH), BF16),
            jax.ShapeDtypeStruct((n // tt * STAT_ROWS, KV_WIDTH), F32),
        ],
        compiler_params=_compiler_params(1),
        name="proj",
    )(layer, x, p["w_in"], p["cos"], p["sin"], p["seg"], p["q_norm"], p["k_norm"],
      p["gmlp_ln_g"], p["gmlp_ln_b"], p["w_cat"], p["b_sp"], p["gmlp_out_norm"])


def _scores(k_blk, q):
    return lax.dot_general(k_blk, q, (((1,), (1,)), ((), ())), preferred_element_type=F32)


def _attn_kernel(q_ref, k_ref, vt_ref, kstat_ref, o_ref, *, tk, exact_max):
    tq = q_ref.shape[0]
    s_len = k_ref.shape[0]
    heads = q_ref.shape[1] // HEAD_DIM
    q_all = q_ref[...]
    q = jnp.concatenate([q_all[:, g * HEAD_DIM:(g + 1) * HEAD_DIM] for g in range(heads)], axis=0)
    nq = heads * tq
    if exact_max:
        def max_body(i, m):
            k_blk = k_ref[pl.ds(pl.multiple_of(i * tk, tk), tk), :]
            return jnp.maximum(m, jnp.max(_scores(k_blk, q), axis=0, keepdims=True))
        off = lax.fori_loop(0, s_len // tk, max_body, jnp.full((1, nq), -jnp.inf, F32))
    else:
        hk = pl.program_id(1)
        lane = lax.broadcasted_iota(jnp.int32, (1, LANES), 1)
        kstat = jnp.max(kstat_ref[...], axis=0, keepdims=True)
        kmax2 = jnp.max(jnp.where(lane // HEAD_DIM == hk, kstat, 0.0), axis=1, keepdims=True)
        qf = q.astype(F32)
        qq = _scores(jnp.ones((8, HEAD_DIM), BF16), (qf * qf).astype(BF16))
        off = jnp.sqrt(qq[0:1] * kmax2)

    l8 = jnp.zeros((8, nq), F32)
    acc = jnp.zeros((HEAD_DIM, nq), F32)
    for c in range(s_len // tk):
        k_blk = k_ref[c * tk:(c + 1) * tk, :]
        vt_blk = vt_ref[:, c * tk:(c + 1) * tk]
        p = jnp.exp2(_scores(k_blk, q) - off)
        l8 = l8 + jnp.sum(p.reshape(tk // 8, 8, nq), axis=0)
        acc = acc + _dot(vt_blk, p.astype(BF16))
    out = acc / jnp.sum(l8, axis=0, keepdims=True)
    out_t = jnp.concatenate([out[:, g * tq:(g + 1) * tq] for g in range(heads)], axis=0)
    o_ref[...] = out_t.T.astype(o_ref.dtype)


def _attn_call(q, k, vt, kstat, token_start, n_seq, s_len, tq, tk, tt, exact_max):
    heads_per_step = 4
    q_cols = heads_per_step * HEAD_DIM
    steps_per_kv = (N_Q_HEADS // N_KV_HEADS) // heads_per_step
    q_blk0 = token_start // tq
    seq0 = token_start // s_len
    nq = s_len // tq
    stat_rows = s_len // tt * STAT_ROWS

    def q_idx(b, hk, gp, j):
        return (q_blk0 + b * nq + j, hk * steps_per_kv + gp)

    return pl.pallas_call(
        functools.partial(_attn_kernel, tk=tk, exact_max=exact_max),
        grid=(n_seq, N_KV_HEADS, steps_per_kv, nq),
        in_specs=[
            pl.BlockSpec((tq, q_cols), q_idx),
            pl.BlockSpec((None, s_len, HEAD_DIM), lambda b, hk, gp, j: (hk, seq0 + b, 0)),
            pl.BlockSpec((HEAD_DIM, s_len), lambda b, hk, gp, j: (hk, seq0 + b)),
            pl.BlockSpec((stat_rows, KV_WIDTH), lambda b, hk, gp, j: (seq0 + b, 0)),
        ],
        out_specs=pl.BlockSpec((tq, q_cols),
                               lambda b, hk, gp, j: (b * nq + j, hk * steps_per_kv + gp)),
        out_shape=jax.ShapeDtypeStruct((n_seq * s_len, ATTN_WIDTH), BF16),
        compiler_params=_compiler_params(4),
        name="attn_exact_max" if exact_max else "attn",
    )(q, k, vt, kstat)


def _group_partner(x, row, d):
    pos = row % EXPERTS_PER_GROUP
    return jnp.where(pos + d < EXPERTS_PER_GROUP,
                     pltpu.roll(x, N_EXPERTS - d, 0),
                     pltpu.roll(x, EXPERTS_PER_GROUP - d, 0))


def _route(scores, bias, row):
    biased = scores + bias
    rank = jnp.zeros(biased.shape, jnp.int32)
    pos = row % EXPERTS_PER_GROUP
    for d in range(1, EXPERTS_PER_GROUP):
        other = _group_partner(biased, row, d)
        other_is_lower = (pos + d) >= EXPERTS_PER_GROUP
        beats = (other > biased) | ((other == biased) & other_is_lower)
        rank = rank + beats.astype(jnp.int32)
    top2 = rank < 2
    contrib = jnp.where(top2, biased, 0.0)
    group_score = contrib
    for d in range(1, EXPERTS_PER_GROUP):
        group_score = group_score + _group_partner(contrib, row, d)
    n_groups = N_EXPERTS // EXPERTS_PER_GROUP
    gpos = row // EXPERTS_PER_GROUP
    losses = jnp.zeros(biased.shape, jnp.int32)
    for d in range(1, n_groups):
        other = pltpu.roll(group_score, N_EXPERTS - d * EXPERTS_PER_GROUP, 0)
        other_is_lower = (gpos + d) >= n_groups
        beats = (other > group_score) | ((other == group_score) & other_is_lower)
        losses = losses + beats.astype(jnp.int32)
    selected = top2 & (losses == 0)
    w = jnp.where(selected, scores, 0.0)
    return w / jnp.sum(w, axis=0, keepdims=True), selected


INFO_EXPERT, INFO_RANK, INFO_GATE, INFO_ROWS = 0, 2, 4, 8


def _post_kernel(l_ref, x_ref, a_ref, gm_ref, aon_ref, wo_ref, g1_ref, b1_ref,
                 wr_ref, rb_ref, x1_ref, info_ref, cnt_ref, *, tt):
    del l_ref
    ts = x_ref.shape[0]
    a = _rms_norm(a_ref[...].astype(F32), aon_ref[...]).astype(BF16)
    mix = _dot(a, wo_ref[:ATTN_WIDTH, :]) + _dot(gm_ref[...], wo_ref[ATTN_WIDTH:, :])
    x1 = _layer_norm(DEEPNORM_ALPHA * x_ref[...] + mix, g1_ref[...], b1_ref[...])
    x1_ref[...] = x1
    x_hi = x1.astype(BF16)
    x_lo = (x1 - x_hi.astype(F32)).astype(BF16)
    by_hi = _scores(wr_ref[...], x_hi)
    logits = (by_hi[:N_EXPERTS] + by_hi[N_EXPERTS:]
              + _scores(wr_ref[:N_EXPERTS, :], x_lo))
    row = lax.broadcasted_iota(jnp.int32, (N_EXPERTS, 1), 0)
    gates, selected = _route(jax.nn.sigmoid(logits), rb_ref[...], row)

    sel = jnp.where(selected, 1.0, 0.0)
    earlier = lax.broadcasted_iota(jnp.int32, (ts, ts), 0)
    later = lax.broadcasted_iota(jnp.int32, (ts, ts), 1)
    prefix = jnp.where((earlier <= later) & (earlier // tt == later // tt), 1.0, 0.0).astype(BF16)
    incl = _dot(sel.astype(BF16), prefix)
    rank = incl - sel
    for t in range(ts // tt):
        last = (t + 1) * tt - 1
        cnt_ref[t * N_EXPERTS:(t + 1) * N_EXPERTS] = jnp.broadcast_to(
            incl[:, last:last + 1], (N_EXPERTS, LANES))

    row_f = row.astype(F32)
    e_lo = jnp.min(jnp.where(selected, row_f, float(N_EXPERTS)), axis=0, keepdims=True)
    e_hi = jnp.max(jnp.where(selected, row_f, -1.0), axis=0, keepdims=True)

    def pick(mat, e):
        return jnp.sum(jnp.where(row_f == e, mat, 0.0), axis=0, keepdims=True)

    fields = {INFO_EXPERT: e_lo, INFO_EXPERT + 1: e_hi,
              INFO_RANK: pick(rank, e_lo), INFO_RANK + 1: pick(rank, e_hi),
              INFO_GATE: pick(gates, e_lo), INFO_GATE + 1: pick(gates, e_hi)}
    info_row = lax.broadcasted_iota(jnp.int32, (INFO_ROWS, 1), 0)
    info = jnp.zeros((INFO_ROWS, ts), F32)
    for r, val in fields.items():
        info = jnp.where(info_row == r, val, info)
    info_ref[...] = info


def _post_call(layer, x, attn, gm, p, tt, tiles_per_step):
    n = x.shape[0]
    ts = tt * tiles_per_step

    def tok(i, l):
        return (i, 0)

    def lay3(i, l):
        return (l[0], 0, 0)

    def const2(i, l):
        return (0, 0)

    grid_spec = pltpu.PrefetchScalarGridSpec(
        num_scalar_prefetch=1,
        grid=(n // ts,),
        in_specs=[
            pl.BlockSpec((ts, D_MODEL), tok),
            pl.BlockSpec((ts, ATTN_WIDTH), tok),
            pl.BlockSpec((ts, GMLP_WIDTH), tok),
            pl.BlockSpec((None, 1, ATTN_WIDTH), lay3),
            pl.BlockSpec((None, D_MODEL, D_MODEL), lay3),
            pl.BlockSpec((None, 1, D_MODEL), lay3),
            pl.BlockSpec((None, 1, D_MODEL), lay3),
            pl.BlockSpec((2 * N_EXPERTS, D_MODEL), const2),
            pl.BlockSpec((N_EXPERTS, 1), const2),
        ],
        out_specs=[pl.BlockSpec((ts, D_MODEL), tok),
                   pl.BlockSpec((INFO_ROWS, ts), lambda i, l: (0, i)),
                   pl.BlockSpec((tiles_per_step * N_EXPERTS, LANES), tok)],
    )
    return pl.pallas_call(
        functools.partial(_post_kernel, tt=tt),
        grid_spec=grid_spec,
        out_shape=[jax.ShapeDtypeStruct((n, D_MODEL), F32),
                   jax.ShapeDtypeStruct((INFO_ROWS, n), F32),
                   jax.ShapeDtypeStruct((n // tt * N_EXPERTS, LANES), F32)],
        compiler_params=_compiler_params(1),
        name="post",
    )(layer, x, attn, gm, p["attn_out_norm"], p["w_out"], p["ln1_g"], p["ln1_b"],
      p["wr"], p["router_bias"])


def _plan_routes(info, cnt, tt, tm):
    n = info.shape[1]
    n_tiles = n // tt
    expert = info[INFO_EXPERT:INFO_EXPERT + 2].T.astype(jnp.int32)
    rank = info[INFO_RANK:INFO_RANK + 2].T.astype(jnp.int32)
    counts = cnt.reshape(n_tiles, N_EXPERTS, LANES)[:, :, 0].astype(jnp.int32)
    totals = jnp.sum(counts, axis=0)
    seg_tiles = (totals + tm - 1) // tm
    seg_end = jnp.cumsum(seg_tiles)
    expert_row0 = (seg_end - seg_tiles) * tm
    tile_base = expert_row0[None, :] + jnp.cumsum(counts, axis=0) - counts
    is_expert = expert.reshape(n_tiles, tt, 2, 1) == jnp.arange(N_EXPERTS, dtype=jnp.int32)
    base = jnp.sum(jnp.where(is_expert, tile_base[:, None, None, :], 0), axis=-1)
    pos = base.reshape(n, 2) + rank
    n_row_tiles = 2 * n // tm + N_EXPERTS
    used = seg_end[-1]
    t_idx = jnp.arange(n_row_tiles, dtype=jnp.int32)
    tile_blk = jnp.minimum(t_idx, used - 1)
    tile_expert = jnp.minimum(
        jnp.sum((tile_blk[:, None] >= seg_end[None, :]).astype(jnp.int32), axis=1), N_EXPERTS - 1)
    return {
        "pos": pos.reshape(n_tiles, 1, 2 * tt).astype(jnp.int32),
        "gates": info[INFO_GATE:INFO_GATE + 2].T,
        "pads": jnp.stack([jnp.append(expert_row0 + totals, used),
                           jnp.append(seg_tiles * tm - totals, n_row_tiles - used)]).astype(jnp.int32),
        "tile_expert": tile_expert.astype(jnp.int32),
        "tile_valid": (t_idx < used).astype(jnp.int32),
        "tile_blk": tile_blk.astype(jnp.int32),
        "n_rows": n_row_tiles * tm,
    }


ROW_SLAB = 8
assert ROW_SLAB * LANES == D_MODEL


def _slab_copy(src_ref, src_row, dst_ref, dst_row, sem):
    src = src_ref.at[pl.ds(pl.multiple_of(src_row * ROW_SLAB, ROW_SLAB), ROW_SLAB)]
    dst = dst_ref.at[pl.ds(pl.multiple_of(dst_row * ROW_SLAB, ROW_SLAB), ROW_SLAB)]
    return pltpu.make_async_copy(src, dst, sem)


def _rows_to_slabs(x, slab_ref):
    rows = x.shape[0]
    for s in range(ROW_SLAB):
        slab_ref[pl.ds(s, rows, stride=ROW_SLAB), :] = x[:, s * LANES:(s + 1) * LANES]


def _slabs_to_rows(slab_ref, rows):
    return jnp.concatenate(
        [slab_ref[pl.ds(s, rows, stride=ROW_SLAB), :] for s in range(ROW_SLAB)], axis=1)


DMA_UNROLL = 8


def _scatter_kernel(pads_ref, pos_ref, x_ref, xs_ref, slab_ref, zero_ref, sems):
    tt = x_ref.shape[0]
    i = pl.program_id(0)
    last = pl.num_programs(0) - 1
    slot = i % 2
    sem = sems.at[2]

    @pl.when(i == 0)
    def _():
        zero_ref[...] = jnp.zeros_like(zero_ref)
        for e in range(N_EXPERTS):
            start, length = pads_ref[0, e], pads_ref[1, e]

            def fill(j, c, start=start):
                _slab_copy(zero_ref, 0, xs_ref, start + j, sem).start()
                return c

            def fill_wait(j, c):
                _slab_copy(zero_ref, 0, xs_ref, 0, sem).wait()
                return c

            lax.fori_loop(0, length, fill, 0)
            lax.fori_loop(0, length, fill_wait, 0)

        tile_slabs = zero_ref.shape[0]
        first_tail_tile, n_tail_tiles = pads_ref[0, N_EXPERTS], pads_ref[1, N_EXPERTS]

        def tail_copy(j):
            row0 = pl.multiple_of((first_tail_tile + j) * tile_slabs, tile_slabs)
            return pltpu.make_async_copy(zero_ref, xs_ref.at[pl.ds(row0, tile_slabs)], sem)

        def tail_fill(j, c):
            tail_copy(j).start()
            return c

        def tail_wait(j, c):
            tail_copy(j).wait()
            return c

        lax.fori_loop(0, n_tail_tiles, tail_fill, 0)
        lax.fori_loop(0, n_tail_tiles, tail_wait, 0)

    _rows_to_slabs(x_ref[...], slab_ref.at[slot])

    def issue(r, c):
        for s in range(2):
            _slab_copy(slab_ref.at[slot], r, xs_ref, pos_ref[0, 2 * r + s],
                       sems.at[slot]).start(priority=s)
        return c

    def drain(which):
        def body(r, c):
            for s in range(2):
                _slab_copy(slab_ref.at[which], 0, xs_ref, 0, sems.at[which]).wait()
            return c
        lax.fori_loop(0, tt, body, 0, unroll=DMA_UNROLL)

    lax.fori_loop(0, tt, issue, 0, unroll=DMA_UNROLL)

    @pl.when(i > 0)
    def _():
        drain(1 - slot)

    @pl.when(i == last)
    def _():
        drain(slot)


def _scatter_call(x1, plan, tt, tm):
    n = x1.shape[0]
    grid_spec = pltpu.PrefetchScalarGridSpec(
        num_scalar_prefetch=1,
        grid=(n // tt,),
        in_specs=[
            pl.BlockSpec((None, 1, 2 * tt), lambda i, pads: (i, 0, 0), memory_space=pltpu.SMEM),
            pl.BlockSpec((tt, D_MODEL), lambda i, pads: (i, 0)),
        ],
        out_specs=pl.BlockSpec(memory_space=pl.ANY),
        scratch_shapes=[pltpu.VMEM((2, tt * ROW_SLAB, LANES), F32),
                        pltpu.VMEM((tm * ROW_SLAB, LANES), F32), pltpu.SemaphoreType.DMA((3,))],
    )
    return pl.pallas_call(
        _scatter_kernel,
        grid_spec=grid_spec,
        out_shape=jax.ShapeDtypeStruct((plan["n_rows"] * ROW_SLAB, LANES), F32),
        compiler_params=_compiler_params(1),
        name="scatter",
    )(plan["pads"], plan["pos"], x1)


def _expert_kernel(l_ref, te_ref, tv_ref, tb_ref, xs_ref, wg_ref, wu_ref, wd_ref, o_ref):
    del l_ref, te_ref, tb_ref
    tm = xs_ref.shape[0] // ROW_SLAB

    valid = tv_ref[pl.program_id(0)] != 0

    @pl.when(valid)
    def _():
        xb = _slabs_to_rows(xs_ref, tm).astype(BF16)
        hidden = jax.nn.silu(_dot(xb, wg_ref[...])) * _dot(xb, wu_ref[...])
        _rows_to_slabs(_dot(hidden.astype(BF16), wd_ref[...]), o_ref)

    @pl.when(jnp.logical_not(valid))
    def _():
        o_ref[...] = jnp.zeros_like(o_ref)


def _expert_call(layer, xs, plan, p, tm):
    n_rows = xs.shape[0] // ROW_SLAB

    def rows(t, l, te, tv, tb):
        return (tb[t], 0)

    def weights(t, l, te, tv, tb):
        return (l[0], te[t], 0, 0)

    grid_spec = pltpu.PrefetchScalarGridSpec(
        num_scalar_prefetch=4,
        grid=(n_rows // tm,),
        in_specs=[
            pl.BlockSpec((tm * ROW_SLAB, LANES), rows),
            pl.BlockSpec((None, None, D_MODEL, EXPERT_FF), weights),
            pl.BlockSpec((None, None, D_MODEL, EXPERT_FF), weights),
            pl.BlockSpec((None, None, EXPERT_FF, D_MODEL), weights),
        ],
        out_specs=pl.BlockSpec((tm * ROW_SLAB, LANES), lambda t, l, te, tv, tb: (t, 0)),
    )
    return pl.pallas_call(
        _expert_kernel,
        grid_spec=grid_spec,
        out_shape=jax.ShapeDtypeStruct(xs.shape, F32),
        compiler_params=_compiler_params(1),
        name="experts",
    )(layer, plan["tile_expert"], plan["tile_valid"], plan["tile_blk"], xs,
      p["w_gate"], p["w_up"], p["w_down"])


def _combine_kernel(l_ref, pos_ref, pos_next_ref, x_ref, gate_ref, os_ref, g2_ref, b2_ref,
                    o_ref, buf_ref, sems):
    del l_ref
    tt = x_ref.shape[0]
    i = pl.program_id(0)
    slot = i % 2

    def gather(p_ref, which):
        def body(r, c):
            for s in range(2):
                _slab_copy(os_ref, p_ref[0, 2 * r + s], buf_ref.at[which, s], r,
                           sems.at[which]).start(priority=s)
            return c
        lax.fori_loop(0, tt, body, 0, unroll=DMA_UNROLL)

    @pl.when(i == 0)
    def _():
        gather(pos_ref, 0)

    @pl.when(i + 1 < pl.num_programs(0))
    def _():
        gather(pos_next_ref, 1 - slot)

    def drain(r, c):
        for s in range(2):
            _slab_copy(os_ref, 0, buf_ref.at[slot, s], 0, sems.at[slot]).wait()
        return c

    lax.fori_loop(0, tt, drain, 0, unroll=DMA_UNROLL)
    gate = gate_ref[...]
    y = (gate[:, 0:1] * _slabs_to_rows(buf_ref.at[slot, 0], tt)
         + gate[:, 1:2] * _slabs_to_rows(buf_ref.at[slot, 1], tt))
    o_ref[...] = _layer_norm(DEEPNORM_ALPHA * x_ref[...] + y, g2_ref[...], b2_ref[...])


def _combine_call(layer, x1, out_sorted, plan, p, tt):
    n = x1.shape[0]

    def tok(i, l):
        return (i, 0)

    def lay3(i, l):
        return (l[0], 0, 0)

    n_tiles = n // tt
    grid_spec = pltpu.PrefetchScalarGridSpec(
        num_scalar_prefetch=1,
        grid=(n_tiles,),
        in_specs=[
            pl.BlockSpec((None, 1, 2 * tt), lambda i, l: (i, 0, 0), memory_space=pltpu.SMEM),
            pl.BlockSpec((None, 1, 2 * tt), lambda i, l: (jnp.minimum(i + 1, n_tiles - 1), 0, 0),
                         memory_space=pltpu.SMEM),
            pl.BlockSpec((tt, D_MODEL), tok),
            pl.BlockSpec((tt, 2), tok),
            pl.BlockSpec(memory_space=pl.ANY),
            pl.BlockSpec((None, 1, D_MODEL), lay3),
            pl.BlockSpec((None, 1, D_MODEL), lay3),
        ],
        out_specs=pl.BlockSpec((tt, D_MODEL), tok),
        scratch_shapes=[pltpu.VMEM((2, 2, tt * ROW_SLAB, LANES), F32),
                        pltpu.SemaphoreType.DMA((2,))],
    )
    return pl.pallas_call(
        _combine_kernel,
        grid_spec=grid_spec,
        out_shape=jax.ShapeDtypeStruct((n, D_MODEL), F32),
        compiler_params=_compiler_params(1),
        name="combine",
    )(layer, plan["pos"], plan["pos"], x1, plan["gates"], out_sorted, p["ln2_g"], p["ln2_b"])


def _rope_tables(max_len):
    t = jnp.arange(max_len, dtype=jnp.int32)
    row = (t // GRID_W).astype(F32)
    col = (t % GRID_W).astype(F32)
    n_pairs_axis = HEAD_DIM // 4
    inv_freq = 1.0 / (ROPE_THETA ** (jnp.arange(n_pairs_axis, dtype=F32) / n_pairs_axis))
    ang = jnp.concatenate([row[:, None] * inv_freq, col[:, None] * inv_freq], -1)
    cos, sin = jnp.cos(ang), jnp.sin(ang)
    reps = LANES // HEAD_DIM
    cos128 = jnp.tile(jnp.concatenate([cos, cos], -1), (1, reps))
    sin128 = jnp.tile(jnp.concatenate([-sin, sin], -1), (1, reps))
    return cos128, sin128


def _prepare_params(max_len, w_in, q_norm, k_norm, gmlp_ln_g, gmlp_ln_b, w_spatial, b_spatial,
                    attn_out_norm, gmlp_out_norm, w_out, ln1_g, ln1_b, w_router, router_bias,
                    w_gate, w_up, w_down, ln2_g, ln2_b):
    depth = w_in.shape[0]
    perm64 = jnp.concatenate([jnp.arange(0, HEAD_DIM, 2), jnp.arange(1, HEAD_DIM, 2)])
    n_rot_heads = N_Q_HEADS + N_KV_HEADS
    rot_cols = (jnp.arange(n_rot_heads)[:, None] * HEAD_DIM + perm64[None, :]).reshape(-1)
    cols = jnp.concatenate([rot_cols, jnp.arange(n_rot_heads * HEAD_DIM, IN_WIDTH)])
    cos128, sin128 = _rope_tables(max_len)
    head_of_lane = jnp.arange(ATTN_WIDTH) // HEAD_DIM
    row3 = lambda a: a.reshape(depth, 1, -1).astype(F32)
    wr_t = w_router.astype(F32).T
    wr_hi = wr_t.astype(BF16)
    max_offset = (1.01 * LOG2_E * HEAD_DIM ** 0.5
                  * jnp.max(jnp.abs(q_norm), axis=1) * jnp.max(jnp.abs(k_norm), axis=1))
    return {
        "attn_needs_exact_max": max_offset > MAX_SAFE_OFFSET,
        "w_in": w_in[:, :, cols].astype(BF16),
        "cos": cos128,
        "sin": sin128,
        "seg": (head_of_lane[:, None] == head_of_lane[None, :]).astype(BF16),
        "q_norm": row3(jnp.tile(q_norm[:, perm64], (1, N_Q_HEADS))),
        "k_norm": row3(jnp.tile(k_norm[:, perm64], (1, N_KV_HEADS))),
        "gmlp_ln_g": row3(gmlp_ln_g),
        "gmlp_ln_b": row3(gmlp_ln_b),
        "w_cat": w_spatial.transpose(0, 2, 1, 3).reshape(depth, CHUNK, N_GMLP_HEADS * CHUNK).astype(BF16),
        "b_sp": jnp.repeat(b_spatial.transpose(0, 2, 1), GMLP_HEAD_DIM, axis=2).astype(F32),
        "attn_out_norm": row3(attn_out_norm),
        "gmlp_out_norm": row3(gmlp_out_norm),
        "w_out": w_out.astype(BF16),
        "ln1_g": row3(ln1_g),
        "ln1_b": row3(ln1_b),
        "wr": jnp.concatenate([wr_hi, (wr_t - wr_hi.astype(F32)).astype(BF16)], axis=0),
        "router_bias": router_bias.astype(F32).reshape(N_EXPERTS, 1),
        "w_gate": w_gate.astype(BF16),
        "w_up": w_up.astype(BF16),
        "w_down": w_down.astype(BF16),
        "ln2_g": row3(ln2_g),
        "ln2_b": row3(ln2_b),
    }


def _trunk(x_tokens, groups, p, depth, tp, tt, tm, tq, tk):
    (_, n_seq0, s0), = groups[:1]
    n_prompt_blocks = (n_seq0 * s0) // tp
    prompt_period = s0 // tp
    sample_period = (groups[1][2] if len(groups) > 1 else s0) // tp
    periods = (n_prompt_blocks, prompt_period, sample_period)

    def layer_body(l, x):
        layer = jnp.reshape(l, (1,)).astype(jnp.int32)
        q, k, vt, gm, kstat = _proj_call(layer, x, p, tp, periods)

        def attention(exact_max):
            outs = [_attn_call(q, k, vt, kstat, start, n_seq, s_len, min(tq, s_len),
                               min(tk, s_len), tp, exact_max)
                    for (start, n_seq, s_len) in groups]
            return outs[0] if len(outs) == 1 else jnp.concatenate(outs, axis=0)

        attn = lax.cond(p["attn_needs_exact_max"][l],
                        functools.partial(attention, True), functools.partial(attention, False))
        x1, info, cnt = _post_call(layer, x, attn, gm, p, tt, tiles_per_step=2)
        plan = _plan_routes(info, cnt, tt, tm)
        xs = _scatter_call(x1, plan, tt, tm)
        out_sorted = _expert_call(layer, xs, plan, p, tm)
        return _combine_call(layer, x1, out_sorted, plan, p, tt)

    return lax.fori_loop(0, depth, layer_body, x_tokens)


def kernel(x_prompt, x_sample, w_in, q_norm, k_norm, gmlp_ln_g, gmlp_ln_b, w_spatial, b_spatial,
           attn_out_norm, gmlp_out_norm, w_out, ln1_g, ln1_b, w_router, router_bias,
           w_gate, w_up, w_down, ln2_g, ln2_b):
    bp, sp, d = x_prompt.shape
    bs, ss, _ = x_sample.shape
    assert sp % ss == 0 and (bp * sp) % ss == 0, "sample sequences must tile the prompt stream"
    p = _prepare_params(max(sp, ss), w_in, q_norm, k_norm, gmlp_ln_g, gmlp_ln_b, w_spatial,
                        b_spatial, attn_out_norm, gmlp_out_norm, w_out, ln1_g, ln1_b, w_router,
                        router_bias, w_gate, w_up, w_down, ln2_g, ln2_b)
    x = jnp.concatenate([x_prompt.reshape(bp * sp, d), x_sample.reshape(bs * ss, d)], axis=0)
    groups = ((0, bp, sp), (bp * sp, bs, ss))
    y = _trunk(x.astype(F32), groups, p, w_in.shape[0], tp=min(512, ss), tt=256, tm=512,
               tq=256, tk=512)
    return (y[:bp * sp].reshape(bp, sp, d), y[bp * sp:].reshape(bs, ss, d))
```

```python
import functools

import jax
import jax.numpy as jnp
from jax import lax
from jax.experimental import pallas as pl
from jax.experimental.pallas import tpu as pltpu

D_MODEL = 1024
DEPTH = 4
HEAD_DIM = 64
N_Q_HEADS = 8
N_KV_HEADS = 2
ATTN_WIDTH = N_Q_HEADS * HEAD_DIM
KV_WIDTH = N_KV_HEADS * HEAD_DIM
GMLP_WIDTH = 512
N_GMLP_HEADS = 8
GMLP_HEAD_DIM = 64
CHUNK = 128
IN_WIDTH = ATTN_WIDTH + 2 * KV_WIDTH + 2 * GMLP_WIDTH
N_EXPERTS = 16
EXPERTS_PER_GROUP = 4
EXPERT_FF = 512
ROPE_THETA = 10000.0
GRID_W = 64
EPS = 1e-6
DEEPNORM_ALPHA = float((2 * DEPTH) ** 0.25)
LOG2_E = 1.4426950408889634
STAT_ROWS = 8
MAX_SAFE_OFFSET = 60.0

LANES = 128
F32 = jnp.float32
BF16 = jnp.bfloat16
VMEM_LIMIT_BYTES = 48 * 1024 * 1024


def _compiler_params(n_grid_dims):
    return pltpu.CompilerParams(
        dimension_semantics=("arbitrary",) * n_grid_dims,
        vmem_limit_bytes=VMEM_LIMIT_BYTES,
    )


def _dot(a, b):
    return jnp.dot(a, b, preferred_element_type=F32)


def _layer_norm(x, g, b):
    mu = jnp.mean(x, axis=-1, keepdims=True)
    xc = x - mu
    var = jnp.mean(xc * xc, axis=-1, keepdims=True)
    return xc * lax.rsqrt(var + EPS) * g + b


def _rms_norm(x, g):
    return x * lax.rsqrt(jnp.mean(x * x, axis=-1, keepdims=True) + EPS) * g


def _head_rms_norm(x, seg, g):
    ssq = _dot((x * x).astype(BF16), seg)
    return x * lax.rsqrt(ssq * (1.0 / HEAD_DIM) + EPS) * g


def _rope_half_split(x, cos, sin_signed):
    first_half = (lax.broadcasted_iota(jnp.int32, (1, LANES), 1) % HEAD_DIM) < (HEAD_DIM // 2)
    cols = []
    for c in range(x.shape[1] // LANES):
        xc = x[:, c * LANES:(c + 1) * LANES]
        partner = jnp.where(first_half,
                            pltpu.roll(xc, LANES - HEAD_DIM // 2, 1),
                            pltpu.roll(xc, HEAD_DIM // 2, 1))
        cols.append(xc * cos + partner * sin_signed)
    return cols[0] if len(cols) == 1 else jnp.concatenate(cols, axis=1)


def _proj_kernel(l_ref, x_ref, w_in_ref, cos_ref, sin_ref, seg_ref, qn_ref, kn_ref,
                 lng_ref, lnb_ref, wcat_ref, bsp_ref, gon_ref,
                 q_ref, k_ref, vt_ref, gm_ref, kstat_ref):
    del l_ref
    tt = x_ref.shape[0]
    h = _dot(x_ref[...].astype(BF16), w_in_ref[...])
    cos = cos_ref[...]
    sin_signed = sin_ref[...]

    q = _head_rms_norm(h[:, :ATTN_WIDTH], seg_ref[...], qn_ref[...])
    q = _rope_half_split(q, cos, sin_signed) * (LOG2_E * HEAD_DIM ** -0.5)
    q_ref[...] = q.astype(BF16)

    k0 = ATTN_WIDTH
    seg_kv = seg_ref[:KV_WIDTH, :KV_WIDTH]
    k = _head_rms_norm(h[:, k0:k0 + KV_WIDTH], seg_kv, kn_ref[...])
    k = _rope_half_split(k, cos, sin_signed).astype(BF16)
    kf = k.astype(F32)
    k_sq = _dot((kf * kf).astype(BF16), seg_kv)
    kstat_ref[...] = jnp.broadcast_to(jnp.max(k_sq, axis=0, keepdims=True), kstat_ref.shape)
    for hk in range(N_KV_HEADS):
        k_ref[hk] = k[:, hk * HEAD_DIM:(hk + 1) * HEAD_DIM]

    v0 = k0 + KV_WIDTH
    vt_ref[...] = h[:, v0:v0 + KV_WIDTH].T.astype(BF16)

    u0 = v0 + KV_WIDTH
    u = jax.nn.gelu(h[:, u0:u0 + GMLP_WIDTH])
    vg = _layer_norm(jax.nn.gelu(h[:, u0 + GMLP_WIDTH:]), lng_ref[...], lnb_ref[...])
    lane_head = lax.broadcasted_iota(jnp.int32, (1, GMLP_WIDTH), 1) // GMLP_HEAD_DIM
    mixed = []
    for c in range(tt // CHUNK):
        vc = vg[c * CHUNK:(c + 1) * CHUNK].astype(BF16)
        stack = jnp.concatenate(
            [jnp.where(lane_head == hh, vc, jnp.zeros_like(vc)) for hh in range(N_GMLP_HEADS)],
            axis=0)
        mixed.append(_dot(wcat_ref[...], stack) + bsp_ref[...])
    mixed = mixed[0] if len(mixed) == 1 else jnp.concatenate(mixed, axis=0)
    gm_ref[...] = _rms_norm(u * mixed, gon_ref[...]).astype(BF16)


def _proj_call(layer, x, p, tt, rope_period_blocks):
    n = x.shape[0]
    n_prompt_blocks, prompt_period, sample_period = rope_period_blocks

    def tok(i, l):
        return (i, 0)

    def rope_idx(i, l):
        return (jnp.where(i < n_prompt_blocks, i % prompt_period, i % sample_period), 0)

    def lay3(i, l):
        return (l[0], 0, 0)

    def const2(i, l):
        return (0, 0)

    grid_spec = pltpu.PrefetchScalarGridSpec(
        num_scalar_prefetch=1,
        grid=(n // tt,),
        in_specs=[
            pl.BlockSpec((tt, D_MODEL), tok),
            pl.BlockSpec((None, D_MODEL, IN_WIDTH), lay3),
            pl.BlockSpec((tt, LANES), rope_idx),
            pl.BlockSpec((tt, LANES), rope_idx),
            pl.BlockSpec((ATTN_WIDTH, ATTN_WIDTH), const2),
            pl.BlockSpec((None, 1, ATTN_WIDTH), lay3),
            pl.BlockSpec((None, 1, KV_WIDTH), lay3),
            pl.BlockSpec((None, 1, GMLP_WIDTH), lay3),
            pl.BlockSpec((None, 1, GMLP_WIDTH), lay3),
            pl.BlockSpec((None, CHUNK, N_GMLP_HEADS * CHUNK), lay3),
            pl.BlockSpec((None, CHUNK, GMLP_WIDTH), lay3),
            pl.BlockSpec((None, 1, GMLP_WIDTH), lay3),
        ],
        out_specs=[
            pl.BlockSpec((tt, ATTN_WIDTH), tok),
            pl.BlockSpec((N_KV_HEADS, tt, HEAD_DIM), lambda i, l: (0, i, 0)),
            pl.BlockSpec((KV_WIDTH, tt), lambda i, l: (0, i)),
            pl.BlockSpec((tt, GMLP_WIDTH), tok),
            pl.BlockSpec((STAT_ROWS, KV_WIDTH), tok),
        ],
    )
    return pl.pallas_call(
        _proj_kernel,
        grid_spec=grid_spec,
        out_shape=[
            jax.ShapeDtypeStruct((n, ATTN_WIDTH), BF16),
            jax.ShapeDtypeStruct((N_KV_HEADS, n, HEAD_DIM), BF16),
            jax.ShapeDtypeStruct((KV_WIDTH, n), BF16),
            jax.ShapeDtypeStruct((n, GMLP_WIDTH), BF16),
            jax.ShapeDtypeStruct((n // tt * STAT_ROWS, KV_WIDTH), F32),
        ],
        compiler_params=_compiler_params(1),
        name="proj",
    )(layer, x, p["w_in"], p["cos"], p["sin"], p["seg"], p["q_norm"], p["k_norm"],
      p["gmlp_ln_g"], p["gmlp_ln_b"], p["w_cat"], p["b_sp"], p["gmlp_out_norm"])


def _scores(k_blk, q):
    return lax.dot_general(k_blk, q, (((1,), (1,)), ((), ())), preferred_element_type=F32)


def _attn_kernel(q_ref, k_ref, vt_ref, kstat_ref, o_ref, *, tk, exact_max):
    tq = q_ref.shape[0]
    s_len = k_ref.shape[0]
    heads = q_ref.shape[1] // HEAD_DIM
    q_all = q_ref[...]
    q = jnp.concatenate([q_all[:, g * HEAD_DIM:(g + 1) * HEAD_DIM] for g in range(heads)], axis=0)
    nq = heads * tq
    if exact_max:
        def max_body(i, m):
            k_blk = k_ref[pl.ds(pl.multiple_of(i * tk, tk), tk), :]
            return jnp.maximum(m, jnp.max(_scores(k_blk, q), axis=0, keepdims=True))
        off = lax.fori_loop(0, s_len // tk, max_body, jnp.full((1, nq), -jnp.inf, F32))
    else:
        hk = pl.program_id(1)
        lane = lax.broadcasted_iota(jnp.int32, (1, LANES), 1)
        kstat = jnp.max(kstat_ref[...], axis=0, keepdims=True)
        kmax2 = jnp.max(jnp.where(lane // HEAD_DIM == hk, kstat, 0.0), axis=1, keepdims=True)
        qf = q.astype(F32)
        qq = _scores(jnp.ones((8, HEAD_DIM), BF16), (qf * qf).astype(BF16))
        off = jnp.sqrt(qq[0:1] * kmax2)

    l8 = jnp.zeros((8, nq), F32)
    acc = jnp.zeros((HEAD_DIM, nq), F32)
    for c in range(s_len // tk):
        k_blk = k_ref[c * tk:(c + 1) * tk, :]
        vt_blk = vt_ref[:, c * tk:(c + 1) * tk]
        p = jnp.exp2(_scores(k_blk, q) - off)
        l8 = l8 + jnp.sum(p.reshape(tk // 8, 8, nq), axis=0)
        acc = acc + _dot(vt_blk, p.astype(BF16))
    out = acc / jnp.sum(l8, axis=0, keepdims=True)
    out_t = jnp.concatenate([out[:, g * tq:(g + 1) * tq] for g in range(heads)], axis=0)
    o_ref[...] = out_t.T.astype(o_ref.dtype)


def _attn_call(q, k, vt, kstat, token_start, n_seq, s_len, tq, tk, tt, exact_max):
    heads_per_step = 4
    q_cols = heads_per_step * HEAD_DIM
    steps_per_kv = (N_Q_HEADS // N_KV_HEADS) // heads_per_step
    q_blk0 = token_start // tq
    seq0 = token_start // s_len
    nq = s_len // tq
    stat_rows = s_len // tt * STAT_ROWS

    def q_idx(b, hk, gp, j):
        return (q_blk0 + b * nq + j, hk * steps_per_kv + gp)

    return pl.pallas_call(
        functools.partial(_attn_kernel, tk=tk, exact_max=exact_max),
        grid=(n_seq, N_KV_HEADS, steps_per_kv, nq),
        in_specs=[
            pl.BlockSpec((tq, q_cols), q_idx),
            pl.BlockSpec((None, s_len, HEAD_DIM), lambda b, hk, gp, j: (hk, seq0 + b, 0)),
            pl.BlockSpec((HEAD_DIM, s_len), lambda b, hk, gp, j: (hk, seq0 + b)),
            pl.BlockSpec((stat_rows, KV_WIDTH), lambda b, hk, gp, j: (seq0 + b, 0)),
        ],
        out_specs=pl.BlockSpec((tq, q_cols),
                               lambda b, hk, gp, j: (b * nq + j, hk * steps_per_kv + gp)),
        out_shape=jax.ShapeDtypeStruct((n_seq * s_len, ATTN_WIDTH), BF16),
        compiler_params=_compiler_params(4),
        name="attn_exact_max" if exact_max else "attn",
    )(q, k, vt, kstat)


def _group_partner(x, row, d):
    pos = row % EXPERTS_PER_GROUP
    return jnp.where(pos + d < EXPERTS_PER_GROUP,
                     pltpu.roll(x, N_EXPERTS - d, 0),
                     pltpu.roll(x, EXPERTS_PER_GROUP - d, 0))


def _route(scores, bias, row):
    biased = scores + bias
    rank = jnp.zeros(biased.shape, jnp.int32)
    pos = row % EXPERTS_PER_GROUP
    for d in range(1, EXPERTS_PER_GROUP):
        other = _group_partner(biased, row, d)
        other_is_lower = (pos + d) >= EXPERTS_PER_GROUP
        beats = (other > biased) | ((other == biased) & other_is_lower)
        rank = rank + beats.astype(jnp.int32)
    top2 = rank < 2
    contrib = jnp.where(top2, biased, 0.0)
    group_score = contrib
    for d in range(1, EXPERTS_PER_GROUP):
        group_score = group_score + _group_partner(contrib, row, d)
    n_groups = N_EXPERTS // EXPERTS_PER_GROUP
    gpos = row // EXPERTS_PER_GROUP
    losses = jnp.zeros(biased.shape, jnp.int32)
    for d in range(1, n_groups):
        other = pltpu.roll(group_score, N_EXPERTS - d * EXPERTS_PER_GROUP, 0)
        other_is_lower = (gpos + d) >= n_groups
        beats = (other > group_score) | ((other == group_score) & other_is_lower)
        losses = losses + beats.astype(jnp.int32)
    selected = top2 & (losses == 0)
    w = jnp.where(selected, scores, 0.0)
    return w / jnp.sum(w, axis=0, keepdims=True), selected


INFO_EXPERT, INFO_RANK, INFO_GATE, INFO_ROWS = 0, 2, 4, 8


def _post_kernel(l_ref, x_ref, a_ref, gm_ref, aon_ref, wo_ref, g1_ref, b1_ref,
                 wr_ref, rb_ref, x1_ref, info_ref, cnt_ref, *, tt):
    del l_ref
    ts = x_ref.shape[0]
    a = _rms_norm(a_ref[...].astype(F32), aon_ref[...]).astype(BF16)
    mix = _dot(a, wo_ref[:ATTN_WIDTH, :]) + _dot(gm_ref[...], wo_ref[ATTN_WIDTH:, :])
    x1 = _layer_norm(DEEPNORM_ALPHA * x_ref[...] + mix, g1_ref[...], b1_ref[...])
    x1_ref[...] = x1
    x_hi = x1.astype(BF16)
    x_lo = (x1 - x_hi.astype(F32)).astype(BF16)
    by_hi = _scores(wr_ref[...], x_hi)
    logits = (by_hi[:N_EXPERTS] + by_hi[N_EXPERTS:]
              + _scores(wr_ref[:N_EXPERTS, :], x_lo))
    row = lax.broadcasted_iota(jnp.int32, (N_EXPERTS, 1), 0)
    gates, selected = _route(jax.nn.sigmoid(logits), rb_ref[...], row)

    sel = jnp.where(selected, 1.0, 0.0)
    earlier = lax.broadcasted_iota(jnp.int32, (ts, ts), 0)
    later = lax.broadcasted_iota(jnp.int32, (ts, ts), 1)
    prefix = jnp.where((earlier <= later) & (earlier // tt == later // tt), 1.0, 0.0).astype(BF16)
    incl = _dot(sel.astype(BF16), prefix)
    rank = incl - sel
    for t in range(ts // tt):
        last = (t + 1) * tt - 1
        cnt_ref[t * N_EXPERTS:(t + 1) * N_EXPERTS] = jnp.broadcast_to(
            incl[:, last:last + 1], (N_EXPERTS, LANES))

    row_f = row.astype(F32)
    e_lo = jnp.min(jnp.where(selected, row_f, float(N_EXPERTS)), axis=0, keepdims=True)
    e_hi = jnp.max(jnp.where(selected, row_f, -1.0), axis=0, keepdims=True)

    def pick(mat, e):
        return jnp.sum(jnp.where(row_f == e, mat, 0.0), axis=0, keepdims=True)

    fields = {INFO_EXPERT: e_lo, INFO_EXPERT + 1: e_hi,
              INFO_RANK: pick(rank, e_lo), INFO_RANK + 1: pick(rank, e_hi),
              INFO_GATE: pick(gates, e_lo), INFO_GATE + 1: pick(gates, e_hi)}
    info_row = lax.broadcasted_iota(jnp.int32, (INFO_ROWS, 1), 0)
    info = jnp.zeros((INFO_ROWS, ts), F32)
    for r, val in fields.items():
        info = jnp.where(info_row == r, val, info)
    info_ref[...] = info


def _post_call(layer, x, attn, gm, p, tt, tiles_per_step):
    n = x.shape[0]
    ts = tt * tiles_per_step

    def tok(i, l):
        return (i, 0)

    def lay3(i, l):
        return (l[0], 0, 0)

    def const2(i, l):
        return (0, 0)

    grid_spec = pltpu.PrefetchScalarGridSpec(
        num_scalar_prefetch=1,
        grid=(n // ts,),
        in_specs=[
            pl.BlockSpec((ts, D_MODEL), tok),
            pl.BlockSpec((ts, ATTN_WIDTH), tok),
            pl.BlockSpec((ts, GMLP_WIDTH), tok),
            pl.BlockSpec((None, 1, ATTN_WIDTH), lay3),
            pl.BlockSpec((None, D_MODEL, D_MODEL), lay3),
            pl.BlockSpec((None, 1, D_MODEL), lay3),
            pl.BlockSpec((None, 1, D_MODEL), lay3),
            pl.BlockSpec((2 * N_EXPERTS, D_MODEL), const2),
            pl.BlockSpec((N_EXPERTS, 1), const2),
        ],
        out_specs=[pl.BlockSpec((ts, D_MODEL), tok),
                   pl.BlockSpec((INFO_ROWS, ts), lambda i, l: (0, i)),
                   pl.BlockSpec((tiles_per_step * N_EXPERTS, LANES), tok)],
    )
    return pl.pallas_call(
        functools.partial(_post_kernel, tt=tt),
        grid_spec=grid_spec,
        out_shape=[jax.ShapeDtypeStruct((n, D_MODEL), F32),
                   jax.ShapeDtypeStruct((INFO_ROWS, n), F32),
                   jax.ShapeDtypeStruct((n // tt * N_EXPERTS, LANES), F32)],
        compiler_params=_compiler_params(1),
        name="post",
    )(layer, x, attn, gm, p["attn_out_norm"], p["w_out"], p["ln1_g"], p["ln1_b"],
      p["wr"], p["router_bias"])


RUN = 8


def _local_rows(tt):
    return -(-(2 * tt + N_EXPERTS * (RUN - 1)) // LANES) * LANES


def _plan_routes(info, cnt, tt, tm):
    n = info.shape[1]
    n_tiles = n // tt
    expert = info[INFO_EXPERT:INFO_EXPERT + 2].T.astype(jnp.int32)
    rank = info[INFO_RANK:INFO_RANK + 2].T.astype(jnp.int32)
    counts = cnt.reshape(n_tiles, N_EXPERTS, LANES)[:, :, 0].astype(jnp.int32)
    run_rows = (counts + RUN - 1) // RUN * RUN
    local_off = jnp.cumsum(run_rows, axis=1) - run_rows
    totals = jnp.sum(run_rows, axis=0)
    seg_tiles = (totals + tm - 1) // tm
    seg_end = jnp.cumsum(seg_tiles)
    expert_row0 = (seg_end - seg_tiles) * tm
    tile_base = expert_row0[None, :] + jnp.cumsum(run_rows, axis=0) - run_rows
    is_expert = expert.reshape(n_tiles, tt, 2, 1) == jnp.arange(N_EXPERTS, dtype=jnp.int32)

    def per_slot(table):
        return jnp.sum(jnp.where(is_expert, table[:, None, None, :], 0), axis=-1)

    rank3 = rank.reshape(n_tiles, tt, 2)
    pos = per_slot(tile_base) + rank3
    lpos = per_slot(local_off) + rank3
    n_row_tiles = (2 * n + n_tiles * N_EXPERTS * (RUN - 1)) // tm + N_EXPERTS + 1
    used = seg_end[-1]
    t_idx = jnp.arange(n_row_tiles, dtype=jnp.int32)
    tile_blk = jnp.minimum(t_idx, used - 1)
    tile_expert = jnp.minimum(
        jnp.sum((tile_blk[:, None] >= seg_end[None, :]).astype(jnp.int32), axis=1), N_EXPERTS - 1)
    return {
        "pos": pos.reshape(n_tiles, 1, 2 * tt).astype(jnp.int32),
        "lpos": lpos.transpose(0, 2, 1).astype(jnp.int32),
        "runs": jnp.stack([local_off, tile_base, run_rows // RUN], axis=1).astype(jnp.int32),
        "gates": info[INFO_GATE:INFO_GATE + 2].T,
        "pads": jnp.stack([jnp.append(expert_row0 + totals, used),
                           jnp.append((seg_tiles * tm - totals) // RUN,
                                      n_row_tiles - used)]).astype(jnp.int32),
        "tile_expert": tile_expert.astype(jnp.int32),
        "tile_valid": (t_idx < used).astype(jnp.int32),
        "tile_blk": tile_blk.astype(jnp.int32),
        "n_rows": n_row_tiles * tm,
    }


ROW_SLAB = 8
assert ROW_SLAB * LANES == D_MODEL


def _slab_copy(src_ref, src_row, dst_ref, dst_row, sem):
    src = src_ref.at[pl.ds(pl.multiple_of(src_row * ROW_SLAB, ROW_SLAB), ROW_SLAB)]
    dst = dst_ref.at[pl.ds(pl.multiple_of(dst_row * ROW_SLAB, ROW_SLAB), ROW_SLAB)]
    return pltpu.make_async_copy(src, dst, sem)


def _rows_to_slabs(x, slab_ref):
    rows = x.shape[0]
    for s in range(ROW_SLAB):
        slab_ref[pl.ds(s, rows, stride=ROW_SLAB), :] = x[:, s * LANES:(s + 1) * LANES]


def _slabs_to_rows(slab_ref, rows):
    return jnp.concatenate(
        [slab_ref[pl.ds(s, rows, stride=ROW_SLAB), :] for s in range(ROW_SLAB)], axis=1)


DMA_UNROLL = 8


def _scatter_kernel(pads_ref, runs_ref, runs_prev_ref, lpos_ref, x_ref, xs_ref, loc_ref, sems,
                    *, tm):
    i = pl.program_id(0)
    last = pl.num_programs(0) - 1
    slot = i % 2
    n_local = loc_ref.shape[1]

    def unit_copy(src_ref, src_row, dst_row, sem):
        return pltpu.make_async_copy(
            src_ref.at[pl.ds(pl.multiple_of(src_row, RUN), RUN)],
            xs_ref.at[pl.ds(pl.multiple_of(dst_row, RUN), RUN)], sem)

    @pl.when(i == 0)
    def _():
        zeros = loc_ref.at[1]
        zeros[...] = jnp.zeros(zeros.shape, zeros.dtype)
        for e in range(N_EXPERTS):
            start, units = pads_ref[0, e], pads_ref[1, e]

            def fill(j, c, start=start):
                unit_copy(zeros, 0, start + j * RUN, sems.at[2]).start()
                return c

            def fill_wait(j, c):
                unit_copy(zeros, 0, 0, sems.at[2]).wait()
                return c

            lax.fori_loop(0, units, fill, 0)
            lax.fori_loop(0, units, fill_wait, 0)

        first_tail_tile, n_tail_tiles = pads_ref[0, N_EXPERTS], pads_ref[1, N_EXPERTS]

        def tail_copy(j):
            row0 = pl.multiple_of((first_tail_tile + j) * tm, tm)
            return pltpu.make_async_copy(zeros.at[pl.ds(0, tm)], xs_ref.at[pl.ds(row0, tm)],
                                         sems.at[2])

        def tail_fill(j, c):
            tail_copy(j).start()
            return c

        def tail_wait(j, c):
            tail_copy(j).wait()
            return c

        lax.fori_loop(0, n_tail_tiles, tail_fill, 0)
        lax.fori_loop(0, n_tail_tiles, tail_wait, 0)

    lpos = lpos_ref[...]
    row = lax.broadcasted_iota(jnp.int32, (n_local, 1), 0)
    onehot = jnp.where((row == lpos[0:1, :]) | (row == lpos[1:2, :]), 1.0, 0.0).astype(BF16)
    loc_ref[slot] = _dot(onehot, x_ref[...].astype(BF16))

    for e in range(N_EXPERTS):
        src0, dst0, units = runs_ref[0, e], runs_ref[1, e], runs_ref[2, e]

        def issue(k, c, src0=src0, dst0=dst0):
            unit_copy(loc_ref.at[slot], src0 + k * RUN, dst0 + k * RUN, sems.at[slot]).start()
            return c

        lax.fori_loop(0, units, issue, 0)

    def drain(which, r_ref):
        n_units = r_ref[2, 0]
        for e in range(1, N_EXPERTS):
            n_units = n_units + r_ref[2, e]

        def body(k, c):
            unit_copy(loc_ref.at[which], 0, 0, sems.at[which]).wait()
            return c

        lax.fori_loop(0, n_units, body, 0)

    @pl.when(i > 0)
    def _():
        drain(1 - slot, runs_prev_ref)

    @pl.when(i == last)
    def _():
        drain(slot, runs_ref)


def _scatter_call(x1, plan, tt, tm):
    n = x1.shape[0]
    smem_runs = lambda index_map: pl.BlockSpec((None, 3, N_EXPERTS), index_map,
                                               memory_space=pltpu.SMEM)
    grid_spec = pltpu.PrefetchScalarGridSpec(
        num_scalar_prefetch=1,
        grid=(n // tt,),
        in_specs=[
            smem_runs(lambda i, pads: (i, 0, 0)),
            smem_runs(lambda i, pads: (jnp.maximum(i - 1, 0), 0, 0)),
            pl.BlockSpec((None, 2, tt), lambda i, pads: (i, 0, 0)),
            pl.BlockSpec((tt, D_MODEL), lambda i, pads: (i, 0)),
        ],
        out_specs=pl.BlockSpec(memory_space=pl.ANY),
        scratch_shapes=[pltpu.VMEM((2, max(_local_rows(tt), tm), D_MODEL), F32),
                        pltpu.SemaphoreType.DMA((3,))],
    )
    return pl.pallas_call(
        functools.partial(_scatter_kernel, tm=tm),
        grid_spec=grid_spec,
        out_shape=jax.ShapeDtypeStruct((plan["n_rows"], D_MODEL), F32),
        compiler_params=_compiler_params(1),
        name="scatter",
    )(plan["pads"], plan["runs"], plan["runs"], plan["lpos"], x1)


def _expert_kernel(l_ref, te_ref, tv_ref, tb_ref, xs_ref, wg_ref, wu_ref, wd_ref, o_ref):
    del l_ref, te_ref, tb_ref
    valid = tv_ref[pl.program_id(0)] != 0

    @pl.when(valid)
    def _():
        xb = xs_ref[...].astype(BF16)
        hidden = jax.nn.silu(_dot(xb, wg_ref[...])) * _dot(xb, wu_ref[...])
        _rows_to_slabs(_dot(hidden.astype(BF16), wd_ref[...]), o_ref)

    @pl.when(jnp.logical_not(valid))
    def _():
        o_ref[...] = jnp.zeros_like(o_ref)


def _expert_call(layer, xs, plan, p, tm):
    n_rows = xs.shape[0]

    def rows(t, l, te, tv, tb):
        return (tb[t], 0)

    def weights(t, l, te, tv, tb):
        return (l[0], te[t], 0, 0)

    grid_spec = pltpu.PrefetchScalarGridSpec(
        num_scalar_prefetch=4,
        grid=(n_rows // tm,),
        in_specs=[
            pl.BlockSpec((tm, D_MODEL), rows),
            pl.BlockSpec((None, None, D_MODEL, EXPERT_FF), weights),
            pl.BlockSpec((None, None, D_MODEL, EXPERT_FF), weights),
            pl.BlockSpec((None, None, EXPERT_FF, D_MODEL), weights),
        ],
        out_specs=pl.BlockSpec((tm * ROW_SLAB, LANES), lambda t, l, te, tv, tb: (t, 0)),
    )
    return pl.pallas_call(
        _expert_kernel,
        grid_spec=grid_spec,
        out_shape=jax.ShapeDtypeStruct((n_rows * ROW_SLAB, LANES), F32),
        compiler_params=_compiler_params(1),
        name="experts",
    )(layer, plan["tile_expert"], plan["tile_valid"], plan["tile_blk"], xs,
      p["w_gate"], p["w_up"], p["w_down"])


def _combine_kernel(l_ref, pos_ref, pos_next_ref, x_ref, gate_ref, os_ref, g2_ref, b2_ref,
                    o_ref, buf_ref, sems):
    del l_ref
    tt = x_ref.shape[0]
    i = pl.program_id(0)
    slot = i % 2

    def gather(p_ref, which):
        def body(r, c):
            for s in range(2):
                _slab_copy(os_ref, p_ref[0, 2 * r + s], buf_ref.at[which, s], r,
                           sems.at[which]).start(priority=s)
            return c
        lax.fori_loop(0, tt, body, 0, unroll=DMA_UNROLL)

    @pl.when(i == 0)
    def _():
        gather(pos_ref, 0)

    @pl.when(i + 1 < pl.num_programs(0))
    def _():
        gather(pos_next_ref, 1 - slot)

    def drain(r, c):
        for s in range(2):
            _slab_copy(os_ref, 0, buf_ref.at[slot, s], 0, sems.at[slot]).wait()
        return c

    lax.fori_loop(0, tt, drain, 0, unroll=DMA_UNROLL)
    gate = gate_ref[...]
    y = (gate[:, 0:1] * _slabs_to_rows(buf_ref.at[slot, 0], tt)
         + gate[:, 1:2] * _slabs_to_rows(buf_ref.at[slot, 1], tt))
    o_ref[...] = _layer_norm(DEEPNORM_ALPHA * x_ref[...] + y, g2_ref[...], b2_ref[...])


def _combine_call(layer, x1, out_sorted, plan, p, tt):
    n = x1.shape[0]

    def tok(i, l):
        return (i, 0)

    def lay3(i, l):
        return (l[0], 0, 0)

    n_tiles = n // tt
    grid_spec = pltpu.PrefetchScalarGridSpec(
        num_scalar_prefetch=1,
        grid=(n_tiles,),
        in_specs=[
            pl.BlockSpec((None, 1, 2 * tt), lambda i, l: (i, 0, 0), memory_space=pltpu.SMEM),
            pl.BlockSpec((None, 1, 2 * tt), lambda i, l: (jnp.minimum(i + 1, n_tiles - 1), 0, 0),
                         memory_space=pltpu.SMEM),
            pl.BlockSpec((tt, D_MODEL), tok),
            pl.BlockSpec((tt, 2), tok),
            pl.BlockSpec(memory_space=pl.ANY),
            pl.BlockSpec((None, 1, D_MODEL), lay3),
            pl.BlockSpec((None, 1, D_MODEL), lay3),
        ],
        out_specs=pl.BlockSpec((tt, D_MODEL), tok),
        scratch_shapes=[pltpu.VMEM((2, 2, tt * ROW_SLAB, LANES), F32),
                        pltpu.SemaphoreType.DMA((2,))],
    )
    return pl.pallas_call(
        _combine_kernel,
        grid_spec=grid_spec,
        out_shape=jax.ShapeDtypeStruct((n, D_MODEL), F32),
        compiler_params=_compiler_params(1),
        name="combine",
    )(layer, plan["pos"], plan["pos"], x1, plan["gates"], out_sorted, p["ln2_g"], p["ln2_b"])


def _rope_tables(max_len):
    t = jnp.arange(max_len, dtype=jnp.int32)
    row = (t // GRID_W).astype(F32)
    col = (t % GRID_W).astype(F32)
    n_pairs_axis = HEAD_DIM // 4
    inv_freq = 1.0 / (ROPE_THETA ** (jnp.arange(n_pairs_axis, dtype=F32) / n_pairs_axis))
    ang = jnp.concatenate([row[:, None] * inv_freq, col[:, None] * inv_freq], -1)
    cos, sin = jnp.cos(ang), jnp.sin(ang)
    reps = LANES // HEAD_DIM
    cos128 = jnp.tile(jnp.concatenate([cos, cos], -1), (1, reps))
    sin128 = jnp.tile(jnp.concatenate([-sin, sin], -1), (1, reps))
    return cos128, sin128


def _prepare_params(max_len, w_in, q_norm, k_norm, gmlp_ln_g, gmlp_ln_b, w_spatial, b_spatial,
                    attn_out_norm, gmlp_out_norm, w_out, ln1_g, ln1_b, w_router, router_bias,
                    w_gate, w_up, w_down, ln2_g, ln2_b):
    depth = w_in.shape[0]
    perm64 = jnp.concatenate([jnp.arange(0, HEAD_DIM, 2), jnp.arange(1, HEAD_DIM, 2)])
    n_rot_heads = N_Q_HEADS + N_KV_HEADS
    rot_cols = (jnp.arange(n_rot_heads)[:, None] * HEAD_DIM + perm64[None, :]).reshape(-1)
    cols = jnp.concatenate([rot_cols, jnp.arange(n_rot_heads * HEAD_DIM, IN_WIDTH)])
    cos128, sin128 = _rope_tables(max_len)
    head_of_lane = jnp.arange(ATTN_WIDTH) // HEAD_DIM
    row3 = lambda a: a.reshape(depth, 1, -1).astype(F32)
    wr_t = w_router.astype(F32).T
    wr_hi = wr_t.astype(BF16)
    max_offset = (1.01 * LOG2_E * HEAD_DIM ** 0.5
                  * jnp.max(jnp.abs(q_norm), axis=1) * jnp.max(jnp.abs(k_norm), axis=1))
    return {
        "attn_needs_exact_max": max_offset > MAX_SAFE_OFFSET,
        "w_in": w_in[:, :, cols].astype(BF16),
        "cos": cos128,
        "sin": sin128,
        "seg": (head_of_lane[:, None] == head_of_lane[None, :]).astype(BF16),
        "q_norm": row3(jnp.tile(q_norm[:, perm64], (1, N_Q_HEADS))),
        "k_norm": row3(jnp.tile(k_norm[:, perm64], (1, N_KV_HEADS))),
        "gmlp_ln_g": row3(gmlp_ln_g),
        "gmlp_ln_b": row3(gmlp_ln_b),
        "w_cat": w_spatial.transpose(0, 2, 1, 3).reshape(depth, CHUNK, N_GMLP_HEADS * CHUNK).astype(BF16),
        "b_sp": jnp.repeat(b_spatial.transpose(0, 2, 1), GMLP_HEAD_DIM, axis=2).astype(F32),
        "attn_out_norm": row3(attn_out_norm),
        "gmlp_out_norm": row3(gmlp_out_norm),
        "w_out": w_out.astype(BF16),
        "ln1_g": row3(ln1_g),
        "ln1_b": row3(ln1_b),
        "wr": jnp.concatenate([wr_hi, (wr_t - wr_hi.astype(F32)).astype(BF16)], axis=0),
        "router_bias": router_bias.astype(F32).reshape(N_EXPERTS, 1),
        "w_gate": w_gate.astype(BF16),
        "w_up": w_up.astype(BF16),
        "w_down": w_down.astype(BF16),
        "ln2_g": row3(ln2_g),
        "ln2_b": row3(ln2_b),
    }


def _trunk(x_tokens, groups, p, depth, tp, tt, tm, tq, tk):
    (_, n_seq0, s0), = groups[:1]
    n_prompt_blocks = (n_seq0 * s0) // tp
    prompt_period = s0 // tp
    sample_period = (groups[1][2] if len(groups) > 1 else s0) // tp
    periods = (n_prompt_blocks, prompt_period, sample_period)

    def layer_body(l, x):
        layer = jnp.reshape(l, (1,)).astype(jnp.int32)
        q, k, vt, gm, kstat = _proj_call(layer, x, p, tp, periods)

        def attention(exact_max):
            outs = [_attn_call(q, k, vt, kstat, start, n_seq, s_len, min(tq, s_len),
                               min(tk, s_len), tp, exact_max)
                    for (start, n_seq, s_len) in groups]
            return outs[0] if len(outs) == 1 else jnp.concatenate(outs, axis=0)

        attn = lax.cond(p["attn_needs_exact_max"][l],
                        functools.partial(attention, True), functools.partial(attention, False))
        x1, info, cnt = _post_call(layer, x, attn, gm, p, tt, tiles_per_step=2)
        plan = _plan_routes(info, cnt, tt, tm)
        xs = _scatter_call(x1, plan, tt, tm)
        out_sorted = _expert_call(layer, xs, plan, p, tm)
        return _combine_call(layer, x1, out_sorted, plan, p, tt)

    return lax.fori_loop(0, depth, layer_body, x_tokens)


def kernel(x_prompt, x_sample, w_in, q_norm, k_norm, gmlp_ln_g, gmlp_ln_b, w_spatial, b_spatial,
           attn_out_norm, gmlp_out_norm, w_out, ln1_g, ln1_b, w_router, router_bias,
           w_gate, w_up, w_down, ln2_g, ln2_b):
    bp, sp, d = x_prompt.shape
    bs, ss, _ = x_sample.shape
    assert sp % ss == 0 and (bp * sp) % ss == 0, "sample sequences must tile the prompt stream"
    p = _prepare_params(max(sp, ss), w_in, q_norm, k_norm, gmlp_ln_g, gmlp_ln_b, w_spatial,
                        b_spatial, attn_out_norm, gmlp_out_norm, w_out, ln1_g, ln1_b, w_router,
                        router_bias, w_gate, w_up, w_down, ln2_g, ln2_b)
    x = jnp.concatenate([x_prompt.reshape(bp * sp, d), x_sample.reshape(bs * ss, d)], axis=0)
    groups = ((0, bp, sp), (bp * sp, bs, ss))
    y = _trunk(x.astype(F32), groups, p, w_in.shape[0], tp=min(512, ss), tt=256, tm=512,
               tq=256, tk=512)
    return (y[:bp * sp].reshape(bp, sp, d), y[bp * sp:].reshape(bs, ss, d))
```

```python
import functools

import jax
import jax.numpy as jnp
from jax import lax
from jax.experimental import pallas as pl
from jax.experimental.pallas import tpu as pltpu

D_MODEL = 1024
DEPTH = 4
HEAD_DIM = 64
N_Q_HEADS = 8
N_KV_HEADS = 2
ATTN_WIDTH = N_Q_HEADS * HEAD_DIM
KV_WIDTH = N_KV_HEADS * HEAD_DIM
GMLP_WIDTH = 512
N_GMLP_HEADS = 8
GMLP_HEAD_DIM = 64
CHUNK = 128
IN_WIDTH = ATTN_WIDTH + 2 * KV_WIDTH + 2 * GMLP_WIDTH
N_EXPERTS = 16
EXPERTS_PER_GROUP = 4
EXPERT_FF = 512
ROPE_THETA = 10000.0
GRID_W = 64
EPS = 1e-6
DEEPNORM_ALPHA = float((2 * DEPTH) ** 0.25)
LOG2_E = 1.4426950408889634
STAT_ROWS = 8
MAX_SAFE_OFFSET = 60.0

LANES = 128
F32 = jnp.float32
BF16 = jnp.bfloat16
VMEM_LIMIT_BYTES = 48 * 1024 * 1024


def _compiler_params(n_grid_dims):
    return pltpu.CompilerParams(
        dimension_semantics=("arbitrary",) * n_grid_dims,
        vmem_limit_bytes=VMEM_LIMIT_BYTES,
    )


def _dot(a, b):
    return jnp.dot(a, b, preferred_element_type=F32)


def _layer_norm(x, g, b):
    mu = jnp.mean(x, axis=-1, keepdims=True)
    xc = x - mu
    var = jnp.mean(xc * xc, axis=-1, keepdims=True)
    return xc * lax.rsqrt(var + EPS) * g + b


def _rms_norm(x, g):
    return x * lax.rsqrt(jnp.mean(x * x, axis=-1, keepdims=True) + EPS) * g


def _head_rms_norm(x, seg, g):
    ssq = _dot((x * x).astype(BF16), seg)
    return x * lax.rsqrt(ssq * (1.0 / HEAD_DIM) + EPS) * g


def _rope_half_split(x, cos, sin_signed):
    first_half = (lax.broadcasted_iota(jnp.int32, (1, LANES), 1) % HEAD_DIM) < (HEAD_DIM // 2)
    cols = []
    for c in range(x.shape[1] // LANES):
        xc = x[:, c * LANES:(c + 1) * LANES]
        partner = jnp.where(first_half,
                            pltpu.roll(xc, LANES - HEAD_DIM // 2, 1),
                            pltpu.roll(xc, HEAD_DIM // 2, 1))
        cols.append(xc * cos + partner * sin_signed)
    return cols[0] if len(cols) == 1 else jnp.concatenate(cols, axis=1)


def _proj_kernel(l_ref, x_ref, w_in_ref, cos_ref, sin_ref, seg_ref, qn_ref, kn_ref,
                 lng_ref, lnb_ref, wcat_ref, bsp_ref, gon_ref,
                 q_ref, k_ref, vt_ref, gm_ref, kstat_ref):
    del l_ref
    tt = x_ref.shape[0]
    h = _dot(x_ref[...].astype(BF16), w_in_ref[...])
    cos = cos_ref[...]
    sin_signed = sin_ref[...]

    q = _head_rms_norm(h[:, :ATTN_WIDTH], seg_ref[...], qn_ref[...])
    q = _rope_half_split(q, cos, sin_signed) * (LOG2_E * HEAD_DIM ** -0.5)
    q_ref[...] = q.astype(BF16)

    k0 = ATTN_WIDTH
    seg_kv = seg_ref[:KV_WIDTH, :KV_WIDTH]
    k = _head_rms_norm(h[:, k0:k0 + KV_WIDTH], seg_kv, kn_ref[...])
    k = _rope_half_split(k, cos, sin_signed).astype(BF16)
    kf = k.astype(F32)
    k_sq = _dot((kf * kf).astype(BF16), seg_kv)
    kstat_ref[...] = jnp.broadcast_to(jnp.max(k_sq, axis=0, keepdims=True), kstat_ref.shape)
    for hk in range(N_KV_HEADS):
        k_ref[hk] = k[:, hk * HEAD_DIM:(hk + 1) * HEAD_DIM]

    v0 = k0 + KV_WIDTH
    vt_ref[...] = h[:, v0:v0 + KV_WIDTH].T.astype(BF16)

    u0 = v0 + KV_WIDTH
    u = jax.nn.gelu(h[:, u0:u0 + GMLP_WIDTH])
    vg = _layer_norm(jax.nn.gelu(h[:, u0 + GMLP_WIDTH:]), lng_ref[...], lnb_ref[...])
    lane_head = lax.broadcasted_iota(jnp.int32, (1, GMLP_WIDTH), 1) // GMLP_HEAD_DIM
    mixed = []
    for c in range(tt // CHUNK):
        vc = vg[c * CHUNK:(c + 1) * CHUNK].astype(BF16)
        stack = jnp.concatenate(
            [jnp.where(lane_head == hh, vc, jnp.zeros_like(vc)) for hh in range(N_GMLP_HEADS)],
            axis=0)
        mixed.append(_dot(wcat_ref[...], stack) + bsp_ref[...])
    mixed = mixed[0] if len(mixed) == 1 else jnp.concatenate(mixed, axis=0)
    gm_ref[...] = _rms_norm(u * mixed, gon_ref[...]).astype(BF16)


def _proj_call(layer, x, p, tt, rope_period_blocks):
    n = x.shape[0]
    n_prompt_blocks, prompt_period, sample_period = rope_period_blocks

    def tok(i, l):
        return (i, 0)

    def rope_idx(i, l):
        return (jnp.where(i < n_prompt_blocks, i % prompt_period, i % sample_period), 0)

    def lay3(i, l):
        return (l[0], 0, 0)

    def const2(i, l):
        return (0, 0)

    grid_spec = pltpu.PrefetchScalarGridSpec(
        num_scalar_prefetch=1,
        grid=(n // tt,),
        in_specs=[
            pl.BlockSpec((tt, D_MODEL), tok),
            pl.BlockSpec((None, D_MODEL, IN_WIDTH), lay3),
            pl.BlockSpec((tt, LANES), rope_idx),
            pl.BlockSpec((tt, LANES), rope_idx),
            pl.BlockSpec((ATTN_WIDTH, ATTN_WIDTH), const2),
            pl.BlockSpec((None, 1, ATTN_WIDTH), lay3),
            pl.BlockSpec((None, 1, KV_WIDTH), lay3),
            pl.BlockSpec((None, 1, GMLP_WIDTH), lay3),
            pl.BlockSpec((None, 1, GMLP_WIDTH), lay3),
            pl.BlockSpec((None, CHUNK, N_GMLP_HEADS * CHUNK), lay3),
            pl.BlockSpec((None, CHUNK, GMLP_WIDTH), lay3),
            pl.BlockSpec((None, 1, GMLP_WIDTH), lay3),
        ],
        out_specs=[
            pl.BlockSpec((tt, ATTN_WIDTH), tok),
            pl.BlockSpec((N_KV_HEADS, tt, HEAD_DIM), lambda i, l: (0, i, 0)),
            pl.BlockSpec((KV_WIDTH, tt), lambda i, l: (0, i)),
            pl.BlockSpec((tt, GMLP_WIDTH), tok),
            pl.BlockSpec((STAT_ROWS, KV_WIDTH), tok),
        ],
    )
    return pl.pallas_call(
        _proj_kernel,
        grid_spec=grid_spec,
        out_shape=[
            jax.ShapeDtypeStruct((n, ATTN_WIDTH), BF16),
            jax.ShapeDtypeStruct((N_KV_HEADS, n, HEAD_DIM), BF16),
            jax.ShapeDtypeStruct((KV_WIDTH, n), BF16),
            jax.ShapeDtypeStruct((n, GMLP_WIDTH), BF16),
            jax.ShapeDtypeStruct((n // tt * STAT_ROWS, KV_WIDTH), F32),
        ],
        compiler_params=_compiler_params(1),
        name="proj",
    )(layer, x, p["w_in"], p["cos"], p["sin"], p["seg"], p["q_norm"], p["k_norm"],
      p["gmlp_ln_g"], p["gmlp_ln_b"], p["w_cat"], p["b_sp"], p["gmlp_out_norm"])


def _scores(k_blk, q):
    return lax.dot_general(k_blk, q, (((1,), (1,)), ((), ())), preferred_element_type=F32)


def _attn_kernel(q_ref, k_ref, vt_ref, kstat_ref, o_ref, *, tk, exact_max):
    tq = q_ref.shape[0]
    s_len = k_ref.shape[0]
    heads = q_ref.shape[1] // HEAD_DIM
    q_all = q_ref[...]
    q = jnp.concatenate([q_all[:, g * HEAD_DIM:(g + 1) * HEAD_DIM] for g in range(heads)], axis=0)
    nq = heads * tq
    if exact_max:
        def max_body(i, m):
            k_blk = k_ref[pl.ds(pl.multiple_of(i * tk, tk), tk), :]
            return jnp.maximum(m, jnp.max(_scores(k_blk, q), axis=0, keepdims=True))
        off = lax.fori_loop(0, s_len // tk, max_body, jnp.full((1, nq), -jnp.inf, F32))
    else:
        hk = pl.program_id(1)
        lane = lax.broadcasted_iota(jnp.int32, (1, LANES), 1)
        kstat = jnp.max(kstat_ref[...], axis=0, keepdims=True)
        kmax2 = jnp.max(jnp.where(lane // HEAD_DIM == hk, kstat, 0.0), axis=1, keepdims=True)
        qf = q.astype(F32)
        qq = _scores(jnp.ones((8, HEAD_DIM), BF16), (qf * qf).astype(BF16))
        off = jnp.sqrt(qq[0:1] * kmax2)

    l8 = jnp.zeros((8, nq), F32)
    acc = jnp.zeros((HEAD_DIM, nq), F32)
    for c in range(s_len // tk):
        k_blk = k_ref[c * tk:(c + 1) * tk, :]
        vt_blk = vt_ref[:, c * tk:(c + 1) * tk]
        p = jnp.exp2(_scores(k_blk, q) - off)
        l8 = l8 + jnp.sum(p.reshape(tk // 8, 8, nq), axis=0)
        acc = acc + _dot(vt_blk, p.astype(BF16))
    out = acc / jnp.sum(l8, axis=0, keepdims=True)
    out_t = jnp.concatenate([out[:, g * tq:(g + 1) * tq] for g in range(heads)], axis=0)
    o_ref[...] = out_t.T.astype(o_ref.dtype)


def _attn_call(q, k, vt, kstat, token_start, n_seq, s_len, tq, tk, tt, exact_max):
    heads_per_step = 4
    q_cols = heads_per_step * HEAD_DIM
    steps_per_kv = (N_Q_HEADS // N_KV_HEADS) // heads_per_step
    q_blk0 = token_start // tq
    seq0 = token_start // s_len
    nq = s_len // tq
    stat_rows = s_len // tt * STAT_ROWS

    def q_idx(b, hk, gp, j):
        return (q_blk0 + b * nq + j, hk * steps_per_kv + gp)

    return pl.pallas_call(
        functools.partial(_attn_kernel, tk=tk, exact_max=exact_max),
        grid=(n_seq, N_KV_HEADS, steps_per_kv, nq),
        in_specs=[
            pl.BlockSpec((tq, q_cols), q_idx),
            pl.BlockSpec((None, s_len, HEAD_DIM), lambda b, hk, gp, j: (hk, seq0 + b, 0)),
            pl.BlockSpec((HEAD_DIM, s_len), lambda b, hk, gp, j: (hk, seq0 + b)),
            pl.BlockSpec((stat_rows, KV_WIDTH), lambda b, hk, gp, j: (seq0 + b, 0)),
        ],
        out_specs=pl.BlockSpec((tq, q_cols),
                               lambda b, hk, gp, j: (b * nq + j, hk * steps_per_kv + gp)),
        out_shape=jax.ShapeDtypeStruct((n_seq * s_len, ATTN_WIDTH), BF16),
        compiler_params=_compiler_params(4),
        name="attn_exact_max" if exact_max else "attn",
    )(q, k, vt, kstat)


def _group_partner(x, row, d):
    pos = row % EXPERTS_PER_GROUP
    return jnp.where(pos + d < EXPERTS_PER_GROUP,
                     pltpu.roll(x, N_EXPERTS - d, 0),
                     pltpu.roll(x, EXPERTS_PER_GROUP - d, 0))


def _route(scores, bias, row):
    biased = scores + bias
    rank = jnp.zeros(biased.shape, jnp.int32)
    pos = row % EXPERTS_PER_GROUP
    for d in range(1, EXPERTS_PER_GROUP):
        other = _group_partner(biased, row, d)
        other_is_lower = (pos + d) >= EXPERTS_PER_GROUP
        beats = (other > biased) | ((other == biased) & other_is_lower)
        rank = rank + beats.astype(jnp.int32)
    top2 = rank < 2
    contrib = jnp.where(top2, biased, 0.0)
    group_score = contrib
    for d in range(1, EXPERTS_PER_GROUP):
        group_score = group_score + _group_partner(contrib, row, d)
    n_groups = N_EXPERTS // EXPERTS_PER_GROUP
    gpos = row // EXPERTS_PER_GROUP
    losses = jnp.zeros(biased.shape, jnp.int32)
    for d in range(1, n_groups):
        other = pltpu.roll(group_score, N_EXPERTS - d * EXPERTS_PER_GROUP, 0)
        other_is_lower = (gpos + d) >= n_groups
        beats = (other > group_score) | ((other == group_score) & other_is_lower)
        losses = losses + beats.astype(jnp.int32)
    selected = top2 & (losses == 0)
    w = jnp.where(selected, scores, 0.0)
    return w / jnp.sum(w, axis=0, keepdims=True), selected


INFO_EXPERT, INFO_RANK, INFO_GATE, INFO_ROWS = 0, 2, 4, 8


def _post_kernel(l_ref, x_ref, a_ref, gm_ref, aon_ref, wo_ref, g1_ref, b1_ref,
                 wr_ref, rb_ref, x1_ref, info_ref, cnt_ref, *, tt):
    del l_ref
    ts = x_ref.shape[0]
    a = _rms_norm(a_ref[...].astype(F32), aon_ref[...]).astype(BF16)
    mix = _dot(a, wo_ref[:ATTN_WIDTH, :]) + _dot(gm_ref[...], wo_ref[ATTN_WIDTH:, :])
    x1 = _layer_norm(DEEPNORM_ALPHA * x_ref[...] + mix, g1_ref[...], b1_ref[...])
    x1_ref[...] = x1
    x_hi = x1.astype(BF16)
    x_lo = (x1 - x_hi.astype(F32)).astype(BF16)
    by_hi = _scores(wr_ref[...], x_hi)
    logits = (by_hi[:N_EXPERTS] + by_hi[N_EXPERTS:]
              + _scores(wr_ref[:N_EXPERTS, :], x_lo))
    row = lax.broadcasted_iota(jnp.int32, (N_EXPERTS, 1), 0)
    gates, selected = _route(jax.nn.sigmoid(logits), rb_ref[...], row)

    sel = jnp.where(selected, 1.0, 0.0)
    earlier = lax.broadcasted_iota(jnp.int32, (ts, ts), 0)
    later = lax.broadcasted_iota(jnp.int32, (ts, ts), 1)
    prefix = jnp.where((earlier <= later) & (earlier // tt == later // tt), 1.0, 0.0).astype(BF16)
    incl = _dot(sel.astype(BF16), prefix)
    rank = incl - sel
    for t in range(ts // tt):
        last = (t + 1) * tt - 1
        cnt_ref[t * N_EXPERTS:(t + 1) * N_EXPERTS] = jnp.broadcast_to(
            incl[:, last:last + 1], (N_EXPERTS, LANES))

    row_f = row.astype(F32)
    e_lo = jnp.min(jnp.where(selected, row_f, float(N_EXPERTS)), axis=0, keepdims=True)
    e_hi = jnp.max(jnp.where(selected, row_f, -1.0), axis=0, keepdims=True)

    def pick(mat, e):
        return jnp.sum(jnp.where(row_f == e, mat, 0.0), axis=0, keepdims=True)

    fields = {INFO_EXPERT: e_lo, INFO_EXPERT + 1: e_hi,
              INFO_RANK: pick(rank, e_lo), INFO_RANK + 1: pick(rank, e_hi),
              INFO_GATE: pick(gates, e_lo), INFO_GATE + 1: pick(gates, e_hi)}
    info_row = lax.broadcasted_iota(jnp.int32, (INFO_ROWS, 1), 0)
    info = jnp.zeros((INFO_ROWS, ts), F32)
    for r, val in fields.items():
        info = jnp.where(info_row == r, val, info)
    info_ref[...] = info


def _post_call(layer, x, attn, gm, p, tt, tiles_per_step):
    n = x.shape[0]
    ts = tt * tiles_per_step

    def tok(i, l):
        return (i, 0)

    def lay3(i, l):
        return (l[0], 0, 0)

    def const2(i, l):
        return (0, 0)

    grid_spec = pltpu.PrefetchScalarGridSpec(
        num_scalar_prefetch=1,
        grid=(n // ts,),
        in_specs=[
            pl.BlockSpec((ts, D_MODEL), tok),
            pl.BlockSpec((ts, ATTN_WIDTH), tok),
            pl.BlockSpec((ts, GMLP_WIDTH), tok),
            pl.BlockSpec((None, 1, ATTN_WIDTH), lay3),
            pl.BlockSpec((None, D_MODEL, D_MODEL), lay3),
            pl.BlockSpec((None, 1, D_MODEL), lay3),
            pl.BlockSpec((None, 1, D_MODEL), lay3),
            pl.BlockSpec((2 * N_EXPERTS, D_MODEL), const2),
            pl.BlockSpec((N_EXPERTS, 1), const2),
        ],
        out_specs=[pl.BlockSpec((ts, D_MODEL), tok),
                   pl.BlockSpec((INFO_ROWS, ts), lambda i, l: (0, i)),
                   pl.BlockSpec((tiles_per_step * N_EXPERTS, LANES), tok)],
    )
    return pl.pallas_call(
        functools.partial(_post_kernel, tt=tt),
        grid_spec=grid_spec,
        out_shape=[jax.ShapeDtypeStruct((n, D_MODEL), F32),
                   jax.ShapeDtypeStruct((INFO_ROWS, n), F32),
                   jax.ShapeDtypeStruct((n // tt * N_EXPERTS, LANES), F32)],
        compiler_params=_compiler_params(1),
        name="post",
    )(layer, x, attn, gm, p["attn_out_norm"], p["w_out"], p["ln1_g"], p["ln1_b"],
      p["wr"], p["router_bias"])


RUN = 8


def _local_rows(tt):
    return -(-(2 * tt + N_EXPERTS * (RUN - 1)) // LANES) * LANES


def _plan_routes(info, cnt, tt, tm):
    n = info.shape[1]
    n_tiles = n // tt
    expert = info[INFO_EXPERT:INFO_EXPERT + 2].T.astype(jnp.int32)
    rank = info[INFO_RANK:INFO_RANK + 2].T.astype(jnp.int32)
    counts = cnt.reshape(n_tiles, N_EXPERTS, LANES)[:, :, 0].astype(jnp.int32)
    run_rows = (counts + RUN - 1) // RUN * RUN
    local_off = jnp.cumsum(run_rows, axis=1) - run_rows
    totals = jnp.sum(run_rows, axis=0)
    seg_tiles = (totals + tm - 1) // tm
    seg_end = jnp.cumsum(seg_tiles)
    expert_row0 = (seg_end - seg_tiles) * tm
    tile_base = expert_row0[None, :] + jnp.cumsum(run_rows, axis=0) - run_rows
    is_expert = expert.reshape(n_tiles, tt, 2, 1) == jnp.arange(N_EXPERTS, dtype=jnp.int32)

    def per_slot(table):
        return jnp.sum(jnp.where(is_expert, table[:, None, None, :], 0), axis=-1)

    rank3 = rank.reshape(n_tiles, tt, 2)
    pos = per_slot(tile_base) + rank3
    lpos = per_slot(local_off) + rank3
    n_row_tiles = (2 * n + n_tiles * N_EXPERTS * (RUN - 1)) // tm + N_EXPERTS + 1
    used = seg_end[-1]
    t_idx = jnp.arange(n_row_tiles, dtype=jnp.int32)
    tile_blk = jnp.minimum(t_idx, used - 1)
    tile_expert = jnp.minimum(
        jnp.sum((tile_blk[:, None] >= seg_end[None, :]).astype(jnp.int32), axis=1), N_EXPERTS - 1)
    return {
        "pos": pos.reshape(n_tiles, 1, 2 * tt).astype(jnp.int32),
        "lpos": lpos.transpose(0, 2, 1).astype(jnp.int32),
        "runs": jnp.stack([local_off, tile_base, run_rows // RUN], axis=1).astype(jnp.int32),
        "gates": info[INFO_GATE:INFO_GATE + 2].T,
        "pads": jnp.stack([jnp.append(expert_row0 + totals, used),
                           jnp.append((seg_tiles * tm - totals) // RUN,
                                      n_row_tiles - used)]).astype(jnp.int32),
        "tile_expert": tile_expert.astype(jnp.int32),
        "tile_valid": (t_idx < used).astype(jnp.int32),
        "tile_blk": tile_blk.astype(jnp.int32),
        "n_rows": n_row_tiles * tm,
    }


ROW_SLAB = 8
assert ROW_SLAB * LANES == D_MODEL


def _slab_copy(src_ref, src_row, dst_ref, dst_row, sem):
    src = src_ref.at[pl.ds(pl.multiple_of(src_row * ROW_SLAB, ROW_SLAB), ROW_SLAB)]
    dst = dst_ref.at[pl.ds(pl.multiple_of(dst_row * ROW_SLAB, ROW_SLAB), ROW_SLAB)]
    return pltpu.make_async_copy(src, dst, sem)


def _rows_to_slabs(x, slab_ref):
    rows = x.shape[0]
    for s in range(ROW_SLAB):
        slab_ref[pl.ds(s, rows, stride=ROW_SLAB), :] = x[:, s * LANES:(s + 1) * LANES]


def _slabs_to_rows(slab_ref, rows):
    return jnp.concatenate(
        [slab_ref[pl.ds(s, rows, stride=ROW_SLAB), :] for s in range(ROW_SLAB)], axis=1)


DMA_UNROLL = 8


PACKED_WIDTH = D_MODEL // 2
HIGH_HALF = 0xFFFF0000


def _pack_bf16_pairs(x):
    lo = pltpu.bitcast(x[:, :PACKED_WIDTH], jnp.uint32)
    hi = pltpu.bitcast(x[:, PACKED_WIDTH:], jnp.uint32)
    return (lo >> 16) | (hi & jnp.uint32(HIGH_HALF))


def _unpack_bf16_pairs(u):
    lo = pltpu.bitcast(u << 16, F32)
    hi = pltpu.bitcast(u & jnp.uint32(HIGH_HALF), F32)
    return jnp.concatenate([lo, hi], axis=1).astype(BF16)


def _scatter_kernel(pads_ref, runs_ref, lpos_ref, x_ref, xs_ref, loc_ref, sems, *, tm):
    i = pl.program_id(0)
    last = pl.num_programs(0) - 1
    slot = i % 2
    n_local = loc_ref.shape[1]

    def unit_copy(src_ref, src_row, dst_row, sem):
        return pltpu.make_async_copy(
            src_ref.at[pl.ds(pl.multiple_of(src_row, RUN), RUN)],
            xs_ref.at[pl.ds(pl.multiple_of(dst_row, RUN), RUN)], sem)

    @pl.when(i == 0)
    def _():
        zeros = loc_ref.at[1]
        zeros[...] = jnp.zeros(zeros.shape, zeros.dtype)
        for e in range(N_EXPERTS):
            start, units = pads_ref[0, e], pads_ref[1, e]

            def fill(j, c, start=start):
                unit_copy(zeros, 0, start + j * RUN, sems.at[2]).start()
                return c

            def fill_wait(j, c):
                unit_copy(zeros, 0, 0, sems.at[2]).wait()
                return c

            lax.fori_loop(0, units, fill, 0)
            lax.fori_loop(0, units, fill_wait, 0)

        first_tail_tile, n_tail_tiles = pads_ref[0, N_EXPERTS], pads_ref[1, N_EXPERTS]

        def tail_copy(j):
            row0 = pl.multiple_of((first_tail_tile + j) * tm, tm)
            return pltpu.make_async_copy(zeros.at[pl.ds(0, tm)], xs_ref.at[pl.ds(row0, tm)],
                                         sems.at[2])

        def tail_fill(j, c):
            tail_copy(j).start()
            return c

        def tail_wait(j, c):
            tail_copy(j).wait()
            return c

        lax.fori_loop(0, n_tail_tiles, tail_fill, 0)
        lax.fori_loop(0, n_tail_tiles, tail_wait, 0)

    lpos = lpos_ref[...]
    row = lax.broadcasted_iota(jnp.int32, (n_local, 1), 0)
    onehot = jnp.where((row == lpos[0:1, :]) | (row == lpos[1:2, :]), 1.0, 0.0).astype(BF16)
    sorted_rows = _dot(onehot, x_ref[...].astype(BF16))
    loc_ref[slot] = _pack_bf16_pairs(sorted_rows)

    for e in range(N_EXPERTS):
        src0, dst0, units = runs_ref[i, 0, e], runs_ref[i, 1, e], runs_ref[i, 2, e]

        def issue(k, c, src0=src0, dst0=dst0):
            unit_copy(loc_ref.at[slot], src0 + k * RUN, dst0 + k * RUN, sems.at[slot]).start()
            return c

        lax.fori_loop(0, units, issue, 0)

    def drain(which, tile):
        n_units = runs_ref[tile, 2, 0]
        for e in range(1, N_EXPERTS):
            n_units = n_units + runs_ref[tile, 2, e]

        def body(k, c):
            unit_copy(loc_ref.at[which], 0, 0, sems.at[which]).wait()
            return c

        lax.fori_loop(0, n_units, body, 0)

    @pl.when(i > 0)
    def _():
        drain(1 - slot, i - 1)

    @pl.when(i == last)
    def _():
        drain(slot, i)


def _scatter_call(x1, plan, tt, tm):
    n = x1.shape[0]
    grid_spec = pltpu.PrefetchScalarGridSpec(
        num_scalar_prefetch=2,
        grid=(n // tt,),
        in_specs=[
            pl.BlockSpec((None, 2, tt), lambda i, pads, runs: (i, 0, 0)),
            pl.BlockSpec((tt, D_MODEL), lambda i, pads, runs: (i, 0)),
        ],
        out_specs=pl.BlockSpec(memory_space=pl.ANY),
        scratch_shapes=[pltpu.VMEM((2, max(_local_rows(tt), tm), PACKED_WIDTH), jnp.uint32),
                        pltpu.SemaphoreType.DMA((3,))],
    )
    return pl.pallas_call(
        functools.partial(_scatter_kernel, tm=tm),
        grid_spec=grid_spec,
        out_shape=jax.ShapeDtypeStruct((plan["n_rows"], PACKED_WIDTH), jnp.uint32),
        compiler_params=_compiler_params(1),
        name="scatter",
    )(plan["pads"], plan["runs"], plan["lpos"], x1)


def _expert_kernel(l_ref, te_ref, tv_ref, tb_ref, xs_ref, wg_ref, wu_ref, wd_ref, o_ref):
    del l_ref, te_ref, tb_ref
    valid = tv_ref[pl.program_id(0)] != 0

    @pl.when(valid)
    def _():
        xb = _unpack_bf16_pairs(xs_ref[...])
        hidden = jax.nn.silu(_dot(xb, wg_ref[...])) * _dot(xb, wu_ref[...])
        _rows_to_slabs(_dot(hidden.astype(BF16), wd_ref[...]), o_ref)

    @pl.when(jnp.logical_not(valid))
    def _():
        o_ref[...] = jnp.zeros_like(o_ref)


def _expert_call(layer, xs, plan, p, tm):
    n_rows = xs.shape[0]

    def rows(t, l, te, tv, tb):
        return (tb[t], 0)

    def weights(t, l, te, tv, tb):
        return (l[0], te[t], 0, 0)

    grid_spec = pltpu.PrefetchScalarGridSpec(
        num_scalar_prefetch=4,
        grid=(n_rows // tm,),
        in_specs=[
            pl.BlockSpec((tm, PACKED_WIDTH), rows),
            pl.BlockSpec((None, None, D_MODEL, EXPERT_FF), weights),
            pl.BlockSpec((None, None, D_MODEL, EXPERT_FF), weights),
            pl.BlockSpec((None, None, EXPERT_FF, D_MODEL), weights),
        ],
        out_specs=pl.BlockSpec((tm * ROW_SLAB, LANES), lambda t, l, te, tv, tb: (t, 0)),
    )
    return pl.pallas_call(
        _expert_kernel,
        grid_spec=grid_spec,
        out_shape=jax.ShapeDtypeStruct((n_rows * ROW_SLAB, LANES), F32),
        compiler_params=_compiler_params(1),
        name="experts",
    )(layer, plan["tile_expert"], plan["tile_valid"], plan["tile_blk"], xs,
      p["w_gate"], p["w_up"], p["w_down"])


def _combine_kernel(l_ref, pos_ref, pos_next_ref, x_ref, gate_ref, os_ref, g2_ref, b2_ref,
                    o_ref, buf_ref, sems):
    del l_ref
    tt = x_ref.shape[0]
    i = pl.program_id(0)
    slot = i % 2

    def gather(p_ref, which):
        def body(r, c):
            for s in range(2):
                _slab_copy(os_ref, p_ref[0, 2 * r + s], buf_ref.at[which, s], r,
                           sems.at[which]).start(priority=s)
            return c
        lax.fori_loop(0, tt, body, 0, unroll=DMA_UNROLL)

    @pl.when(i == 0)
    def _():
        gather(pos_ref, 0)

    @pl.when(i + 1 < pl.num_programs(0))
    def _():
        gather(pos_next_ref, 1 - slot)

    def drain(r, c):
        for s in range(2):
            _slab_copy(os_ref, 0, buf_ref.at[slot, s], 0, sems.at[slot]).wait()
        return c

    lax.fori_loop(0, tt, drain, 0, unroll=DMA_UNROLL)
    gate = gate_ref[...]
    y = (gate[:, 0:1] * _slabs_to_rows(buf_ref.at[slot, 0], tt)
         + gate[:, 1:2] * _slabs_to_rows(buf_ref.at[slot, 1], tt))
    o_ref[...] = _layer_norm(DEEPNORM_ALPHA * x_ref[...] + y, g2_ref[...], b2_ref[...])


def _combine_call(layer, x1, out_sorted, plan, p, tt):
    n = x1.shape[0]

    def tok(i, l):
        return (i, 0)

    def lay3(i, l):
        return (l[0], 0, 0)

    n_tiles = n // tt
    grid_spec = pltpu.PrefetchScalarGridSpec(
        num_scalar_prefetch=1,
        grid=(n_tiles,),
        in_specs=[
            pl.BlockSpec((None, 1, 2 * tt), lambda i, l: (i, 0, 0), memory_space=pltpu.SMEM),
            pl.BlockSpec((None, 1, 2 * tt), lambda i, l: (jnp.minimum(i + 1, n_tiles - 1), 0, 0),
                         memory_space=pltpu.SMEM),
            pl.BlockSpec((tt, D_MODEL), tok),
            pl.BlockSpec((tt, 2), tok),
            pl.BlockSpec(memory_space=pl.ANY),
            pl.BlockSpec((None, 1, D_MODEL), lay3),
            pl.BlockSpec((None, 1, D_MODEL), lay3),
        ],
        out_specs=pl.BlockSpec((tt, D_MODEL), tok),
        scratch_shapes=[pltpu.VMEM((2, 2, tt * ROW_SLAB, LANES), F32),
                        pltpu.SemaphoreType.DMA((2,))],
    )
    return pl.pallas_call(
        _combine_kernel,
        grid_spec=grid_spec,
        out_shape=jax.ShapeDtypeStruct((n, D_MODEL), F32),
        compiler_params=_compiler_params(1),
        name="combine",
    )(layer, plan["pos"], plan["pos"], x1, plan["gates"], out_sorted, p["ln2_g"], p["ln2_b"])


def _rope_tables(max_len):
    t = jnp.arange(max_len, dtype=jnp.int32)
    row = (t // GRID_W).astype(F32)
    col = (t % GRID_W).astype(F32)
    n_pairs_axis = HEAD_DIM // 4
    inv_freq = 1.0 / (ROPE_THETA ** (jnp.arange(n_pairs_axis, dtype=F32) / n_pairs_axis))
    ang = jnp.concatenate([row[:, None] * inv_freq, col[:, None] * inv_freq], -1)
    cos, sin = jnp.cos(ang), jnp.sin(ang)
    reps = LANES // HEAD_DIM
    cos128 = jnp.tile(jnp.concatenate([cos, cos], -1), (1, reps))
    sin128 = jnp.tile(jnp.concatenate([-sin, sin], -1), (1, reps))
    return cos128, sin128


def _prepare_params(max_len, w_in, q_norm, k_norm, gmlp_ln_g, gmlp_ln_b, w_spatial, b_spatial,
                    attn_out_norm, gmlp_out_norm, w_out, ln1_g, ln1_b, w_router, router_bias,
                    w_gate, w_up, w_down, ln2_g, ln2_b):
    depth = w_in.shape[0]
    perm64 = jnp.concatenate([jnp.arange(0, HEAD_DIM, 2), jnp.arange(1, HEAD_DIM, 2)])
    n_rot_heads = N_Q_HEADS + N_KV_HEADS
    rot_cols = (jnp.arange(n_rot_heads)[:, None] * HEAD_DIM + perm64[None, :]).reshape(-1)
    cols = jnp.concatenate([rot_cols, jnp.arange(n_rot_heads * HEAD_DIM, IN_WIDTH)])
    cos128, sin128 = _rope_tables(max_len)
    head_of_lane = jnp.arange(ATTN_WIDTH) // HEAD_DIM
    row3 = lambda a: a.reshape(depth, 1, -1).astype(F32)
    wr_t = w_router.astype(F32).T
    wr_hi = wr_t.astype(BF16)
    max_offset = (1.01 * LOG2_E * HEAD_DIM ** 0.5
                  * jnp.max(jnp.abs(q_norm), axis=1) * jnp.max(jnp.abs(k_norm), axis=1))
    return {
        "attn_needs_exact_max": max_offset > MAX_SAFE_OFFSET,
        "w_in": w_in[:, :, cols].astype(BF16),
        "cos": cos128,
        "sin": sin128,
        "seg": (head_of_lane[:, None] == head_of_lane[None, :]).astype(BF16),
        "q_norm": row3(jnp.tile(q_norm[:, perm64], (1, N_Q_HEADS))),
        "k_norm": row3(jnp.tile(k_norm[:, perm64], (1, N_KV_HEADS))),
        "gmlp_ln_g": row3(gmlp_ln_g),
        "gmlp_ln_b": row3(gmlp_ln_b),
        "w_cat": w_spatial.transpose(0, 2, 1, 3).reshape(depth, CHUNK, N_GMLP_HEADS * CHUNK).astype(BF16),
        "b_sp": jnp.repeat(b_spatial.transpose(0, 2, 1), GMLP_HEAD_DIM, axis=2).astype(F32),
        "attn_out_norm": row3(attn_out_norm),
        "gmlp_out_norm": row3(gmlp_out_norm),
        "w_out": w_out.astype(BF16),
        "ln1_g": row3(ln1_g),
        "ln1_b": row3(ln1_b),
        "wr": jnp.concatenate([wr_hi, (wr_t - wr_hi.astype(F32)).astype(BF16)], axis=0),
        "router_bias": router_bias.astype(F32).reshape(N_EXPERTS, 1),
        "w_gate": w_gate.astype(BF16),
        "w_up": w_up.astype(BF16),
        "w_down": w_down.astype(BF16),
        "ln2_g": row3(ln2_g),
        "ln2_b": row3(ln2_b),
    }


def _trunk(x_tokens, groups, p, depth, tp, tt, tm, tq, tk):
    (_, n_seq0, s0), = groups[:1]
    n_prompt_blocks = (n_seq0 * s0) // tp
    prompt_period = s0 // tp
    sample_period = (groups[1][2] if len(groups) > 1 else s0) // tp
    periods = (n_prompt_blocks, prompt_period, sample_period)

    def layer_body(l, x):
        layer = jnp.reshape(l, (1,)).astype(jnp.int32)
        q, k, vt, gm, kstat = _proj_call(layer, x, p, tp, periods)

        def attention(exact_max):
            outs = [_attn_call(q, k, vt, kstat, start, n_seq, s_len, min(tq, s_len),
                               min(tk, s_len), tp, exact_max)
                    for (start, n_seq, s_len) in groups]
            return outs[0] if len(outs) == 1 else jnp.concatenate(outs, axis=0)

        attn = lax.cond(p["attn_needs_exact_max"][l],
                        functools.partial(attention, True), functools.partial(attention, False))
        x1, info, cnt = _post_call(layer, x, attn, gm, p, tt, tiles_per_step=2)
        plan = _plan_routes(info, cnt, tt, tm)
        xs = _scatter_call(x1, plan, tt, tm)
        out_sorted = _expert_call(layer, xs, plan, p, tm)
        return _combine_call(layer, x1, out_sorted, plan, p, tt)

    return lax.fori_loop(0, depth, layer_body, x_tokens)


def kernel(x_prompt, x_sample, w_in, q_norm, k_norm, gmlp_ln_g, gmlp_ln_b, w_spatial, b_spatial,
           attn_out_norm, gmlp_out_norm, w_out, ln1_g, ln1_b, w_router, router_bias,
           w_gate, w_up, w_down, ln2_g, ln2_b):
    bp, sp, d = x_prompt.shape
    bs, ss, _ = x_sample.shape
    assert sp % ss == 0 and (bp * sp) % ss == 0, "sample sequences must tile the prompt stream"
    p = _prepare_params(max(sp, ss), w_in, q_norm, k_norm, gmlp_ln_g, gmlp_ln_b, w_spatial,
                        b_spatial, attn_out_norm, gmlp_out_norm, w_out, ln1_g, ln1_b, w_router,
                        router_bias, w_gate, w_up, w_down, ln2_g, ln2_b)
    x = jnp.concatenate([x_prompt.reshape(bp * sp, d), x_sample.reshape(bs * ss, d)], axis=0)
    groups = ((0, bp, sp), (bp * sp, bs, ss))
    y = _trunk(x.astype(F32), groups, p, w_in.shape[0], tp=min(512, ss), tt=256, tm=512,
               tq=256, tk=512)
    return (y[:bp * sp].reshape(bp, sp, d), y[bp * sp:].reshape(bs, ss, d))
```

```python
import functools

import jax
import jax.numpy as jnp
from jax import lax
from jax.experimental import pallas as pl
from jax.experimental.pallas import tpu as pltpu

D_MODEL = 1024
DEPTH = 4
HEAD_DIM = 64
N_Q_HEADS = 8
N_KV_HEADS = 2
ATTN_WIDTH = N_Q_HEADS * HEAD_DIM
KV_WIDTH = N_KV_HEADS * HEAD_DIM
GMLP_WIDTH = 512
N_GMLP_HEADS = 8
GMLP_HEAD_DIM = 64
CHUNK = 128
IN_WIDTH = ATTN_WIDTH + 2 * KV_WIDTH + 2 * GMLP_WIDTH
N_EXPERTS = 16
EXPERTS_PER_GROUP = 4
EXPERT_FF = 512
ROPE_THETA = 10000.0
GRID_W = 64
EPS = 1e-6
DEEPNORM_ALPHA = float((2 * DEPTH) ** 0.25)
LOG2_E = 1.4426950408889634
STAT_ROWS = 8
MAX_SAFE_OFFSET = 60.0

LANES = 128
F32 = jnp.float32
BF16 = jnp.bfloat16
VMEM_LIMIT_BYTES = 48 * 1024 * 1024


def _compiler_params(n_grid_dims):
    return pltpu.CompilerParams(
        dimension_semantics=("arbitrary",) * n_grid_dims,
        vmem_limit_bytes=VMEM_LIMIT_BYTES,
    )


def _dot(a, b):
    return jnp.dot(a, b, preferred_element_type=F32)


def _layer_norm(x, g, b):
    mu = jnp.mean(x, axis=-1, keepdims=True)
    xc = x - mu
    var = jnp.mean(xc * xc, axis=-1, keepdims=True)
    return xc * lax.rsqrt(var + EPS) * g + b


def _rms_norm(x, g):
    return x * lax.rsqrt(jnp.mean(x * x, axis=-1, keepdims=True) + EPS) * g


def _head_rms_norm(x, seg, g):
    ssq = _dot((x * x).astype(BF16), seg)
    return x * lax.rsqrt(ssq * (1.0 / HEAD_DIM) + EPS) * g


def _rope_half_split(x, cos, sin_signed):
    first_half = (lax.broadcasted_iota(jnp.int32, (1, LANES), 1) % HEAD_DIM) < (HEAD_DIM // 2)
    cols = []
    for c in range(x.shape[1] // LANES):
        xc = x[:, c * LANES:(c + 1) * LANES]
        partner = jnp.where(first_half,
                            pltpu.roll(xc, LANES - HEAD_DIM // 2, 1),
                            pltpu.roll(xc, HEAD_DIM // 2, 1))
        cols.append(xc * cos + partner * sin_signed)
    return cols[0] if len(cols) == 1 else jnp.concatenate(cols, axis=1)


def _proj_kernel(l_ref, x_ref, w_in_ref, cos_ref, sin_ref, seg_ref, qn_ref, kn_ref,
                 lng_ref, lnb_ref, wcat_ref, bsp_ref, gon_ref,
                 q_ref, k_ref, vt_ref, gm_ref, kstat_ref):
    del l_ref
    tt = x_ref.shape[0]
    h = _dot(x_ref[...].astype(BF16), w_in_ref[...])
    cos = cos_ref[...]
    sin_signed = sin_ref[...]

    q = _head_rms_norm(h[:, :ATTN_WIDTH], seg_ref[...], qn_ref[...])
    q = _rope_half_split(q, cos, sin_signed) * (LOG2_E * HEAD_DIM ** -0.5)
    q_ref[...] = q.astype(BF16)

    k0 = ATTN_WIDTH
    seg_kv = seg_ref[:KV_WIDTH, :KV_WIDTH]
    k = _head_rms_norm(h[:, k0:k0 + KV_WIDTH], seg_kv, kn_ref[...])
    k = _rope_half_split(k, cos, sin_signed).astype(BF16)
    kf = k.astype(F32)
    k_sq = _dot((kf * kf).astype(BF16), seg_kv)
    kstat_ref[...] = jnp.broadcast_to(jnp.max(k_sq, axis=0, keepdims=True), kstat_ref.shape)
    for hk in range(N_KV_HEADS):
        k_ref[hk] = k[:, hk * HEAD_DIM:(hk + 1) * HEAD_DIM]

    v0 = k0 + KV_WIDTH
    vt_ref[...] = h[:, v0:v0 + KV_WIDTH].T.astype(BF16)

    u0 = v0 + KV_WIDTH
    u = jax.nn.gelu(h[:, u0:u0 + GMLP_WIDTH])
    vg = _layer_norm(jax.nn.gelu(h[:, u0 + GMLP_WIDTH:]), lng_ref[...], lnb_ref[...])
    lane_head = lax.broadcasted_iota(jnp.int32, (1, GMLP_WIDTH), 1) // GMLP_HEAD_DIM
    mixed = []
    for c in range(tt // CHUNK):
        vc = vg[c * CHUNK:(c + 1) * CHUNK].astype(BF16)
        stack = jnp.concatenate(
            [jnp.where(lane_head == hh, vc, jnp.zeros_like(vc)) for hh in range(N_GMLP_HEADS)],
            axis=0)
        mixed.append(_dot(wcat_ref[...], stack) + bsp_ref[...])
    mixed = mixed[0] if len(mixed) == 1 else jnp.concatenate(mixed, axis=0)
    gm_ref[...] = _rms_norm(u * mixed, gon_ref[...]).astype(BF16)


def _proj_call(layer, x, p, tt, rope_period_blocks):
    n = x.shape[0]
    n_prompt_blocks, prompt_period, sample_period = rope_period_blocks

    def tok(i, l):
        return (i, 0)

    def rope_idx(i, l):
        return (jnp.where(i < n_prompt_blocks, i % prompt_period, i % sample_period), 0)

    def lay3(i, l):
        return (l[0], 0, 0)

    def const2(i, l):
        return (0, 0)

    grid_spec = pltpu.PrefetchScalarGridSpec(
        num_scalar_prefetch=1,
        grid=(n // tt,),
        in_specs=[
            pl.BlockSpec((tt, D_MODEL), tok),
            pl.BlockSpec((None, D_MODEL, IN_WIDTH), lay3),
            pl.BlockSpec((tt, LANES), rope_idx),
            pl.BlockSpec((tt, LANES), rope_idx),
            pl.BlockSpec((ATTN_WIDTH, ATTN_WIDTH), const2),
            pl.BlockSpec((None, 1, ATTN_WIDTH), lay3),
            pl.BlockSpec((None, 1, KV_WIDTH), lay3),
            pl.BlockSpec((None, 1, GMLP_WIDTH), lay3),
            pl.BlockSpec((None, 1, GMLP_WIDTH), lay3),
            pl.BlockSpec((None, CHUNK, N_GMLP_HEADS * CHUNK), lay3),
            pl.BlockSpec((None, CHUNK, GMLP_WIDTH), lay3),
            pl.BlockSpec((None, 1, GMLP_WIDTH), lay3),
        ],
        out_specs=[
            pl.BlockSpec((tt, ATTN_WIDTH), tok),
            pl.BlockSpec((N_KV_HEADS, tt, HEAD_DIM), lambda i, l: (0, i, 0)),
            pl.BlockSpec((KV_WIDTH, tt), lambda i, l: (0, i)),
            pl.BlockSpec((tt, GMLP_WIDTH), tok),
            pl.BlockSpec((STAT_ROWS, KV_WIDTH), tok),
        ],
    )
    return pl.pallas_call(
        _proj_kernel,
        grid_spec=grid_spec,
        out_shape=[
            jax.ShapeDtypeStruct((n, ATTN_WIDTH), BF16),
            jax.ShapeDtypeStruct((N_KV_HEADS, n, HEAD_DIM), BF16),
            jax.ShapeDtypeStruct((KV_WIDTH, n), BF16),
            jax.ShapeDtypeStruct((n, GMLP_WIDTH), BF16),
            jax.ShapeDtypeStruct((n // tt * STAT_ROWS, KV_WIDTH), F32),
        ],
        compiler_params=_compiler_params(1),
        name="proj",
    )(layer, x, p["w_in"], p["cos"], p["sin"], p["seg"], p["q_norm"], p["k_norm"],
      p["gmlp_ln_g"], p["gmlp_ln_b"], p["w_cat"], p["b_sp"], p["gmlp_out_norm"])


def _scores(k_blk, q):
    return lax.dot_general(k_blk, q, (((1,), (1,)), ((), ())), preferred_element_type=F32)


def _attn_kernel(q_ref, k_ref, vt_ref, kstat_ref, o_ref, *, tk, exact_max):
    tq = q_ref.shape[0]
    s_len = k_ref.shape[0]
    heads = q_ref.shape[1] // HEAD_DIM
    q_all = q_ref[...]
    q = jnp.concatenate([q_all[:, g * HEAD_DIM:(g + 1) * HEAD_DIM] for g in range(heads)], axis=0)
    nq = heads * tq
    if exact_max:
        def max_body(i, m):
            k_blk = k_ref[pl.ds(pl.multiple_of(i * tk, tk), tk), :]
            return jnp.maximum(m, jnp.max(_scores(k_blk, q), axis=0, keepdims=True))
        off = lax.fori_loop(0, s_len // tk, max_body, jnp.full((1, nq), -jnp.inf, F32))
    else:
        hk = pl.program_id(1)
        lane = lax.broadcasted_iota(jnp.int32, (1, LANES), 1)
        kstat = jnp.max(kstat_ref[...], axis=0, keepdims=True)
        kmax2 = jnp.max(jnp.where(lane // HEAD_DIM == hk, kstat, 0.0), axis=1, keepdims=True)
        qf = q.astype(F32)
        qq = _scores(jnp.ones((8, HEAD_DIM), BF16), (qf * qf).astype(BF16))
        off = jnp.sqrt(qq[0:1] * kmax2)

    l8 = jnp.zeros((8, nq), F32)
    acc = jnp.zeros((HEAD_DIM, nq), F32)
    for c in range(s_len // tk):
        k_blk = k_ref[c * tk:(c + 1) * tk, :]
        vt_blk = vt_ref[:, c * tk:(c + 1) * tk]
        p = jnp.exp2(_scores(k_blk, q) - off)
        l8 = l8 + jnp.sum(p.reshape(tk // 8, 8, nq), axis=0)
        acc = acc + _dot(vt_blk, p.astype(BF16))
    out = acc / jnp.sum(l8, axis=0, keepdims=True)
    out_t = jnp.concatenate([out[:, g * tq:(g + 1) * tq] for g in range(heads)], axis=0)
    o_ref[...] = out_t.T.astype(o_ref.dtype)


def _attn_call(q, k, vt, kstat, token_start, n_seq, s_len, tq, tk, tt, exact_max):
    heads_per_step = 4
    q_cols = heads_per_step * HEAD_DIM
    steps_per_kv = (N_Q_HEADS // N_KV_HEADS) // heads_per_step
    q_blk0 = token_start // tq
    seq0 = token_start // s_len
    nq = s_len // tq
    stat_rows = s_len // tt * STAT_ROWS

    def q_idx(b, hk, gp, j):
        return (q_blk0 + b * nq + j, hk * steps_per_kv + gp)

    return pl.pallas_call(
        functools.partial(_attn_kernel, tk=tk, exact_max=exact_max),
        grid=(n_seq, N_KV_HEADS, steps_per_kv, nq),
        in_specs=[
            pl.BlockSpec((tq, q_cols), q_idx),
            pl.BlockSpec((None, s_len, HEAD_DIM), lambda b, hk, gp, j: (hk, seq0 + b, 0)),
            pl.BlockSpec((HEAD_DIM, s_len), lambda b, hk, gp, j: (hk, seq0 + b)),
            pl.BlockSpec((stat_rows, KV_WIDTH), lambda b, hk, gp, j: (seq0 + b, 0)),
        ],
        out_specs=pl.BlockSpec((tq, q_cols),
                               lambda b, hk, gp, j: (b * nq + j, hk * steps_per_kv + gp)),
        out_shape=jax.ShapeDtypeStruct((n_seq * s_len, ATTN_WIDTH), BF16),
        compiler_params=_compiler_params(4),
        name="attn_exact_max" if exact_max else "attn",
    )(q, k, vt, kstat)


def _group_partner(x, row, d):
    pos = row % EXPERTS_PER_GROUP
    return jnp.where(pos + d < EXPERTS_PER_GROUP,
                     pltpu.roll(x, N_EXPERTS - d, 0),
                     pltpu.roll(x, EXPERTS_PER_GROUP - d, 0))


def _route(scores, bias, row):
    biased = scores + bias
    rank = jnp.zeros(biased.shape, jnp.int32)
    pos = row % EXPERTS_PER_GROUP
    for d in range(1, EXPERTS_PER_GROUP):
        other = _group_partner(biased, row, d)
        other_is_lower = (pos + d) >= EXPERTS_PER_GROUP
        beats = (other > biased) | ((other == biased) & other_is_lower)
        rank = rank + beats.astype(jnp.int32)
    top2 = rank < 2
    contrib = jnp.where(top2, biased, 0.0)
    group_score = contrib
    for d in range(1, EXPERTS_PER_GROUP):
        group_score = group_score + _group_partner(contrib, row, d)
    n_groups = N_EXPERTS // EXPERTS_PER_GROUP
    gpos = row // EXPERTS_PER_GROUP
    losses = jnp.zeros(biased.shape, jnp.int32)
    for d in range(1, n_groups):
        other = pltpu.roll(group_score, N_EXPERTS - d * EXPERTS_PER_GROUP, 0)
        other_is_lower = (gpos + d) >= n_groups
        beats = (other > group_score) | ((other == group_score) & other_is_lower)
        losses = losses + beats.astype(jnp.int32)
    selected = top2 & (losses == 0)
    w = jnp.where(selected, scores, 0.0)
    return w / jnp.sum(w, axis=0, keepdims=True), selected


INFO_EXPERT, INFO_RANK, INFO_GATE, INFO_ROWS = 0, 2, 4, 8


def _post_kernel(l_ref, x_ref, a_ref, gm_ref, aon_ref, wo_ref, g1_ref, b1_ref,
                 wr_ref, rb_ref, x1_ref, info_ref, cnt_ref, *, tt):
    del l_ref
    ts = x_ref.shape[0]
    a = _rms_norm(a_ref[...].astype(F32), aon_ref[...]).astype(BF16)
    mix = _dot(a, wo_ref[:ATTN_WIDTH, :]) + _dot(gm_ref[...], wo_ref[ATTN_WIDTH:, :])
    x1 = _layer_norm(DEEPNORM_ALPHA * x_ref[...] + mix, g1_ref[...], b1_ref[...])
    x1_ref[...] = x1
    x_hi = x1.astype(BF16)
    x_lo = (x1 - x_hi.astype(F32)).astype(BF16)
    by_hi = _scores(wr_ref[...], x_hi)
    logits = (by_hi[:N_EXPERTS] + by_hi[N_EXPERTS:]
              + _scores(wr_ref[:N_EXPERTS, :], x_lo))
    row = lax.broadcasted_iota(jnp.int32, (N_EXPERTS, 1), 0)
    gates, selected = _route(jax.nn.sigmoid(logits), rb_ref[...], row)

    sel = jnp.where(selected, 1.0, 0.0)
    earlier = lax.broadcasted_iota(jnp.int32, (ts, ts), 0)
    later = lax.broadcasted_iota(jnp.int32, (ts, ts), 1)
    prefix = jnp.where((earlier <= later) & (earlier // tt == later // tt), 1.0, 0.0).astype(BF16)
    incl = _dot(sel.astype(BF16), prefix)
    rank = incl - sel
    for t in range(ts // tt):
        last = (t + 1) * tt - 1
        cnt_ref[t * N_EXPERTS:(t + 1) * N_EXPERTS] = jnp.broadcast_to(
            incl[:, last:last + 1], (N_EXPERTS, LANES))

    row_f = row.astype(F32)
    e_lo = jnp.min(jnp.where(selected, row_f, float(N_EXPERTS)), axis=0, keepdims=True)
    e_hi = jnp.max(jnp.where(selected, row_f, -1.0), axis=0, keepdims=True)

    def pick(mat, e):
        return jnp.sum(jnp.where(row_f == e, mat, 0.0), axis=0, keepdims=True)

    fields = {INFO_EXPERT: e_lo, INFO_EXPERT + 1: e_hi,
              INFO_RANK: pick(rank, e_lo), INFO_RANK + 1: pick(rank, e_hi),
              INFO_GATE: pick(gates, e_lo), INFO_GATE + 1: pick(gates, e_hi)}
    info_row = lax.broadcasted_iota(jnp.int32, (INFO_ROWS, 1), 0)
    info = jnp.zeros((INFO_ROWS, ts), F32)
    for r, val in fields.items():
        info = jnp.where(info_row == r, val, info)
    info_ref[...] = info


def _post_call(layer, x, attn, gm, p, tt, tiles_per_step):
    n = x.shape[0]
    ts = tt * tiles_per_step

    def tok(i, l):
        return (i, 0)

    def lay3(i, l):
        return (l[0], 0, 0)

    def const2(i, l):
        return (0, 0)

    grid_spec = pltpu.PrefetchScalarGridSpec(
        num_scalar_prefetch=1,
        grid=(n // ts,),
        in_specs=[
            pl.BlockSpec((ts, D_MODEL), tok),
            pl.BlockSpec((ts, ATTN_WIDTH), tok),
            pl.BlockSpec((ts, GMLP_WIDTH), tok),
            pl.BlockSpec((None, 1, ATTN_WIDTH), lay3),
            pl.BlockSpec((None, D_MODEL, D_MODEL), lay3),
            pl.BlockSpec((None, 1, D_MODEL), lay3),
            pl.BlockSpec((None, 1, D_MODEL), lay3),
            pl.BlockSpec((2 * N_EXPERTS, D_MODEL), const2),
            pl.BlockSpec((N_EXPERTS, 1), const2),
        ],
        out_specs=[pl.BlockSpec((ts, D_MODEL), tok),
                   pl.BlockSpec((INFO_ROWS, ts), lambda i, l: (0, i)),
                   pl.BlockSpec((tiles_per_step * N_EXPERTS, LANES), tok)],
    )
    return pl.pallas_call(
        functools.partial(_post_kernel, tt=tt),
        grid_spec=grid_spec,
        out_shape=[jax.ShapeDtypeStruct((n, D_MODEL), F32),
                   jax.ShapeDtypeStruct((INFO_ROWS, n), F32),
                   jax.ShapeDtypeStruct((n // tt * N_EXPERTS, LANES), F32)],
        compiler_params=_compiler_params(1),
        name="post",
    )(layer, x, attn, gm, p["attn_out_norm"], p["w_out"], p["ln1_g"], p["ln1_b"],
      p["wr"], p["router_bias"])


RUN = 8


def _local_rows(tt):
    return -(-(2 * tt + N_EXPERTS * (RUN - 1)) // LANES) * LANES


def _plan_routes(info, cnt, tt, tm):
    n = info.shape[1]
    n_tiles = n // tt
    expert = info[INFO_EXPERT:INFO_EXPERT + 2].T.astype(jnp.int32)
    rank = info[INFO_RANK:INFO_RANK + 2].T.astype(jnp.int32)
    counts = cnt.reshape(n_tiles, N_EXPERTS, LANES)[:, :, 0].astype(jnp.int32)
    run_rows = (counts + RUN - 1) // RUN * RUN
    local_off = jnp.cumsum(run_rows, axis=1) - run_rows
    totals = jnp.sum(run_rows, axis=0)
    seg_tiles = (totals + tm - 1) // tm
    seg_end = jnp.cumsum(seg_tiles)
    expert_row0 = (seg_end - seg_tiles) * tm
    tile_base = expert_row0[None, :] + jnp.cumsum(run_rows, axis=0) - run_rows
    is_expert = expert.reshape(n_tiles, tt, 2, 1) == jnp.arange(N_EXPERTS, dtype=jnp.int32)

    def per_slot(table):
        return jnp.sum(jnp.where(is_expert, table[:, None, None, :], 0), axis=-1)

    rank3 = rank.reshape(n_tiles, tt, 2)
    pos = per_slot(tile_base) + rank3
    lpos = per_slot(local_off) + rank3
    n_row_tiles = (2 * n + n_tiles * N_EXPERTS * (RUN - 1)) // tm + N_EXPERTS + 1
    used = seg_end[-1]
    t_idx = jnp.arange(n_row_tiles, dtype=jnp.int32)
    tile_blk = jnp.minimum(t_idx, used - 1)
    tile_expert = jnp.minimum(
        jnp.sum((tile_blk[:, None] >= seg_end[None, :]).astype(jnp.int32), axis=1), N_EXPERTS - 1)
    return {
        "pos": pos.reshape(n_tiles, 1, 2 * tt).astype(jnp.int32),
        "lpos": lpos.transpose(0, 2, 1).astype(jnp.int32),
        "runs": jnp.stack([local_off, tile_base, run_rows // RUN], axis=1).astype(jnp.int32),
        "gates": info[INFO_GATE:INFO_GATE + 2].T,
        "pads": jnp.stack([jnp.append(expert_row0 + totals, used),
                           jnp.append((seg_tiles * tm - totals) // RUN,
                                      n_row_tiles - used)]).astype(jnp.int32),
        "tile_expert": tile_expert.astype(jnp.int32),
        "tile_valid": (t_idx < used).astype(jnp.int32),
        "tile_blk": tile_blk.astype(jnp.int32),
        "n_rows": n_row_tiles * tm,
    }


ROW_SLAB = 8
assert ROW_SLAB * LANES == D_MODEL


def _slab_copy(src_ref, src_row, dst_ref, dst_row, sem):
    src = src_ref.at[pl.ds(pl.multiple_of(src_row * ROW_SLAB, ROW_SLAB), ROW_SLAB)]
    dst = dst_ref.at[pl.ds(pl.multiple_of(dst_row * ROW_SLAB, ROW_SLAB), ROW_SLAB)]
    return pltpu.make_async_copy(src, dst, sem)


def _rows_to_slabs(x, slab_ref):
    rows = x.shape[0]
    for s in range(ROW_SLAB):
        slab_ref[pl.ds(s, rows, stride=ROW_SLAB), :] = x[:, s * LANES:(s + 1) * LANES]


def _slabs_to_rows(slab_ref, rows):
    return jnp.concatenate(
        [slab_ref[pl.ds(s, rows, stride=ROW_SLAB), :] for s in range(ROW_SLAB)], axis=1)


DMA_UNROLL = 8


PACKED_WIDTH = D_MODEL // 2
HIGH_HALF = 0xFFFF0000


def _pack_bf16_pairs(x):
    lo = pltpu.bitcast(x[:, :PACKED_WIDTH], jnp.uint32)
    hi = pltpu.bitcast(x[:, PACKED_WIDTH:], jnp.uint32)
    return (lo >> 16) | (hi & jnp.uint32(HIGH_HALF))


def _unpack_bf16_pairs(u):
    lo = pltpu.bitcast(u << 16, F32)
    hi = pltpu.bitcast(u & jnp.uint32(HIGH_HALF), F32)
    return jnp.concatenate([lo, hi], axis=1).astype(BF16)


def _scatter_kernel(pads_ref, runs_ref, lpos_ref, x_ref, xs_ref, loc_ref, sems, *, tm):
    i = pl.program_id(0)
    last = pl.num_programs(0) - 1
    slot = i % 2
    n_local = loc_ref.shape[1]

    def unit_copy(src_ref, src_row, dst_row, sem):
        return pltpu.make_async_copy(
            src_ref.at[pl.ds(pl.multiple_of(src_row, RUN), RUN)],
            xs_ref.at[pl.ds(pl.multiple_of(dst_row, RUN), RUN)], sem)

    @pl.when(i == 0)
    def _():
        zeros = loc_ref.at[1]
        zeros[...] = jnp.zeros(zeros.shape, zeros.dtype)
        for e in range(N_EXPERTS):
            start, units = pads_ref[0, e], pads_ref[1, e]

            def fill(j, c, start=start):
                unit_copy(zeros, 0, start + j * RUN, sems.at[2]).start()
                return c

            def fill_wait(j, c):
                unit_copy(zeros, 0, 0, sems.at[2]).wait()
                return c

            lax.fori_loop(0, units, fill, 0)
            lax.fori_loop(0, units, fill_wait, 0)

        first_tail_tile, n_tail_tiles = pads_ref[0, N_EXPERTS], pads_ref[1, N_EXPERTS]

        def tail_copy(j):
            row0 = pl.multiple_of((first_tail_tile + j) * tm, tm)
            return pltpu.make_async_copy(zeros.at[pl.ds(0, tm)], xs_ref.at[pl.ds(row0, tm)],
                                         sems.at[2])

        def tail_fill(j, c):
            tail_copy(j).start()
            return c

        def tail_wait(j, c):
            tail_copy(j).wait()
            return c

        lax.fori_loop(0, n_tail_tiles, tail_fill, 0)
        lax.fori_loop(0, n_tail_tiles, tail_wait, 0)

    lpos = lpos_ref[...]
    row = lax.broadcasted_iota(jnp.int32, (n_local, 1), 0)
    onehot = jnp.where((row == lpos[0:1, :]) | (row == lpos[1:2, :]), 1.0, 0.0).astype(BF16)
    sorted_rows = _dot(onehot, x_ref[...].astype(BF16))
    loc_ref[slot] = _pack_bf16_pairs(sorted_rows)

    for e in range(N_EXPERTS):
        src0, dst0, units = runs_ref[i, 0, e], runs_ref[i, 1, e], runs_ref[i, 2, e]

        def issue(k, c, src0=src0, dst0=dst0):
            unit_copy(loc_ref.at[slot], src0 + k * RUN, dst0 + k * RUN, sems.at[slot]).start()
            return c

        lax.fori_loop(0, units, issue, 0)

    def drain(which, tile):
        n_units = runs_ref[tile, 2, 0]
        for e in range(1, N_EXPERTS):
            n_units = n_units + runs_ref[tile, 2, e]

        def wait_units(count):
            def body(k, c):
                for _ in range(count):
                    unit_copy(loc_ref.at[which], 0, 0, sems.at[which]).wait()
                return c
            return body

        lax.fori_loop(0, n_units // DMA_UNROLL, wait_units(DMA_UNROLL), 0)
        lax.fori_loop(0, n_units % DMA_UNROLL, wait_units(1), 0)

    @pl.when(i > 0)
    def _():
        drain(1 - slot, i - 1)

    @pl.when(i == last)
    def _():
        drain(slot, i)


def _scatter_call(x1, plan, tt, tm):
    n = x1.shape[0]
    grid_spec = pltpu.PrefetchScalarGridSpec(
        num_scalar_prefetch=2,
        grid=(n // tt,),
        in_specs=[
            pl.BlockSpec((None, 2, tt), lambda i, pads, runs: (i, 0, 0)),
            pl.BlockSpec((tt, D_MODEL), lambda i, pads, runs: (i, 0)),
        ],
        out_specs=pl.BlockSpec(memory_space=pl.ANY),
        scratch_shapes=[pltpu.VMEM((2, max(_local_rows(tt), tm), PACKED_WIDTH), jnp.uint32),
                        pltpu.SemaphoreType.DMA((3,))],
    )
    return pl.pallas_call(
        functools.partial(_scatter_kernel, tm=tm),
        grid_spec=grid_spec,
        out_shape=jax.ShapeDtypeStruct((plan["n_rows"], PACKED_WIDTH), jnp.uint32),
        compiler_params=_compiler_params(1),
        name="scatter",
    )(plan["pads"], plan["runs"], plan["lpos"], x1)


def _expert_kernel(l_ref, te_ref, tv_ref, tb_ref, xs_ref, wg_ref, wu_ref, wd_ref, o_ref):
    del l_ref, te_ref, tb_ref
    valid = tv_ref[pl.program_id(0)] != 0

    @pl.when(valid)
    def _():
        xb = _unpack_bf16_pairs(xs_ref[...])
        hidden = jax.nn.silu(_dot(xb, wg_ref[...])) * _dot(xb, wu_ref[...])
        _rows_to_slabs(_dot(hidden.astype(BF16), wd_ref[...]), o_ref)

    @pl.when(jnp.logical_not(valid))
    def _():
        o_ref[...] = jnp.zeros_like(o_ref)


def _expert_call(layer, xs, plan, p, tm):
    n_rows = xs.shape[0]

    def rows(t, l, te, tv, tb):
        return (tb[t], 0)

    def weights(t, l, te, tv, tb):
        return (l[0], te[t], 0, 0)

    grid_spec = pltpu.PrefetchScalarGridSpec(
        num_scalar_prefetch=4,
        grid=(n_rows // tm,),
        in_specs=[
            pl.BlockSpec((tm, PACKED_WIDTH), rows),
            pl.BlockSpec((None, None, D_MODEL, EXPERT_FF), weights),
            pl.BlockSpec((None, None, D_MODEL, EXPERT_FF), weights),
            pl.BlockSpec((None, None, EXPERT_FF, D_MODEL), weights),
        ],
        out_specs=pl.BlockSpec((tm * ROW_SLAB, LANES), lambda t, l, te, tv, tb: (t, 0)),
    )
    return pl.pallas_call(
        _expert_kernel,
        grid_spec=grid_spec,
        out_shape=jax.ShapeDtypeStruct((n_rows * ROW_SLAB, LANES), F32),
        compiler_params=_compiler_params(1),
        name="experts",
    )(layer, plan["tile_expert"], plan["tile_valid"], plan["tile_blk"], xs,
      p["w_gate"], p["w_up"], p["w_down"])


def _combine_kernel(l_ref, pos_ref, pos_next_ref, x_ref, gate_ref, os_ref, g2_ref, b2_ref,
                    o_ref, buf_ref, sems):
    del l_ref
    tt = x_ref.shape[0]
    i = pl.program_id(0)
    slot = i % 2

    def gather(p_ref, which):
        def body(r, c):
            for s in range(2):
                _slab_copy(os_ref, p_ref[0, 2 * r + s], buf_ref.at[which, s], r,
                           sems.at[which]).start(priority=s)
            return c
        lax.fori_loop(0, tt, body, 0, unroll=DMA_UNROLL)

    @pl.when(i == 0)
    def _():
        gather(pos_ref, 0)

    @pl.when(i + 1 < pl.num_programs(0))
    def _():
        gather(pos_next_ref, 1 - slot)

    def drain(r, c):
        for s in range(2):
            _slab_copy(os_ref, 0, buf_ref.at[slot, s], 0, sems.at[slot]).wait()
        return c

    lax.fori_loop(0, tt, drain, 0, unroll=DMA_UNROLL)
    gate = gate_ref[...]
    y = (gate[:, 0:1] * _slabs_to_rows(buf_ref.at[slot, 0], tt)
         + gate[:, 1:2] * _slabs_to_rows(buf_ref.at[slot, 1], tt))
    o_ref[...] = _layer_norm(DEEPNORM_ALPHA * x_ref[...] + y, g2_ref[...], b2_ref[...])


def _combine_call(layer, x1, out_sorted, plan, p, tt):
    n = x1.shape[0]

    def tok(i, l):
        return (i, 0)

    def lay3(i, l):
        return (l[0], 0, 0)

    n_tiles = n // tt
    grid_spec = pltpu.PrefetchScalarGridSpec(
        num_scalar_prefetch=1,
        grid=(n_tiles,),
        in_specs=[
            pl.BlockSpec((None, 1, 2 * tt), lambda i, l: (i, 0, 0), memory_space=pltpu.SMEM),
            pl.BlockSpec((None, 1, 2 * tt), lambda i, l: (jnp.minimum(i + 1, n_tiles - 1), 0, 0),
                         memory_space=pltpu.SMEM),
            pl.BlockSpec((tt, D_MODEL), tok),
            pl.BlockSpec((tt, 2), tok),
            pl.BlockSpec(memory_space=pl.ANY),
            pl.BlockSpec((None, 1, D_MODEL), lay3),
            pl.BlockSpec((None, 1, D_MODEL), lay3),
        ],
        out_specs=pl.BlockSpec((tt, D_MODEL), tok),
        scratch_shapes=[pltpu.VMEM((2, 2, tt * ROW_SLAB, LANES), F32),
                        pltpu.SemaphoreType.DMA((2,))],
    )
    return pl.pallas_call(
        _combine_kernel,
        grid_spec=grid_spec,
        out_shape=jax.ShapeDtypeStruct((n, D_MODEL), F32),
        compiler_params=_compiler_params(1),
        name="combine",
    )(layer, plan["pos"], plan["pos"], x1, plan["gates"], out_sorted, p["ln2_g"], p["ln2_b"])


def _rope_tables(max_len):
    t = jnp.arange(max_len, dtype=jnp.int32)
    row = (t // GRID_W).astype(F32)
    col = (t % GRID_W).astype(F32)
    n_pairs_axis = HEAD_DIM // 4
    inv_freq = 1.0 / (ROPE_THETA ** (jnp.arange(n_pairs_axis, dtype=F32) / n_pairs_axis))
    ang = jnp.concatenate([row[:, None] * inv_freq, col[:, None] * inv_freq], -1)
    cos, sin = jnp.cos(ang), jnp.sin(ang)
    reps = LANES // HEAD_DIM
    cos128 = jnp.tile(jnp.concatenate([cos, cos], -1), (1, reps))
    sin128 = jnp.tile(jnp.concatenate([-sin, sin], -1), (1, reps))
    return cos128, sin128


def _prepare_params(max_len, w_in, q_norm, k_norm, gmlp_ln_g, gmlp_ln_b, w_spatial, b_spatial,
                    attn_out_norm, gmlp_out_norm, w_out, ln1_g, ln1_b, w_router, router_bias,
                    w_gate, w_up, w_down, ln2_g, ln2_b):
    depth = w_in.shape[0]
    perm64 = jnp.concatenate([jnp.arange(0, HEAD_DIM, 2), jnp.arange(1, HEAD_DIM, 2)])
    n_rot_heads = N_Q_HEADS + N_KV_HEADS
    rot_cols = (jnp.arange(n_rot_heads)[:, None] * HEAD_DIM + perm64[None, :]).reshape(-1)
    cols = jnp.concatenate([rot_cols, jnp.arange(n_rot_heads * HEAD_DIM, IN_WIDTH)])
    cos128, sin128 = _rope_tables(max_len)
    head_of_lane = jnp.arange(ATTN_WIDTH) // HEAD_DIM
    row3 = lambda a: a.reshape(depth, 1, -1).astype(F32)
    wr_t = w_router.astype(F32).T
    wr_hi = wr_t.astype(BF16)
    max_offset = (1.01 * LOG2_E * HEAD_DIM ** 0.5
                  * jnp.max(jnp.abs(q_norm), axis=1) * jnp.max(jnp.abs(k_norm), axis=1))
    return {
        "attn_needs_exact_max": max_offset > MAX_SAFE_OFFSET,
        "w_in": w_in[:, :, cols].astype(BF16),
        "cos": cos128,
        "sin": sin128,
        "seg": (head_of_lane[:, None] == head_of_lane[None, :]).astype(BF16),
        "q_norm": row3(jnp.tile(q_norm[:, perm64], (1, N_Q_HEADS))),
        "k_norm": row3(jnp.tile(k_norm[:, perm64], (1, N_KV_HEADS))),
        "gmlp_ln_g": row3(gmlp_ln_g),
        "gmlp_ln_b": row3(gmlp_ln_b),
        "w_cat": w_spatial.transpose(0, 2, 1, 3).reshape(depth, CHUNK, N_GMLP_HEADS * CHUNK).astype(BF16),
        "b_sp": jnp.repeat(b_spatial.transpose(0, 2, 1), GMLP_HEAD_DIM, axis=2).astype(F32),
        "attn_out_norm": row3(attn_out_norm),
        "gmlp_out_norm": row3(gmlp_out_norm),
        "w_out": w_out.astype(BF16),
        "ln1_g": row3(ln1_g),
        "ln1_b": row3(ln1_b),
        "wr": jnp.concatenate([wr_hi, (wr_t - wr_hi.astype(F32)).astype(BF16)], axis=0),
        "router_bias": router_bias.astype(F32).reshape(N_EXPERTS, 1),
        "w_gate": w_gate.astype(BF16),
        "w_up": w_up.astype(BF16),
        "w_down": w_down.astype(BF16),
        "ln2_g": row3(ln2_g),
        "ln2_b": row3(ln2_b),
    }


def _trunk(x_tokens, groups, p, depth, tp, tt, tm, tq, tk):
    (_, n_seq0, s0), = groups[:1]
    n_prompt_blocks = (n_seq0 * s0) // tp
    prompt_period = s0 // tp
    sample_period = (groups[1][2] if len(groups) > 1 else s0) // tp
    periods = (n_prompt_blocks, prompt_period, sample_period)

    def layer_body(l, x):
        layer = jnp.reshape(l, (1,)).astype(jnp.int32)
        q, k, vt, gm, kstat = _proj_call(layer, x, p, tp, periods)

        def attention(exact_max):
            outs = [_attn_call(q, k, vt, kstat, start, n_seq, s_len, min(tq, s_len),
                               min(tk, s_len), tp, exact_max)
                    for (start, n_seq, s_len) in groups]
            return outs[0] if len(outs) == 1 else jnp.concatenate(outs, axis=0)

        attn = lax.cond(p["attn_needs_exact_max"][l],
                        functools.partial(attention, True), functools.partial(attention, False))
        x1, info, cnt = _post_call(layer, x, attn, gm, p, tt, tiles_per_step=2)
        plan = _plan_routes(info, cnt, tt, tm)
        xs = _scatter_call(x1, plan, tt, tm)
        out_sorted = _expert_call(layer, xs, plan, p, tm)
        return _combine_call(layer, x1, out_sorted, plan, p, tt)

    return lax.fori_loop(0, depth, layer_body, x_tokens)


def kernel(x_prompt, x_sample, w_in, q_norm, k_norm, gmlp_ln_g, gmlp_ln_b, w_spatial, b_spatial,
           attn_out_norm, gmlp_out_norm, w_out, ln1_g, ln1_b, w_router, router_bias,
           w_gate, w_up, w_down, ln2_g, ln2_b):
    bp, sp, d = x_prompt.shape
    bs, ss, _ = x_sample.shape
    assert sp % ss == 0 and (bp * sp) % ss == 0, "sample sequences must tile the prompt stream"
    p = _prepare_params(max(sp, ss), w_in, q_norm, k_norm, gmlp_ln_g, gmlp_ln_b, w_spatial,
                        b_spatial, attn_out_norm, gmlp_out_norm, w_out, ln1_g, ln1_b, w_router,
                        router_bias, w_gate, w_up, w_down, ln2_g, ln2_b)
    x = jnp.concatenate([x_prompt.reshape(bp * sp, d), x_sample.reshape(bs * ss, d)], axis=0)
    groups = ((0, bp, sp), (bp * sp, bs, ss))
    y = _trunk(x.astype(F32), groups, p, w_in.shape[0], tp=min(512, ss), tt=256, tm=512,
               tq=512, tk=512)
    return (y[:bp * sp].reshape(bp, sp, d), y[bp * sp:].reshape(bs, ss, d))
```

```python
import functools

import jax
import jax.numpy as jnp
from jax import lax
from jax.experimental import pallas as pl
from jax.experimental.pallas import tpu as pltpu

D_MODEL = 1024
DEPTH = 4
HEAD_DIM = 64
N_Q_HEADS = 8
N_KV_HEADS = 2
ATTN_WIDTH = N_Q_HEADS * HEAD_DIM
KV_WIDTH = N_KV_HEADS * HEAD_DIM
GMLP_WIDTH = 512
N_GMLP_HEADS = 8
GMLP_HEAD_DIM = 64
CHUNK = 128
IN_WIDTH = ATTN_WIDTH + 2 * KV_WIDTH + 2 * GMLP_WIDTH
N_EXPERTS = 16
EXPERTS_PER_GROUP = 4
EXPERT_FF = 512
ROPE_THETA = 10000.0
GRID_W = 64
EPS = 1e-6
DEEPNORM_ALPHA = float((2 * DEPTH) ** 0.25)
LOG2_E = 1.4426950408889634
STAT_ROWS = 8
MAX_SAFE_OFFSET = 60.0

LANES = 128
F32 = jnp.float32
BF16 = jnp.bfloat16
VMEM_LIMIT_BYTES = 48 * 1024 * 1024


def _compiler_params(n_grid_dims):
    return pltpu.CompilerParams(
        dimension_semantics=("arbitrary",) * n_grid_dims,
        vmem_limit_bytes=VMEM_LIMIT_BYTES,
    )


def _dot(a, b):
    return jnp.dot(a, b, preferred_element_type=F32)


def _layer_norm(x, g, b):
    mu = jnp.mean(x, axis=-1, keepdims=True)
    xc = x - mu
    var = jnp.mean(xc * xc, axis=-1, keepdims=True)
    return xc * lax.rsqrt(var + EPS) * g + b


def _rms_norm(x, g):
    return x * lax.rsqrt(jnp.mean(x * x, axis=-1, keepdims=True) + EPS) * g


def _head_rms_norm(x, seg, g):
    ssq = _dot((x * x).astype(BF16), seg)
    return x * lax.rsqrt(ssq * (1.0 / HEAD_DIM) + EPS) * g


def _rope_half_split(x, cos, sin_signed):
    first_half = (lax.broadcasted_iota(jnp.int32, (1, LANES), 1) % HEAD_DIM) < (HEAD_DIM // 2)
    cols = []
    for c in range(x.shape[1] // LANES):
        xc = x[:, c * LANES:(c + 1) * LANES]
        partner = jnp.where(first_half,
                            pltpu.roll(xc, LANES - HEAD_DIM // 2, 1),
                            pltpu.roll(xc, HEAD_DIM // 2, 1))
        cols.append(xc * cos + partner * sin_signed)
    return cols[0] if len(cols) == 1 else jnp.concatenate(cols, axis=1)


def _proj_kernel(l_ref, x_ref, w_in_ref, cos_ref, sin_ref, seg_ref, qn_ref, kn_ref,
                 lng_ref, lnb_ref, wcat_ref, bsp_ref, gon_ref,
                 q_ref, k_ref, vt_ref, gm_ref, kstat_ref):
    del l_ref
    tt = x_ref.shape[0]
    h = _dot(x_ref[...].astype(BF16), w_in_ref[...])
    cos = cos_ref[...]
    sin_signed = sin_ref[...]

    q = _head_rms_norm(h[:, :ATTN_WIDTH], seg_ref[...], qn_ref[...])
    q = _rope_half_split(q, cos, sin_signed) * (LOG2_E * HEAD_DIM ** -0.5)
    q_ref[...] = q.astype(BF16)

    k0 = ATTN_WIDTH
    seg_kv = seg_ref[:KV_WIDTH, :KV_WIDTH]
    k = _head_rms_norm(h[:, k0:k0 + KV_WIDTH], seg_kv, kn_ref[...])
    k = _rope_half_split(k, cos, sin_signed).astype(BF16)
    kf = k.astype(F32)
    k_sq = _dot((kf * kf).astype(BF16), seg_kv)
    kstat_ref[...] = jnp.broadcast_to(jnp.max(k_sq, axis=0, keepdims=True), kstat_ref.shape)
    for hk in range(N_KV_HEADS):
        k_ref[hk] = k[:, hk * HEAD_DIM:(hk + 1) * HEAD_DIM]

    v0 = k0 + KV_WIDTH
    vt_ref[...] = h[:, v0:v0 + KV_WIDTH].T.astype(BF16)

    u0 = v0 + KV_WIDTH
    u = jax.nn.gelu(h[:, u0:u0 + GMLP_WIDTH])
    vg = _layer_norm(jax.nn.gelu(h[:, u0 + GMLP_WIDTH:]), lng_ref[...], lnb_ref[...])
    lane_head = lax.broadcasted_iota(jnp.int32, (1, GMLP_WIDTH), 1) // GMLP_HEAD_DIM
    mixed = []
    for c in range(tt // CHUNK):
        vc = vg[c * CHUNK:(c + 1) * CHUNK].astype(BF16)
        stack = jnp.concatenate(
            [jnp.where(lane_head == hh, vc, jnp.zeros_like(vc)) for hh in range(N_GMLP_HEADS)],
            axis=0)
        mixed.append(_dot(wcat_ref[...], stack) + bsp_ref[...])
    mixed = mixed[0] if len(mixed) == 1 else jnp.concatenate(mixed, axis=0)
    gm_ref[...] = _rms_norm(u * mixed, gon_ref[...]).astype(BF16)


def _proj_call(layer, x, p, tt, rope_period_blocks):
    n = x.shape[0]
    n_prompt_blocks, prompt_period, sample_period = rope_period_blocks

    def tok(i, l):
        return (i, 0)

    def rope_idx(i, l):
        return (jnp.where(i < n_prompt_blocks, i % prompt_period, i % sample_period), 0)

    def lay3(i, l):
        return (l[0], 0, 0)

    def const2(i, l):
        return (0, 0)

    grid_spec = pltpu.PrefetchScalarGridSpec(
        num_scalar_prefetch=1,
        grid=(n // tt,),
        in_specs=[
            pl.BlockSpec((tt, D_MODEL), tok),
            pl.BlockSpec((None, D_MODEL, IN_WIDTH), lay3),
            pl.BlockSpec((tt, LANES), rope_idx),
            pl.BlockSpec((tt, LANES), rope_idx),
            pl.BlockSpec((ATTN_WIDTH, ATTN_WIDTH), const2),
            pl.BlockSpec((None, 1, ATTN_WIDTH), lay3),
            pl.BlockSpec((None, 1, KV_WIDTH), lay3),
            pl.BlockSpec((None, 1, GMLP_WIDTH), lay3),
            pl.BlockSpec((None, 1, GMLP_WIDTH), lay3),
            pl.BlockSpec((None, CHUNK, N_GMLP_HEADS * CHUNK), lay3),
            pl.BlockSpec((None, CHUNK, GMLP_WIDTH), lay3),
            pl.BlockSpec((None, 1, GMLP_WIDTH), lay3),
        ],
        out_specs=[
            pl.BlockSpec((tt, ATTN_WIDTH), tok),
            pl.BlockSpec((N_KV_HEADS, tt, HEAD_DIM), lambda i, l: (0, i, 0)),
            pl.BlockSpec((KV_WIDTH, tt), lambda i, l: (0, i)),
            pl.BlockSpec((tt, GMLP_WIDTH), tok),
            pl.BlockSpec((STAT_ROWS, KV_WIDTH), tok),
        ],
    )
    return pl.pallas_call(
        _proj_kernel,
        grid_spec=grid_spec,
        out_shape=[
            jax.ShapeDtypeStruct((n, ATTN_WIDTH), BF16),
            jax.ShapeDtypeStruct((N_KV_HEADS, n, HEAD_DIM), BF16),
            jax.ShapeDtypeStruct((KV_WIDTH, n), BF16),
            jax.ShapeDtypeStruct((n, GMLP_WIDTH), BF16),
            jax.ShapeDtypeStruct((n // tt * STAT_ROWS, KV_WIDTH), F32),
        ],
        compiler_params=_compiler_params(1),
        name="proj",
    )(layer, x, p["w_in"], p["cos"], p["sin"], p["seg"], p["q_norm"], p["k_norm"],
      p["gmlp_ln_g"], p["gmlp_ln_b"], p["w_cat"], p["b_sp"], p["gmlp_out_norm"])


def _scores(k_blk, q):
    return lax.dot_general(k_blk, q, (((1,), (1,)), ((), ())), preferred_element_type=F32)


def _attn_kernel(q_ref, k_ref, vt_ref, kstat_ref, o_ref, *, tk, exact_max):
    tq = q_ref.shape[0]
    s_len = k_ref.shape[0]
    heads = q_ref.shape[1] // HEAD_DIM
    q_all = q_ref[...]
    q = jnp.concatenate([q_all[:, g * HEAD_DIM:(g + 1) * HEAD_DIM] for g in range(heads)], axis=0)
    nq = heads * tq
    if exact_max:
        def max_body(i, m):
            k_blk = k_ref[pl.ds(pl.multiple_of(i * tk, tk), tk), :]
            return jnp.maximum(m, jnp.max(_scores(k_blk, q), axis=0, keepdims=True))
        off = lax.fori_loop(0, s_len // tk, max_body, jnp.full((1, nq), -jnp.inf, F32))
    else:
        hk = pl.program_id(1)
        lane = lax.broadcasted_iota(jnp.int32, (1, LANES), 1)
        kstat = jnp.max(kstat_ref[...], axis=0, keepdims=True)
        kmax2 = jnp.max(jnp.where(lane // HEAD_DIM == hk, kstat, 0.0), axis=1, keepdims=True)
        qf = q.astype(F32)
        qq = _scores(jnp.ones((8, HEAD_DIM), BF16), (qf * qf).astype(BF16))
        off = jnp.sqrt(qq[0:1] * kmax2)

    l8 = jnp.zeros((8, nq), F32)
    acc = jnp.zeros((HEAD_DIM, nq), F32)
    for c in range(s_len // tk):
        k_blk = k_ref[c * tk:(c + 1) * tk, :]
        vt_blk = vt_ref[:, c * tk:(c + 1) * tk]
        p = jnp.exp2(_scores(k_blk, q) - off)
        l8 = l8 + jnp.sum(p.reshape(tk // 8, 8, nq), axis=0)
        acc = acc + _dot(vt_blk, p.astype(BF16))
    out = acc / jnp.sum(l8, axis=0, keepdims=True)
    out_t = jnp.concatenate([out[:, g * tq:(g + 1) * tq] for g in range(heads)], axis=0)
    o_ref[...] = out_t.T.astype(o_ref.dtype)


def _attn_call(q, k, vt, kstat, token_start, n_seq, s_len, tq, tk, tt, exact_max):
    heads_per_step = 4
    q_cols = heads_per_step * HEAD_DIM
    steps_per_kv = (N_Q_HEADS // N_KV_HEADS) // heads_per_step
    q_blk0 = token_start // tq
    seq0 = token_start // s_len
    nq = s_len // tq
    stat_rows = s_len // tt * STAT_ROWS

    def q_idx(b, hk, gp, j):
        return (q_blk0 + b * nq + j, hk * steps_per_kv + gp)

    return pl.pallas_call(
        functools.partial(_attn_kernel, tk=tk, exact_max=exact_max),
        grid=(n_seq, N_KV_HEADS, steps_per_kv, nq),
        in_specs=[
            pl.BlockSpec((tq, q_cols), q_idx),
            pl.BlockSpec((None, s_len, HEAD_DIM), lambda b, hk, gp, j: (hk, seq0 + b, 0)),
            pl.BlockSpec((HEAD_DIM, s_len), lambda b, hk, gp, j: (hk, seq0 + b)),
            pl.BlockSpec((stat_rows, KV_WIDTH), lambda b, hk, gp, j: (seq0 + b, 0)),
        ],
        out_specs=pl.BlockSpec((tq, q_cols),
                               lambda b, hk, gp, j: (b * nq + j, hk * steps_per_kv + gp)),
        out_shape=jax.ShapeDtypeStruct((n_seq * s_len, ATTN_WIDTH), BF16),
        compiler_params=_compiler_params(4),
        name="attn_exact_max" if exact_max else "attn",
    )(q, k, vt, kstat)


def _group_partner(x, row, d):
    pos = row % EXPERTS_PER_GROUP
    return jnp.where(pos + d < EXPERTS_PER_GROUP,
                     pltpu.roll(x, N_EXPERTS - d, 0),
                     pltpu.roll(x, EXPERTS_PER_GROUP - d, 0))


def _route(scores, bias, row):
    biased = scores + bias
    rank = jnp.zeros(biased.shape, jnp.int32)
    pos = row % EXPERTS_PER_GROUP
    for d in range(1, EXPERTS_PER_GROUP):
        other = _group_partner(biased, row, d)
        other_is_lower = (pos + d) >= EXPERTS_PER_GROUP
        beats = (other > biased) | ((other == biased) & other_is_lower)
        rank = rank + beats.astype(jnp.int32)
    top2 = rank < 2
    contrib = jnp.where(top2, biased, 0.0)
    group_score = contrib
    for d in range(1, EXPERTS_PER_GROUP):
        group_score = group_score + _group_partner(contrib, row, d)
    n_groups = N_EXPERTS // EXPERTS_PER_GROUP
    gpos = row // EXPERTS_PER_GROUP
    losses = jnp.zeros(biased.shape, jnp.int32)
    for d in range(1, n_groups):
        other = pltpu.roll(group_score, N_EXPERTS - d * EXPERTS_PER_GROUP, 0)
        other_is_lower = (gpos + d) >= n_groups
        beats = (other > group_score) | ((other == group_score) & other_is_lower)
        losses = losses + beats.astype(jnp.int32)
    selected = top2 & (losses == 0)
    w = jnp.where(selected, scores, 0.0)
    return w / jnp.sum(w, axis=0, keepdims=True), selected


INFO_EXPERT, INFO_RANK, INFO_GATE, INFO_ROWS = 0, 2, 4, 8


def _post_kernel(l_ref, x_ref, *refs, tt, group_first_step):
    del l_ref
    n_groups = len(group_first_step)
    attn_refs = refs[:n_groups]
    (gm_ref, aon_ref, wo_ref, g1_ref, b1_ref, wr_ref, rb_ref,
     x1_ref, info_ref, cnt_ref) = refs[n_groups:]
    ts = x_ref.shape[0]
    attn = attn_refs[0][...]
    for g in range(1, n_groups):
        attn = jnp.where(pl.program_id(0) >= group_first_step[g], attn_refs[g][...], attn)
    a = _rms_norm(attn.astype(F32), aon_ref[...]).astype(BF16)
    mix = _dot(a, wo_ref[:ATTN_WIDTH, :]) + _dot(gm_ref[...], wo_ref[ATTN_WIDTH:, :])
    x1 = _layer_norm(DEEPNORM_ALPHA * x_ref[...] + mix, g1_ref[...], b1_ref[...])
    x1_ref[...] = x1
    x_hi = x1.astype(BF16)
    x_lo = (x1 - x_hi.astype(F32)).astype(BF16)
    by_hi = _scores(wr_ref[...], x_hi)
    logits = (by_hi[:N_EXPERTS] + by_hi[N_EXPERTS:]
              + _scores(wr_ref[:N_EXPERTS, :], x_lo))
    row = lax.broadcasted_iota(jnp.int32, (N_EXPERTS, 1), 0)
    gates, selected = _route(jax.nn.sigmoid(logits), rb_ref[...], row)

    sel = jnp.where(selected, 1.0, 0.0)
    earlier = lax.broadcasted_iota(jnp.int32, (ts, ts), 0)
    later = lax.broadcasted_iota(jnp.int32, (ts, ts), 1)
    prefix = jnp.where((earlier <= later) & (earlier // tt == later // tt), 1.0, 0.0).astype(BF16)
    incl = _dot(sel.astype(BF16), prefix)
    rank = incl - sel
    for t in range(ts // tt):
        last = (t + 1) * tt - 1
        cnt_ref[t * N_EXPERTS:(t + 1) * N_EXPERTS] = jnp.broadcast_to(
            incl[:, last:last + 1], (N_EXPERTS, LANES))

    row_f = row.astype(F32)
    e_lo = jnp.min(jnp.where(selected, row_f, float(N_EXPERTS)), axis=0, keepdims=True)
    e_hi = jnp.max(jnp.where(selected, row_f, -1.0), axis=0, keepdims=True)

    def pick(mat, e):
        return jnp.sum(jnp.where(row_f == e, mat, 0.0), axis=0, keepdims=True)

    fields = {INFO_EXPERT: e_lo, INFO_EXPERT + 1: e_hi,
              INFO_RANK: pick(rank, e_lo), INFO_RANK + 1: pick(rank, e_hi),
              INFO_GATE: pick(gates, e_lo), INFO_GATE + 1: pick(gates, e_hi)}
    info_row = lax.broadcasted_iota(jnp.int32, (INFO_ROWS, 1), 0)
    info = jnp.zeros((INFO_ROWS, ts), F32)
    for r, val in fields.items():
        info = jnp.where(info_row == r, val, info)
    info_ref[...] = info


def _post_call(layer, x, attn_groups, gm, p, tt, tiles_per_step):
    n = x.shape[0]
    ts = tt * tiles_per_step
    group_first_step, first = [], 0
    for a in attn_groups:
        group_first_step.append(first)
        first += a.shape[0] // ts

    def tok(i, l):
        return (i, 0)

    def group_block(first_step, n_steps):
        return lambda i, l: (jnp.clip(i - first_step, 0, n_steps - 1), 0)

    def lay3(i, l):
        return (l[0], 0, 0)

    def const2(i, l):
        return (0, 0)

    grid_spec = pltpu.PrefetchScalarGridSpec(
        num_scalar_prefetch=1,
        grid=(n // ts,),
        in_specs=[
            pl.BlockSpec((ts, D_MODEL), tok),
            *[pl.BlockSpec((ts, ATTN_WIDTH), group_block(f, a.shape[0] // ts))
              for f, a in zip(group_first_step, attn_groups)],
            pl.BlockSpec((ts, GMLP_WIDTH), tok),
            pl.BlockSpec((None, 1, ATTN_WIDTH), lay3),
            pl.BlockSpec((None, D_MODEL, D_MODEL), lay3),
            pl.BlockSpec((None, 1, D_MODEL), lay3),
            pl.BlockSpec((None, 1, D_MODEL), lay3),
            pl.BlockSpec((2 * N_EXPERTS, D_MODEL), const2),
            pl.BlockSpec((N_EXPERTS, 1), const2),
        ],
        out_specs=[pl.BlockSpec((ts, D_MODEL), tok),
                   pl.BlockSpec((INFO_ROWS, ts), lambda i, l: (0, i)),
                   pl.BlockSpec((tiles_per_step * N_EXPERTS, LANES), tok)],
    )
    return pl.pallas_call(
        functools.partial(_post_kernel, tt=tt, group_first_step=tuple(group_first_step)),
        grid_spec=grid_spec,
        out_shape=[jax.ShapeDtypeStruct((n, D_MODEL), F32),
                   jax.ShapeDtypeStruct((INFO_ROWS, n), F32),
                   jax.ShapeDtypeStruct((n // tt * N_EXPERTS, LANES), F32)],
        compiler_params=_compiler_params(1),
        name="post",
    )(layer, x, *attn_groups, gm, p["attn_out_norm"], p["w_out"], p["ln1_g"], p["ln1_b"],
      p["wr"], p["router_bias"])


RUN = 8


def _local_rows(tt):
    return -(-(2 * tt + N_EXPERTS * (RUN - 1)) // LANES) * LANES


def _plan_routes(info, cnt, tt, tm):
    n = info.shape[1]
    n_tiles = n // tt
    expert = info[INFO_EXPERT:INFO_EXPERT + 2].T.astype(jnp.int32)
    rank = info[INFO_RANK:INFO_RANK + 2].T.astype(jnp.int32)
    counts = cnt.reshape(n_tiles, N_EXPERTS, LANES)[:, :, 0].astype(jnp.int32)
    run_rows = (counts + RUN - 1) // RUN * RUN
    local_off = jnp.cumsum(run_rows, axis=1) - run_rows
    totals = jnp.sum(run_rows, axis=0)
    seg_tiles = (totals + tm - 1) // tm
    seg_end = jnp.cumsum(seg_tiles)
    expert_row0 = (seg_end - seg_tiles) * tm
    tile_base = expert_row0[None, :] + jnp.cumsum(run_rows, axis=0) - run_rows
    is_expert = expert.reshape(n_tiles, tt, 2, 1) == jnp.arange(N_EXPERTS, dtype=jnp.int32)

    def per_slot(table):
        return jnp.sum(jnp.where(is_expert, table[:, None, None, :], 0), axis=-1)

    rank3 = rank.reshape(n_tiles, tt, 2)
    pos = per_slot(tile_base) + rank3
    lpos = per_slot(local_off) + rank3
    units = run_rows // RUN
    unit_end = jnp.cumsum(units, axis=1)
    u = jnp.arange(2 * tt // RUN + N_EXPERTS, dtype=jnp.int32)
    unit_expert = jnp.minimum(
        jnp.sum((u[None, :, None] >= unit_end[:, None, :]).astype(jnp.int32), axis=-1),
        N_EXPERTS - 1)
    is_unit_expert = unit_expert[:, :, None] == jnp.arange(N_EXPERTS, dtype=jnp.int32)

    def per_unit(table):
        return jnp.sum(jnp.where(is_unit_expert, table[:, None, :], 0), axis=-1)

    within = (u[None, :] - per_unit(unit_end - units)) * RUN
    unit_src = per_unit(local_off) + within
    unit_dst = per_unit(tile_base) + within
    n_row_tiles = (2 * n + n_tiles * N_EXPERTS * (RUN - 1)) // tm + N_EXPERTS + 1
    used = seg_end[-1]
    t_idx = jnp.arange(n_row_tiles, dtype=jnp.int32)
    tile_blk = jnp.minimum(t_idx, used - 1)
    tile_expert = jnp.minimum(
        jnp.sum((tile_blk[:, None] >= seg_end[None, :]).astype(jnp.int32), axis=1), N_EXPERTS - 1)
    return {
        "pos": pos.reshape(n_tiles, 1, 2 * tt).astype(jnp.int32),
        "lpos": lpos.transpose(0, 2, 1).astype(jnp.int32),
        "units": jnp.concatenate([unit_src, unit_dst], axis=1)[:, None, :].astype(jnp.int32),
        "n_units": unit_end[:, -1].astype(jnp.int32),
        "gates": info[INFO_GATE:INFO_GATE + 2].T,
        "pads": jnp.stack([jnp.append(expert_row0 + totals, used),
                           jnp.append((seg_tiles * tm - totals) // RUN,
                                      n_row_tiles - used)]).astype(jnp.int32),
        "tile_expert": tile_expert.astype(jnp.int32),
        "tile_valid": (t_idx < used).astype(jnp.int32),
        "tile_blk": tile_blk.astype(jnp.int32),
        "n_rows": n_row_tiles * tm,
    }


ROW_SLAB = 8
assert ROW_SLAB * LANES == D_MODEL


def _slab_copy(src_ref, src_row, dst_ref, dst_row, sem):
    src = src_ref.at[pl.ds(pl.multiple_of(src_row * ROW_SLAB, ROW_SLAB), ROW_SLAB)]
    dst = dst_ref.at[pl.ds(pl.multiple_of(dst_row * ROW_SLAB, ROW_SLAB), ROW_SLAB)]
    return pltpu.make_async_copy(src, dst, sem)


def _rows_to_slabs(x, slab_ref):
    rows = x.shape[0]
    for s in range(ROW_SLAB):
        slab_ref[pl.ds(s, rows, stride=ROW_SLAB), :] = x[:, s * LANES:(s + 1) * LANES]


def _slabs_to_rows(slab_ref, rows):
    return jnp.concatenate(
        [slab_ref[pl.ds(s, rows, stride=ROW_SLAB), :] for s in range(ROW_SLAB)], axis=1)


DMA_UNROLL = 8


PACKED_WIDTH = D_MODEL // 2
HIGH_HALF = 0xFFFF0000


def _pack_bf16_pairs(x):
    lo = pltpu.bitcast(x[:, :PACKED_WIDTH], jnp.uint32)
    hi = pltpu.bitcast(x[:, PACKED_WIDTH:], jnp.uint32)
    return (lo >> 16) | (hi & jnp.uint32(HIGH_HALF))


def _unpack_bf16_pairs(u):
    lo = pltpu.bitcast(u << 16, F32)
    hi = pltpu.bitcast(u & jnp.uint32(HIGH_HALF), F32)
    return jnp.concatenate([lo, hi], axis=1).astype(BF16)


def _scatter_kernel(pads_ref, n_units_ref, units_ref, lpos_ref, x_ref, xs_ref, loc_ref, sems,
                    *, tm):
    i = pl.program_id(0)
    last = pl.num_programs(0) - 1
    slot = i % 2
    n_local = loc_ref.shape[1]

    def unit_copy(src_ref, src_row, dst_row, sem):
        return pltpu.make_async_copy(
            src_ref.at[pl.ds(pl.multiple_of(src_row, RUN), RUN)],
            xs_ref.at[pl.ds(pl.multiple_of(dst_row, RUN), RUN)], sem)

    @pl.when(i == 0)
    def _():
        zeros = loc_ref.at[1]
        zeros[...] = jnp.zeros(zeros.shape, zeros.dtype)
        for e in range(N_EXPERTS):
            start, units = pads_ref[0, e], pads_ref[1, e]

            def fill(j, c, start=start):
                unit_copy(zeros, 0, start + j * RUN, sems.at[2]).start()
                return c

            def fill_wait(j, c):
                unit_copy(zeros, 0, 0, sems.at[2]).wait()
                return c

            lax.fori_loop(0, units, fill, 0)
            lax.fori_loop(0, units, fill_wait, 0)

        first_tail_tile, n_tail_tiles = pads_ref[0, N_EXPERTS], pads_ref[1, N_EXPERTS]

        def tail_copy(j):
            row0 = pl.multiple_of((first_tail_tile + j) * tm, tm)
            return pltpu.make_async_copy(zeros.at[pl.ds(0, tm)], xs_ref.at[pl.ds(row0, tm)],
                                         sems.at[2])

        def tail_fill(j, c):
            tail_copy(j).start()
            return c

        def tail_wait(j, c):
            tail_copy(j).wait()
            return c

        lax.fori_loop(0, n_tail_tiles, tail_fill, 0)
        lax.fori_loop(0, n_tail_tiles, tail_wait, 0)

    lpos = lpos_ref[...]
    row = lax.broadcasted_iota(jnp.int32, (n_local, 1), 0)
    onehot = jnp.where((row == lpos[0:1, :]) | (row == lpos[1:2, :]), 1.0, 0.0).astype(BF16)
    sorted_rows = _dot(onehot, x_ref[...].astype(BF16))
    loc_ref[slot] = _pack_bf16_pairs(sorted_rows)

    max_units = units_ref.shape[1] // 2

    def for_each_unit(n_units, fn):
        full = n_units // DMA_UNROLL

        def block(k, c):
            for j in range(DMA_UNROLL):
                fn(k * DMA_UNROLL + j)
            return c

        def single(u, c):
            fn(u)
            return c

        lax.fori_loop(0, full, block, 0)
        lax.fori_loop(full * DMA_UNROLL, n_units, single, 0)

    def start_unit(u):
        unit_copy(loc_ref.at[slot], units_ref[0, u], units_ref[0, max_units + u],
                  sems.at[slot]).start()

    for_each_unit(n_units_ref[i], start_unit)

    def drain(which, tile):
        for_each_unit(n_units_ref[tile],
                      lambda u: unit_copy(loc_ref.at[which], 0, 0, sems.at[which]).wait())

    @pl.when(i > 0)
    def _():
        drain(1 - slot, i - 1)

    @pl.when(i == last)
    def _():
        drain(slot, i)


def _scatter_call(x1, plan, tt, tm):
    n = x1.shape[0]
    grid_spec = pltpu.PrefetchScalarGridSpec(
        num_scalar_prefetch=2,
        grid=(n // tt,),
        in_specs=[
            pl.BlockSpec((None, 1, plan["units"].shape[2]), lambda i, pads, nu: (i, 0, 0),
                         memory_space=pltpu.SMEM),
            pl.BlockSpec((None, 2, tt), lambda i, pads, nu: (i, 0, 0)),
            pl.BlockSpec((tt, D_MODEL), lambda i, pads, nu: (i, 0)),
        ],
        out_specs=pl.BlockSpec(memory_space=pl.ANY),
        scratch_shapes=[pltpu.VMEM((2, max(_local_rows(tt), tm), PACKED_WIDTH), jnp.uint32),
                        pltpu.SemaphoreType.DMA((3,))],
    )
    return pl.pallas_call(
        functools.partial(_scatter_kernel, tm=tm),
        grid_spec=grid_spec,
        out_shape=jax.ShapeDtypeStruct((plan["n_rows"], PACKED_WIDTH), jnp.uint32),
        compiler_params=_compiler_params(1),
        name="scatter",
    )(plan["pads"], plan["n_units"], plan["units"], plan["lpos"], x1)


def _expert_kernel(l_ref, te_ref, tv_ref, tb_ref, xs_ref, wg_ref, wu_ref, wd_ref, o_ref):
    del l_ref, te_ref, tb_ref
    valid = tv_ref[pl.program_id(0)] != 0

    @pl.when(valid)
    def _():
        xb = _unpack_bf16_pairs(xs_ref[...])
        hidden = jax.nn.silu(_dot(xb, wg_ref[...])) * _dot(xb, wu_ref[...])
        _rows_to_slabs(_dot(hidden.astype(BF16), wd_ref[...]), o_ref)

    @pl.when(jnp.logical_not(valid))
    def _():
        o_ref[...] = jnp.zeros_like(o_ref)


def _expert_call(layer, xs, plan, p, tm):
    n_rows = xs.shape[0]

    def rows(t, l, te, tv, tb):
        return (tb[t], 0)

    def weights(t, l, te, tv, tb):
        return (l[0], te[t], 0, 0)

    grid_spec = pltpu.PrefetchScalarGridSpec(
        num_scalar_prefetch=4,
        grid=(n_rows // tm,),
        in_specs=[
            pl.BlockSpec((tm, PACKED_WIDTH), rows),
            pl.BlockSpec((None, None, D_MODEL, EXPERT_FF), weights),
            pl.BlockSpec((None, None, D_MODEL, EXPERT_FF), weights),
            pl.BlockSpec((None, None, EXPERT_FF, D_MODEL), weights),
        ],
        out_specs=pl.BlockSpec((tm * ROW_SLAB, LANES), lambda t, l, te, tv, tb: (t, 0)),
    )
    return pl.pallas_call(
        _expert_kernel,
        grid_spec=grid_spec,
        out_shape=jax.ShapeDtypeStruct((n_rows * ROW_SLAB, LANES), F32),
        compiler_params=_compiler_params(1),
        name="experts",
    )(layer, plan["tile_expert"], plan["tile_valid"], plan["tile_blk"], xs,
      p["w_gate"], p["w_up"], p["w_down"])


def _combine_kernel(l_ref, pos_ref, pos_next_ref, x_ref, gate_ref, os_ref, g2_ref, b2_ref,
                    o_ref, buf_ref, sems):
    del l_ref
    tt = x_ref.shape[0]
    i = pl.program_id(0)
    slot = i % 2

    def gather(p_ref, which):
        def body(r, c):
            for s in range(2):
                _slab_copy(os_ref, p_ref[0, 2 * r + s], buf_ref.at[which, s], r,
                           sems.at[which]).start(priority=s)
            return c
        lax.fori_loop(0, tt, body, 0, unroll=DMA_UNROLL)

    @pl.when(i == 0)
    def _():
        gather(pos_ref, 0)

    @pl.when(i + 1 < pl.num_programs(0))
    def _():
        gather(pos_next_ref, 1 - slot)

    def drain(r, c):
        for s in range(2):
            _slab_copy(os_ref, 0, buf_ref.at[slot, s], 0, sems.at[slot]).wait()
        return c

    lax.fori_loop(0, tt, drain, 0, unroll=DMA_UNROLL)
    gate = gate_ref[...]
    y = (gate[:, 0:1] * _slabs_to_rows(buf_ref.at[slot, 0], tt)
         + gate[:, 1:2] * _slabs_to_rows(buf_ref.at[slot, 1], tt))
    o_ref[...] = _layer_norm(DEEPNORM_ALPHA * x_ref[...] + y, g2_ref[...], b2_ref[...])


def _combine_call(layer, x1, out_sorted, plan, p, tt):
    n = x1.shape[0]

    def tok(i, l):
        return (i, 0)

    def lay3(i, l):
        return (l[0], 0, 0)

    n_tiles = n // tt
    grid_spec = pltpu.PrefetchScalarGridSpec(
        num_scalar_prefetch=1,
        grid=(n_tiles,),
        in_specs=[
            pl.BlockSpec((None, 1, 2 * tt), lambda i, l: (i, 0, 0), memory_space=pltpu.SMEM),
            pl.BlockSpec((None, 1, 2 * tt), lambda i, l: (jnp.minimum(i + 1, n_tiles - 1), 0, 0),
                         memory_space=pltpu.SMEM),
            pl.BlockSpec((tt, D_MODEL), tok),
            pl.BlockSpec((tt, 2), tok),
            pl.BlockSpec(memory_space=pl.ANY),
            pl.BlockSpec((None, 1, D_MODEL), lay3),
            pl.BlockSpec((None, 1, D_MODEL), lay3),
        ],
        out_specs=pl.BlockSpec((tt, D_MODEL), tok),
        scratch_shapes=[pltpu.VMEM((2, 2, tt * ROW_SLAB, LANES), F32),
                        pltpu.SemaphoreType.DMA((2,))],
    )
    return pl.pallas_call(
        _combine_kernel,
        grid_spec=grid_spec,
        out_shape=jax.ShapeDtypeStruct((n, D_MODEL), F32),
        compiler_params=_compiler_params(1),
        name="combine",
    )(layer, plan["pos"], plan["pos"], x1, plan["gates"], out_sorted, p["ln2_g"], p["ln2_b"])


def _rope_tables(max_len):
    t = jnp.arange(max_len, dtype=jnp.int32)
    row = (t // GRID_W).astype(F32)
    col = (t % GRID_W).astype(F32)
    n_pairs_axis = HEAD_DIM // 4
    inv_freq = 1.0 / (ROPE_THETA ** (jnp.arange(n_pairs_axis, dtype=F32) / n_pairs_axis))
    ang = jnp.concatenate([row[:, None] * inv_freq, col[:, None] * inv_freq], -1)
    cos, sin = jnp.cos(ang), jnp.sin(ang)
    reps = LANES // HEAD_DIM
    cos128 = jnp.tile(jnp.concatenate([cos, cos], -1), (1, reps))
    sin128 = jnp.tile(jnp.concatenate([-sin, sin], -1), (1, reps))
    return cos128, sin128


def _prepare_params(max_len, w_in, q_norm, k_norm, gmlp_ln_g, gmlp_ln_b, w_spatial, b_spatial,
                    attn_out_norm, gmlp_out_norm, w_out, ln1_g, ln1_b, w_router, router_bias,
                    w_gate, w_up, w_down, ln2_g, ln2_b):
    depth = w_in.shape[0]
    perm64 = jnp.concatenate([jnp.arange(0, HEAD_DIM, 2), jnp.arange(1, HEAD_DIM, 2)])
    n_rot_heads = N_Q_HEADS + N_KV_HEADS
    rot_cols = (jnp.arange(n_rot_heads)[:, None] * HEAD_DIM + perm64[None, :]).reshape(-1)
    cols = jnp.concatenate([rot_cols, jnp.arange(n_rot_heads * HEAD_DIM, IN_WIDTH)])
    cos128, sin128 = _rope_tables(max_len)
    head_of_lane = jnp.arange(ATTN_WIDTH) // HEAD_DIM
    row3 = lambda a: a.reshape(depth, 1, -1).astype(F32)
    wr_t = w_router.astype(F32).T
    wr_hi = wr_t.astype(BF16)
    max_offset = (1.01 * LOG2_E * HEAD_DIM ** 0.5
                  * jnp.max(jnp.abs(q_norm), axis=1) * jnp.max(jnp.abs(k_norm), axis=1))
    return {
        "attn_needs_exact_max": max_offset > MAX_SAFE_OFFSET,
        "w_in": w_in[:, :, cols].astype(BF16),
        "cos": cos128,
        "sin": sin128,
        "seg": (head_of_lane[:, None] == head_of_lane[None, :]).astype(BF16),
        "q_norm": row3(jnp.tile(q_norm[:, perm64], (1, N_Q_HEADS))),
        "k_norm": row3(jnp.tile(k_norm[:, perm64], (1, N_KV_HEADS))),
        "gmlp_ln_g": row3(gmlp_ln_g),
        "gmlp_ln_b": row3(gmlp_ln_b),
        "w_cat": w_spatial.transpose(0, 2, 1, 3).reshape(depth, CHUNK, N_GMLP_HEADS * CHUNK).astype(BF16),
        "b_sp": jnp.repeat(b_spatial.transpose(0, 2, 1), GMLP_HEAD_DIM, axis=2).astype(F32),
        "attn_out_norm": row3(attn_out_norm),
        "gmlp_out_norm": row3(gmlp_out_norm),
        "w_out": w_out.astype(BF16),
        "ln1_g": row3(ln1_g),
        "ln1_b": row3(ln1_b),
        "wr": jnp.concatenate([wr_hi, (wr_t - wr_hi.astype(F32)).astype(BF16)], axis=0),
        "router_bias": router_bias.astype(F32).reshape(N_EXPERTS, 1),
        "w_gate": w_gate.astype(BF16),
        "w_up": w_up.astype(BF16),
        "w_down": w_down.astype(BF16),
        "ln2_g": row3(ln2_g),
        "ln2_b": row3(ln2_b),
    }


def _trunk(x_tokens, groups, p, depth, tp, tt, tm, tq, tk):
    (_, n_seq0, s0), = groups[:1]
    n_prompt_blocks = (n_seq0 * s0) // tp
    prompt_period = s0 // tp
    sample_period = (groups[1][2] if len(groups) > 1 else s0) // tp
    periods = (n_prompt_blocks, prompt_period, sample_period)

    def layer_body(l, x):
        layer = jnp.reshape(l, (1,)).astype(jnp.int32)
        q, k, vt, gm, kstat = _proj_call(layer, x, p, tp, periods)

        def attention(exact_max):
            return tuple(_attn_call(q, k, vt, kstat, start, n_seq, s_len, min(tq, s_len),
                                    min(tk, s_len), tp, exact_max)
                         for (start, n_seq, s_len) in groups)

        attn = lax.cond(p["attn_needs_exact_max"][l],
                        functools.partial(attention, True), functools.partial(attention, False))
        x1, info, cnt = _post_call(layer, x, attn, gm, p, tt, tiles_per_step=2)
        plan = _plan_routes(info, cnt, tt, tm)
        xs = _scatter_call(x1, plan, tt, tm)
        out_sorted = _expert_call(layer, xs, plan, p, tm)
        return _combine_call(layer, x1, out_sorted, plan, p, tt)

    return lax.fori_loop(0, depth, layer_body, x_tokens)


def kernel(x_prompt, x_sample, w_in, q_norm, k_norm, gmlp_ln_g, gmlp_ln_b, w_spatial, b_spatial,
           attn_out_norm, gmlp_out_norm, w_out, ln1_g, ln1_b, w_router, router_bias,
           w_gate, w_up, w_down, ln2_g, ln2_b):
    bp, sp, d = x_prompt.shape
    bs, ss, _ = x_sample.shape
    assert sp % ss == 0 and (bp * sp) % ss == 0, "sample sequences must tile the prompt stream"
    p = _prepare_params(max(sp, ss), w_in, q_norm, k_norm, gmlp_ln_g, gmlp_ln_b, w_spatial,
                        b_spatial, attn_out_norm, gmlp_out_norm, w_out, ln1_g, ln1_b, w_router,
                        router_bias, w_gate, w_up, w_down, ln2_g, ln2_b)
    x = jnp.concatenate([x_prompt.reshape(bp * sp, d), x_sample.reshape(bs * ss, d)], axis=0)
    groups = ((0, bp, sp), (bp * sp, bs, ss))
    y = _trunk(x.astype(F32), groups, p, w_in.shape[0], tp=min(512, ss), tt=256, tm=512,
               tq=256, tk=512)
    return (y[:bp * sp].reshape(bp, sp, d), y[bp * sp:].reshape(bs, ss, d))
```

```python
import functools

import jax
import jax.numpy as jnp
from jax import lax
from jax.experimental import pallas as pl
from jax.experimental.pallas import tpu as pltpu

D_MODEL = 1024
DEPTH = 4
HEAD_DIM = 64
N_Q_HEADS = 8
N_KV_HEADS = 2
ATTN_WIDTH = N_Q_HEADS * HEAD_DIM
KV_WIDTH = N_KV_HEADS * HEAD_DIM
GMLP_WIDTH = 512
N_GMLP_HEADS = 8
GMLP_HEAD_DIM = 64
CHUNK = 128
IN_WIDTH = ATTN_WIDTH + 2 * KV_WIDTH + 2 * GMLP_WIDTH
N_EXPERTS = 16
EXPERTS_PER_GROUP = 4
EXPERT_FF = 512
ROPE_THETA = 10000.0
GRID_W = 64
EPS = 1e-6
DEEPNORM_ALPHA = float((2 * DEPTH) ** 0.25)
LOG2_E = 1.4426950408889634
STAT_ROWS = 8
MAX_SAFE_OFFSET = 60.0

LANES = 128
F32 = jnp.float32
BF16 = jnp.bfloat16
VMEM_LIMIT_BYTES = 48 * 1024 * 1024


def _compiler_params(n_grid_dims):
    return pltpu.CompilerParams(
        dimension_semantics=("arbitrary",) * n_grid_dims,
        vmem_limit_bytes=VMEM_LIMIT_BYTES,
    )


def _dot(a, b):
    return jnp.dot(a, b, preferred_element_type=F32)


def _layer_norm(x, g, b):
    mu = jnp.mean(x, axis=-1, keepdims=True)
    xc = x - mu
    var = jnp.mean(xc * xc, axis=-1, keepdims=True)
    return xc * lax.rsqrt(var + EPS) * g + b


def _rms_norm(x, g):
    return x * lax.rsqrt(jnp.mean(x * x, axis=-1, keepdims=True) + EPS) * g


def _head_rms_norm(x, seg, g):
    ssq = _dot((x * x).astype(BF16), seg)
    return x * lax.rsqrt(ssq * (1.0 / HEAD_DIM) + EPS) * g


def _rope_half_split(x, cos, sin_signed):
    first_half = (lax.broadcasted_iota(jnp.int32, (1, LANES), 1) % HEAD_DIM) < (HEAD_DIM // 2)
    cols = []
    for c in range(x.shape[1] // LANES):
        xc = x[:, c * LANES:(c + 1) * LANES]
        partner = jnp.where(first_half,
                            pltpu.roll(xc, LANES - HEAD_DIM // 2, 1),
                            pltpu.roll(xc, HEAD_DIM // 2, 1))
        cols.append(xc * cos + partner * sin_signed)
    return cols[0] if len(cols) == 1 else jnp.concatenate(cols, axis=1)


def _proj_kernel(l_ref, x_ref, w_in_ref, cos_ref, sin_ref, seg_ref, qn_ref, kn_ref,
                 lng_ref, lnb_ref, wcat_ref, bsp_ref, gon_ref,
                 q_ref, k_ref, vt_ref, gm_ref, kstat_ref):
    del l_ref
    tt = x_ref.shape[0]
    h = _dot(x_ref[...].astype(BF16), w_in_ref[...])
    cos = cos_ref[...]
    sin_signed = sin_ref[...]

    q = _head_rms_norm(h[:, :ATTN_WIDTH], seg_ref[...], qn_ref[...])
    q = _rope_half_split(q, cos, sin_signed) * (LOG2_E * HEAD_DIM ** -0.5)
    q_ref[...] = q.astype(BF16)

    k0 = ATTN_WIDTH
    seg_kv = seg_ref[:KV_WIDTH, :KV_WIDTH]
    k = _head_rms_norm(h[:, k0:k0 + KV_WIDTH], seg_kv, kn_ref[...])
    k = _rope_half_split(k, cos, sin_signed).astype(BF16)
    kf = k.astype(F32)
    k_sq = _dot((kf * kf).astype(BF16), seg_kv)
    kstat_ref[...] = jnp.broadcast_to(jnp.max(k_sq, axis=0, keepdims=True), kstat_ref.shape)
    for hk in range(N_KV_HEADS):
        k_ref[hk] = k[:, hk * HEAD_DIM:(hk + 1) * HEAD_DIM]

    v0 = k0 + KV_WIDTH
    vt_ref[...] = h[:, v0:v0 + KV_WIDTH].T.astype(BF16)

    u0 = v0 + KV_WIDTH
    u = jax.nn.gelu(h[:, u0:u0 + GMLP_WIDTH])
    vg = _layer_norm(jax.nn.gelu(h[:, u0 + GMLP_WIDTH:]), lng_ref[...], lnb_ref[...])
    lane_head = lax.broadcasted_iota(jnp.int32, (1, GMLP_WIDTH), 1) // GMLP_HEAD_DIM
    mixed = []
    for c in range(tt // CHUNK):
        vc = vg[c * CHUNK:(c + 1) * CHUNK].astype(BF16)
        stack = jnp.concatenate(
            [jnp.where(lane_head == hh, vc, jnp.zeros_like(vc)) for hh in range(N_GMLP_HEADS)],
            axis=0)
        mixed.append(_dot(wcat_ref[...], stack) + bsp_ref[...])
    mixed = mixed[0] if len(mixed) == 1 else jnp.concatenate(mixed, axis=0)
    gm_ref[...] = _rms_norm(u * mixed, gon_ref[...]).astype(BF16)


def _proj_call(layer, x, p, tt, rope_period_blocks):
    n = x.shape[0]
    n_prompt_blocks, prompt_period, sample_period = rope_period_blocks

    def tok(i, l):
        return (i, 0)

    def rope_idx(i, l):
        return (jnp.where(i < n_prompt_blocks, i % prompt_period, i % sample_period), 0)

    def lay3(i, l):
        return (l[0], 0, 0)

    def const2(i, l):
        return (0, 0)

    grid_spec = pltpu.PrefetchScalarGridSpec(
        num_scalar_prefetch=1,
        grid=(n // tt,),
        in_specs=[
            pl.BlockSpec((tt, D_MODEL), tok),
            pl.BlockSpec((None, D_MODEL, IN_WIDTH), lay3),
            pl.BlockSpec((tt, LANES), rope_idx),
            pl.BlockSpec((tt, LANES), rope_idx),
            pl.BlockSpec((ATTN_WIDTH, ATTN_WIDTH), const2),
            pl.BlockSpec((None, 1, ATTN_WIDTH), lay3),
            pl.BlockSpec((None, 1, KV_WIDTH), lay3),
            pl.BlockSpec((None, 1, GMLP_WIDTH), lay3),
            pl.BlockSpec((None, 1, GMLP_WIDTH), lay3),
            pl.BlockSpec((None, CHUNK, N_GMLP_HEADS * CHUNK), lay3),
            pl.BlockSpec((None, CHUNK, GMLP_WIDTH), lay3),
            pl.BlockSpec((None, 1, GMLP_WIDTH), lay3),
        ],
        out_specs=[
            pl.BlockSpec((tt, ATTN_WIDTH), tok),
            pl.BlockSpec((N_KV_HEADS, tt, HEAD_DIM), lambda i, l: (0, i, 0)),
            pl.BlockSpec((KV_WIDTH, tt), lambda i, l: (0, i)),
            pl.BlockSpec((tt, GMLP_WIDTH), tok),
            pl.BlockSpec((STAT_ROWS, KV_WIDTH), tok),
        ],
    )
    return pl.pallas_call(
        _proj_kernel,
        grid_spec=grid_spec,
        out_shape=[
            jax.ShapeDtypeStruct((n, ATTN_WIDTH), BF16),
            jax.ShapeDtypeStruct((N_KV_HEADS, n, HEAD_DIM), BF16),
            jax.ShapeDtypeStruct((KV_WIDTH, n), BF16),
            jax.ShapeDtypeStruct((n, GMLP_WIDTH), BF16),
            jax.ShapeDtypeStruct((n // tt * STAT_ROWS, KV_WIDTH), F32),
        ],
        compiler_params=_compiler_params(1),
        name="proj",
    )(layer, x, p["w_in"], p["cos"], p["sin"], p["seg"], p["q_norm"], p["k_norm"],
      p["gmlp_ln_g"], p["gmlp_ln_b"], p["w_cat"], p["b_sp"], p["gmlp_out_norm"])


def _scores(k_blk, q):
    return lax.dot_general(k_blk, q, (((1,), (1,)), ((), ())), preferred_element_type=F32)


def _attn_kernel(q_ref, k_ref, vt_ref, kstat_ref, o_ref, *, tk, exact_max):
    tq = q_ref.shape[0]
    s_len = k_ref.shape[0]
    heads = q_ref.shape[1] // HEAD_DIM
    q_all = q_ref[...]
    q = jnp.concatenate([q_all[:, g * HEAD_DIM:(g + 1) * HEAD_DIM] for g in range(heads)], axis=0)
    nq = heads * tq
    if exact_max:
        def max_body(i, m):
            k_blk = k_ref[pl.ds(pl.multiple_of(i * tk, tk), tk), :]
            return jnp.maximum(m, jnp.max(_scores(k_blk, q), axis=0, keepdims=True))
        off = lax.fori_loop(0, s_len // tk, max_body, jnp.full((1, nq), -jnp.inf, F32))
    else:
        hk = pl.program_id(1)
        lane = lax.broadcasted_iota(jnp.int32, (1, LANES), 1)
        kstat = jnp.max(kstat_ref[...], axis=0, keepdims=True)
        kmax2 = jnp.max(jnp.where(lane // HEAD_DIM == hk, kstat, 0.0), axis=1, keepdims=True)
        qf = q.astype(F32)
        qq = _scores(jnp.ones((8, HEAD_DIM), BF16), (qf * qf).astype(BF16))
        off = jnp.sqrt(qq[0:1] * kmax2)

    l8 = jnp.zeros((8, nq), F32)
    acc = jnp.zeros((HEAD_DIM, nq), F32)
    for c in range(s_len // tk):
        k_blk = k_ref[c * tk:(c + 1) * tk, :]
        vt_blk = vt_ref[:, c * tk:(c + 1) * tk]
        p = jnp.exp2(_scores(k_blk, q) - off)
        l8 = l8 + jnp.sum(p.reshape(tk // 8, 8, nq), axis=0)
        acc = acc + _dot(vt_blk, p.astype(BF16))
    out = acc / jnp.sum(l8, axis=0, keepdims=True)
    out_t = jnp.concatenate([out[:, g * tq:(g + 1) * tq] for g in range(heads)], axis=0)
    o_ref[...] = out_t.T.astype(o_ref.dtype)


def _attn_call(q, k, vt, kstat, token_start, n_seq, s_len, tq, tk, tt, exact_max):
    heads_per_step = 4
    q_cols = heads_per_step * HEAD_DIM
    steps_per_kv = (N_Q_HEADS // N_KV_HEADS) // heads_per_step
    q_blk0 = token_start // tq
    seq0 = token_start // s_len
    nq = s_len // tq
    stat_rows = s_len // tt * STAT_ROWS

    def q_idx(b, hk, gp, j):
        return (q_blk0 + b * nq + j, hk * steps_per_kv + gp)

    return pl.pallas_call(
        functools.partial(_attn_kernel, tk=tk, exact_max=exact_max),
        grid=(n_seq, N_KV_HEADS, steps_per_kv, nq),
        in_specs=[
            pl.BlockSpec((tq, q_cols), q_idx),
            pl.BlockSpec((None, s_len, HEAD_DIM), lambda b, hk, gp, j: (hk, seq0 + b, 0)),
            pl.BlockSpec((HEAD_DIM, s_len), lambda b, hk, gp, j: (hk, seq0 + b)),
            pl.BlockSpec((stat_rows, KV_WIDTH), lambda b, hk, gp, j: (seq0 + b, 0)),
        ],
        out_specs=pl.BlockSpec((tq, q_cols),
                               lambda b, hk, gp, j: (b * nq + j, hk * steps_per_kv + gp)),
        out_shape=jax.ShapeDtypeStruct((n_seq * s_len, ATTN_WIDTH), BF16),
        compiler_params=_compiler_params(4),
        name="attn_exact_max" if exact_max else "attn",
    )(q, k, vt, kstat)


def _group_partner(x, row, d):
    pos = row % EXPERTS_PER_GROUP
    return jnp.where(pos + d < EXPERTS_PER_GROUP,
                     pltpu.roll(x, N_EXPERTS - d, 0),
                     pltpu.roll(x, EXPERTS_PER_GROUP - d, 0))


def _route(scores, bias, row):
    biased = scores + bias
    rank = jnp.zeros(biased.shape, jnp.int32)
    pos = row % EXPERTS_PER_GROUP
    for d in range(1, EXPERTS_PER_GROUP):
        other = _group_partner(biased, row, d)
        other_is_lower = (pos + d) >= EXPERTS_PER_GROUP
        beats = (other > biased) | ((other == biased) & other_is_lower)
        rank = rank + beats.astype(jnp.int32)
    top2 = rank < 2
    contrib = jnp.where(top2, biased, 0.0)
    group_score = contrib
    for d in range(1, EXPERTS_PER_GROUP):
        group_score = group_score + _group_partner(contrib, row, d)
    n_groups = N_EXPERTS // EXPERTS_PER_GROUP
    gpos = row // EXPERTS_PER_GROUP
    losses = jnp.zeros(biased.shape, jnp.int32)
    for d in range(1, n_groups):
        other = pltpu.roll(group_score, N_EXPERTS - d * EXPERTS_PER_GROUP, 0)
        other_is_lower = (gpos + d) >= n_groups
        beats = (other > group_score) | ((other == group_score) & other_is_lower)
        losses = losses + beats.astype(jnp.int32)
    selected = top2 & (losses == 0)
    w = jnp.where(selected, scores, 0.0)
    return w / jnp.sum(w, axis=0, keepdims=True), selected


INFO_EXPERT, INFO_RANK, INFO_GATE, INFO_ROWS = 0, 2, 4, 8


def _post_kernel(l_ref, x_ref, *refs, tt, group_first_step):
    del l_ref
    n_groups = len(group_first_step)
    attn_refs = refs[:n_groups]
    (gm_ref, aon_ref, wo_ref, g1_ref, b1_ref, wr_ref, rb_ref,
     x1_ref, info_ref, cnt_ref) = refs[n_groups:]
    ts = x_ref.shape[0]
    attn = attn_refs[0][...]
    for g in range(1, n_groups):
        attn = jnp.where(pl.program_id(0) >= group_first_step[g], attn_refs[g][...], attn)
    a = _rms_norm(attn.astype(F32), aon_ref[...]).astype(BF16)
    mix = _dot(a, wo_ref[:ATTN_WIDTH, :]) + _dot(gm_ref[...], wo_ref[ATTN_WIDTH:, :])
    x1 = _layer_norm(DEEPNORM_ALPHA * x_ref[...] + mix, g1_ref[...], b1_ref[...])
    x1_ref[...] = x1
    x_hi = x1.astype(BF16)
    x_lo = (x1 - x_hi.astype(F32)).astype(BF16)
    by_hi = _scores(wr_ref[...], x_hi)
    logits = (by_hi[:N_EXPERTS] + by_hi[N_EXPERTS:]
              + _scores(wr_ref[:N_EXPERTS, :], x_lo))
    row = lax.broadcasted_iota(jnp.int32, (N_EXPERTS, 1), 0)
    gates, selected = _route(jax.nn.sigmoid(logits), rb_ref[...], row)

    sel = jnp.where(selected, 1.0, 0.0)
    earlier = lax.broadcasted_iota(jnp.int32, (ts, ts), 0)
    later = lax.broadcasted_iota(jnp.int32, (ts, ts), 1)
    prefix = jnp.where((earlier <= later) & (earlier // tt == later // tt), 1.0, 0.0).astype(BF16)
    incl = _dot(sel.astype(BF16), prefix)
    rank = incl - sel
    for t in range(ts // tt):
        last = (t + 1) * tt - 1
        cnt_ref[t * N_EXPERTS:(t + 1) * N_EXPERTS] = jnp.broadcast_to(
            incl[:, last:last + 1], (N_EXPERTS, LANES))

    row_f = row.astype(F32)
    e_lo = jnp.min(jnp.where(selected, row_f, float(N_EXPERTS)), axis=0, keepdims=True)
    e_hi = jnp.max(jnp.where(selected, row_f, -1.0), axis=0, keepdims=True)

    def pick(mat, e):
        return jnp.sum(jnp.where(row_f == e, mat, 0.0), axis=0, keepdims=True)

    fields = {INFO_EXPERT: e_lo, INFO_EXPERT + 1: e_hi,
              INFO_RANK: pick(rank, e_lo), INFO_RANK + 1: pick(rank, e_hi),
              INFO_GATE: pick(gates, e_lo), INFO_GATE + 1: pick(gates, e_hi)}
    info_row = lax.broadcasted_iota(jnp.int32, (INFO_ROWS, 1), 0)
    info = jnp.zeros((INFO_ROWS, ts), F32)
    for r, val in fields.items():
        info = jnp.where(info_row == r, val, info)
    info_ref[...] = info


def _post_call(layer, x, attn_groups, gm, p, tt, tiles_per_step):
    n = x.shape[0]
    ts = tt * tiles_per_step
    group_first_step, first = [], 0
    for a in attn_groups:
        group_first_step.append(first)
        first += a.shape[0] // ts

    def tok(i, l):
        return (i, 0)

    def group_block(first_step, n_steps):
        return lambda i, l: (jnp.clip(i - first_step, 0, n_steps - 1), 0)

    def lay3(i, l):
        return (l[0], 0, 0)

    def const2(i, l):
        return (0, 0)

    grid_spec = pltpu.PrefetchScalarGridSpec(
        num_scalar_prefetch=1,
        grid=(n // ts,),
        in_specs=[
            pl.BlockSpec((ts, D_MODEL), tok),
            *[pl.BlockSpec((ts, ATTN_WIDTH), group_block(f, a.shape[0] // ts))
              for f, a in zip(group_first_step, attn_groups)],
            pl.BlockSpec((ts, GMLP_WIDTH), tok),
            pl.BlockSpec((None, 1, ATTN_WIDTH), lay3),
            pl.BlockSpec((None, D_MODEL, D_MODEL), lay3),
            pl.BlockSpec((None, 1, D_MODEL), lay3),
            pl.BlockSpec((None, 1, D_MODEL), lay3),
            pl.BlockSpec((2 * N_EXPERTS, D_MODEL), const2),
            pl.BlockSpec((N_EXPERTS, 1), const2),
        ],
        out_specs=[pl.BlockSpec((ts, D_MODEL), tok),
                   pl.BlockSpec((INFO_ROWS, ts), lambda i, l: (0, i)),
                   pl.BlockSpec((tiles_per_step * N_EXPERTS, LANES), tok)],
    )
    return pl.pallas_call(
        functools.partial(_post_kernel, tt=tt, group_first_step=tuple(group_first_step)),
        grid_spec=grid_spec,
        out_shape=[jax.ShapeDtypeStruct((n, D_MODEL), F32),
                   jax.ShapeDtypeStruct((INFO_ROWS, n), F32),
                   jax.ShapeDtypeStruct((n // tt * N_EXPERTS, LANES), F32)],
        compiler_params=_compiler_params(1),
        name="post",
    )(layer, x, *attn_groups, gm, p["attn_out_norm"], p["w_out"], p["ln1_g"], p["ln1_b"],
      p["wr"], p["router_bias"])


RUN = 8


def _local_rows(tt):
    return -(-(2 * tt + N_EXPERTS * (RUN - 1)) // LANES) * LANES


def _plan_routes(info, cnt, tt, tm):
    n = info.shape[1]
    n_tiles = n // tt
    expert = info[INFO_EXPERT:INFO_EXPERT + 2].T.astype(jnp.int32)
    rank = info[INFO_RANK:INFO_RANK + 2].T.astype(jnp.int32)
    counts = cnt.reshape(n_tiles, N_EXPERTS, LANES)[:, :, 0].astype(jnp.int32)
    run_rows = (counts + RUN - 1) // RUN * RUN
    local_off = jnp.cumsum(run_rows, axis=1) - run_rows
    totals = jnp.sum(run_rows, axis=0)
    seg_tiles = (totals + tm - 1) // tm
    seg_end = jnp.cumsum(seg_tiles)
    expert_row0 = (seg_end - seg_tiles) * tm
    tile_base = expert_row0[None, :] + jnp.cumsum(run_rows, axis=0) - run_rows
    is_expert = expert.reshape(n_tiles, tt, 2, 1) == jnp.arange(N_EXPERTS, dtype=jnp.int32)

    def per_slot(table):
        return jnp.sum(jnp.where(is_expert, table[:, None, None, :], 0), axis=-1)

    rank3 = rank.reshape(n_tiles, tt, 2)
    pos = per_slot(tile_base) + rank3
    lpos = per_slot(local_off) + rank3
    units = run_rows // RUN
    unit_end = jnp.cumsum(units, axis=1)
    u = jnp.arange(2 * tt // RUN + N_EXPERTS, dtype=jnp.int32)
    unit_expert = jnp.minimum(
        jnp.sum((u[None, :, None] >= unit_end[:, None, :]).astype(jnp.int32), axis=-1),
        N_EXPERTS - 1)
    is_unit_expert = unit_expert[:, :, None] == jnp.arange(N_EXPERTS, dtype=jnp.int32)

    def per_unit(table):
        return jnp.sum(jnp.where(is_unit_expert, table[:, None, :], 0), axis=-1)

    within = (u[None, :] - per_unit(unit_end - units)) * RUN
    unit_src = per_unit(local_off) + within
    unit_dst = per_unit(tile_base) + within
    n_row_tiles = (2 * n + n_tiles * N_EXPERTS * (RUN - 1)) // tm + N_EXPERTS + 1
    used = seg_end[-1]
    t_idx = jnp.arange(n_row_tiles, dtype=jnp.int32)
    tile_blk = jnp.minimum(t_idx, used - 1)
    tile_expert = jnp.minimum(
        jnp.sum((tile_blk[:, None] >= seg_end[None, :]).astype(jnp.int32), axis=1), N_EXPERTS - 1)
    return {
        "pos": pos.reshape(n_tiles, 1, 2 * tt).astype(jnp.int32),
        "lpos": lpos.transpose(0, 2, 1).astype(jnp.int32),
        "units": jnp.concatenate([unit_src, unit_dst], axis=1)[:, None, :].astype(jnp.int32),
        "n_units": unit_end[:, -1].astype(jnp.int32),
        "gates": info[INFO_GATE:INFO_GATE + 2].T,
        "pads": jnp.stack([jnp.append(expert_row0 + totals, used),
                           jnp.append((seg_tiles * tm - totals) // RUN,
                                      n_row_tiles - used)]).astype(jnp.int32),
        "tile_expert": tile_expert.astype(jnp.int32),
        "tile_valid": (t_idx < used).astype(jnp.int32),
        "tile_blk": tile_blk.astype(jnp.int32),
        "n_rows": n_row_tiles * tm,
    }


ROW_SLAB = 8
assert ROW_SLAB * LANES == D_MODEL


def _slab_copy(src_ref, src_row, dst_ref, dst_row, sem):
    src = src_ref.at[pl.ds(pl.multiple_of(src_row * ROW_SLAB, ROW_SLAB), ROW_SLAB)]
    dst = dst_ref.at[pl.ds(pl.multiple_of(dst_row * ROW_SLAB, ROW_SLAB), ROW_SLAB)]
    return pltpu.make_async_copy(src, dst, sem)


def _rows_to_slabs(x, slab_ref):
    rows = x.shape[0]
    for s in range(ROW_SLAB):
        slab_ref[pl.ds(s, rows, stride=ROW_SLAB), :] = x[:, s * LANES:(s + 1) * LANES]


def _slabs_to_rows(slab_ref, rows):
    return jnp.concatenate(
        [slab_ref[pl.ds(s, rows, stride=ROW_SLAB), :] for s in range(ROW_SLAB)], axis=1)


DMA_UNROLL = 8


PACKED_WIDTH = D_MODEL // 2
HIGH_HALF = 0xFFFF0000


def _pack_bf16_pairs(x):
    lo = pltpu.bitcast(x[:, :PACKED_WIDTH], jnp.uint32)
    hi = pltpu.bitcast(x[:, PACKED_WIDTH:], jnp.uint32)
    return (lo >> 16) | (hi & jnp.uint32(HIGH_HALF))


def _unpack_bf16_pairs(u):
    lo = pltpu.bitcast(u << 16, F32)
    hi = pltpu.bitcast(u & jnp.uint32(HIGH_HALF), F32)
    return jnp.concatenate([lo, hi], axis=1).astype(BF16)


def _scatter_kernel(pads_ref, n_units_ref, units_ref, lpos_ref, x_ref, xs_ref, loc_ref, sems,
                    *, tm):
    i = pl.program_id(0)
    last = pl.num_programs(0) - 1
    slot = i % 2
    n_local = loc_ref.shape[1]

    def unit_copy(src_ref, src_row, dst_row, sem):
        return pltpu.make_async_copy(
            src_ref.at[pl.ds(pl.multiple_of(src_row, RUN), RUN)],
            xs_ref.at[pl.ds(pl.multiple_of(dst_row, RUN), RUN)], sem)

    @pl.when(i == 0)
    def _():
        zeros = loc_ref.at[1]
        zeros[...] = jnp.zeros(zeros.shape, zeros.dtype)
        for e in range(N_EXPERTS):
            start, units = pads_ref[0, e], pads_ref[1, e]

            def fill(j, c, start=start):
                unit_copy(zeros, 0, start + j * RUN, sems.at[2]).start()
                return c

            def fill_wait(j, c):
                unit_copy(zeros, 0, 0, sems.at[2]).wait()
                return c

            lax.fori_loop(0, units, fill, 0)
            lax.fori_loop(0, units, fill_wait, 0)

        first_tail_tile, n_tail_tiles = pads_ref[0, N_EXPERTS], pads_ref[1, N_EXPERTS]

        def tail_copy(j):
            row0 = pl.multiple_of((first_tail_tile + j) * tm, tm)
            return pltpu.make_async_copy(zeros.at[pl.ds(0, tm)], xs_ref.at[pl.ds(row0, tm)],
                                         sems.at[2])

        def tail_fill(j, c):
            tail_copy(j).start()
            return c

        def tail_wait(j, c):
            tail_copy(j).wait()
            return c

        lax.fori_loop(0, n_tail_tiles, tail_fill, 0)
        lax.fori_loop(0, n_tail_tiles, tail_wait, 0)

    lpos = lpos_ref[...]
    row = lax.broadcasted_iota(jnp.int32, (n_local, 1), 0)
    onehot = jnp.where((row == lpos[0:1, :]) | (row == lpos[1:2, :]), 1.0, 0.0).astype(BF16)
    sorted_rows = _dot(onehot, x_ref[...].astype(BF16))
    loc_ref[slot] = _pack_bf16_pairs(sorted_rows)

    max_units = units_ref.shape[1] // 2

    def for_each_unit(n_units, fn):
        full = n_units // DMA_UNROLL

        def block(k, c):
            for j in range(DMA_UNROLL):
                fn(k * DMA_UNROLL + j)
            return c

        def single(u, c):
            fn(u)
            return c

        lax.fori_loop(0, full, block, 0)
        lax.fori_loop(full * DMA_UNROLL, n_units, single, 0)

    def start_unit(u):
        unit_copy(loc_ref.at[slot], units_ref[0, u], units_ref[0, max_units + u],
                  sems.at[slot]).start()

    for_each_unit(n_units_ref[i], start_unit)

    def drain(which, tile):
        for_each_unit(n_units_ref[tile],
                      lambda u: unit_copy(loc_ref.at[which], 0, 0, sems.at[which]).wait())

    @pl.when(i > 0)
    def _():
        drain(1 - slot, i - 1)

    @pl.when(i == last)
    def _():
        drain(slot, i)


def _scatter_call(x1, plan, tt, tm):
    n = x1.shape[0]
    grid_spec = pltpu.PrefetchScalarGridSpec(
        num_scalar_prefetch=2,
        grid=(n // tt,),
        in_specs=[
            pl.BlockSpec((None, 1, plan["units"].shape[2]), lambda i, pads, nu: (i, 0, 0),
                         memory_space=pltpu.SMEM),
            pl.BlockSpec((None, 2, tt), lambda i, pads, nu: (i, 0, 0)),
            pl.BlockSpec((tt, D_MODEL), lambda i, pads, nu: (i, 0)),
        ],
        out_specs=pl.BlockSpec(memory_space=pl.ANY),
        scratch_shapes=[pltpu.VMEM((2, max(_local_rows(tt), tm), PACKED_WIDTH), jnp.uint32),
                        pltpu.SemaphoreType.DMA((3,))],
    )
    return pl.pallas_call(
        functools.partial(_scatter_kernel, tm=tm),
        grid_spec=grid_spec,
        out_shape=jax.ShapeDtypeStruct((plan["n_rows"], PACKED_WIDTH), jnp.uint32),
        compiler_params=_compiler_params(1),
        name="scatter",
    )(plan["pads"], plan["n_units"], plan["units"], plan["lpos"], x1)


def _expert_kernel(l_ref, te_ref, tv_ref, tb_ref, xs_ref, wg_ref, wu_ref, wd_ref, o_ref):
    del l_ref, te_ref, tb_ref
    valid = tv_ref[pl.program_id(0)] != 0

    @pl.when(valid)
    def _():
        xb = _unpack_bf16_pairs(xs_ref[...])
        hidden = jax.nn.silu(_dot(xb, wg_ref[...])) * _dot(xb, wu_ref[...])
        _rows_to_slabs(_dot(hidden.astype(BF16), wd_ref[...]), o_ref)

    @pl.when(jnp.logical_not(valid))
    def _():
        o_ref[...] = jnp.zeros_like(o_ref)


def _expert_call(layer, xs, plan, p, tm):
    n_rows = xs.shape[0]

    def rows(t, l, te, tv, tb):
        return (tb[t], 0)

    def weights(t, l, te, tv, tb):
        return (l[0], te[t], 0, 0)

    grid_spec = pltpu.PrefetchScalarGridSpec(
        num_scalar_prefetch=4,
        grid=(n_rows // tm,),
        in_specs=[
            pl.BlockSpec((tm, PACKED_WIDTH), rows),
            pl.BlockSpec((None, None, D_MODEL, EXPERT_FF), weights),
            pl.BlockSpec((None, None, D_MODEL, EXPERT_FF), weights),
            pl.BlockSpec((None, None, EXPERT_FF, D_MODEL), weights),
        ],
        out_specs=pl.BlockSpec((tm * ROW_SLAB, LANES), lambda t, l, te, tv, tb: (t, 0)),
    )
    return pl.pallas_call(
        _expert_kernel,
        grid_spec=grid_spec,
        out_shape=jax.ShapeDtypeStruct((n_rows * ROW_SLAB, LANES), F32),
        compiler_params=_compiler_params(1),
        name="experts",
    )(layer, plan["tile_expert"], plan["tile_valid"], plan["tile_blk"], xs,
      p["w_gate"], p["w_up"], p["w_down"])


def _combine_kernel(l_ref, pos_ref, pos_next_ref, x_ref, gate_ref, os_ref, g2_ref, b2_ref,
                    o_ref, buf_ref, sems):
    del l_ref
    tt = x_ref.shape[0]
    i = pl.program_id(0)
    slot = i % 2

    def gather(p_ref, which):
        def body(r, c):
            for s in range(2):
                _slab_copy(os_ref, p_ref[0, 2 * r + s], buf_ref.at[which, s], r,
                           sems.at[which]).start(priority=s)
            return c
        lax.fori_loop(0, tt, body, 0, unroll=DMA_UNROLL)

    @pl.when(i == 0)
    def _():
        gather(pos_ref, 0)

    @pl.when(i + 1 < pl.num_programs(0))
    def _():
        gather(pos_next_ref, 1 - slot)

    def drain(r, c):
        for s in range(2):
            _slab_copy(os_ref, 0, buf_ref.at[slot, s], 0, sems.at[slot]).wait()
        return c

    lax.fori_loop(0, tt, drain, 0, unroll=DMA_UNROLL)
    gate = gate_ref[...]
    y = (gate[:, 0:1] * _slabs_to_rows(buf_ref.at[slot, 0], tt)
         + gate[:, 1:2] * _slabs_to_rows(buf_ref.at[slot, 1], tt))
    o_ref[...] = _layer_norm(DEEPNORM_ALPHA * x_ref[...] + y, g2_ref[...], b2_ref[...])


def _combine_call(layer, x1, out_sorted, plan, p, tt):
    n = x1.shape[0]

    def tok(i, l):
        return (i, 0)

    def lay3(i, l):
        return (l[0], 0, 0)

    n_tiles = n // tt
    grid_spec = pltpu.PrefetchScalarGridSpec(
        num_scalar_prefetch=1,
        grid=(n_tiles,),
        in_specs=[
            pl.BlockSpec((None, 1, 2 * tt), lambda i, l: (i, 0, 0), memory_space=pltpu.SMEM),
            pl.BlockSpec((None, 1, 2 * tt), lambda i, l: (jnp.minimum(i + 1, n_tiles - 1), 0, 0),
                         memory_space=pltpu.SMEM),
            pl.BlockSpec((tt, D_MODEL), tok),
            pl.BlockSpec((tt, 2), tok),
            pl.BlockSpec(memory_space=pl.ANY),
            pl.BlockSpec((None, 1, D_MODEL), lay3),
            pl.BlockSpec((None, 1, D_MODEL), lay3),
        ],
        out_specs=pl.BlockSpec((tt, D_MODEL), tok),
        scratch_shapes=[pltpu.VMEM((2, 2, tt * ROW_SLAB, LANES), F32),
                        pltpu.SemaphoreType.DMA((2,))],
    )
    return pl.pallas_call(
        _combine_kernel,
        grid_spec=grid_spec,
        out_shape=jax.ShapeDtypeStruct((n, D_MODEL), F32),
        compiler_params=_compiler_params(1),
        name="combine",
    )(layer, plan["pos"], plan["pos"], x1, plan["gates"], out_sorted, p["ln2_g"], p["ln2_b"])


def _rope_tables(max_len):
    t = jnp.arange(max_len, dtype=jnp.int32)
    row = (t // GRID_W).astype(F32)
    col = (t % GRID_W).astype(F32)
    n_pairs_axis = HEAD_DIM // 4
    inv_freq = 1.0 / (ROPE_THETA ** (jnp.arange(n_pairs_axis, dtype=F32) / n_pairs_axis))
    ang = jnp.concatenate([row[:, None] * inv_freq, col[:, None] * inv_freq], -1)
    cos, sin = jnp.cos(ang), jnp.sin(ang)
    reps = LANES // HEAD_DIM
    cos128 = jnp.tile(jnp.concatenate([cos, cos], -1), (1, reps))
    sin128 = jnp.tile(jnp.concatenate([-sin, sin], -1), (1, reps))
    return cos128, sin128


def _prepare_params(max_len, w_in, q_norm, k_norm, gmlp_ln_g, gmlp_ln_b, w_spatial, b_spatial,
                    attn_out_norm, gmlp_out_norm, w_out, ln1_g, ln1_b, w_router, router_bias,
                    w_gate, w_up, w_down, ln2_g, ln2_b):
    depth = w_in.shape[0]
    perm64 = jnp.concatenate([jnp.arange(0, HEAD_DIM, 2), jnp.arange(1, HEAD_DIM, 2)])
    n_rot_heads = N_Q_HEADS + N_KV_HEADS
    rot_cols = (jnp.arange(n_rot_heads)[:, None] * HEAD_DIM + perm64[None, :]).reshape(-1)
    cols = jnp.concatenate([rot_cols, jnp.arange(n_rot_heads * HEAD_DIM, IN_WIDTH)])
    cos128, sin128 = _rope_tables(max_len)
    head_of_lane = jnp.arange(ATTN_WIDTH) // HEAD_DIM
    row3 = lambda a: a.reshape(depth, 1, -1).astype(F32)
    wr_t = w_router.astype(F32).T
    wr_hi = wr_t.astype(BF16)
    max_offset = (1.01 * LOG2_E * HEAD_DIM ** 0.5
                  * jnp.max(jnp.abs(q_norm), axis=1) * jnp.max(jnp.abs(k_norm), axis=1))
    return {
        "attn_needs_exact_max": max_offset > MAX_SAFE_OFFSET,
        "w_in": w_in[:, :, cols].astype(BF16),
        "cos": cos128,
        "sin": sin128,
        "seg": (head_of_lane[:, None] == head_of_lane[None, :]).astype(BF16),
        "q_norm": row3(jnp.tile(q_norm[:, perm64], (1, N_Q_HEADS))),
        "k_norm": row3(jnp.tile(k_norm[:, perm64], (1, N_KV_HEADS))),
        "gmlp_ln_g": row3(gmlp_ln_g),
        "gmlp_ln_b": row3(gmlp_ln_b),
        "w_cat": w_spatial.transpose(0, 2, 1, 3).reshape(depth, CHUNK, N_GMLP_HEADS * CHUNK).astype(BF16),
        "b_sp": jnp.repeat(b_spatial.transpose(0, 2, 1), GMLP_HEAD_DIM, axis=2).astype(F32),
        "attn_out_norm": row3(attn_out_norm),
        "gmlp_out_norm": row3(gmlp_out_norm),
        "w_out": w_out.astype(BF16),
        "ln1_g": row3(ln1_g),
        "ln1_b": row3(ln1_b),
        "wr": jnp.concatenate([wr_hi, (wr_t - wr_hi.astype(F32)).astype(BF16)], axis=0),
        "router_bias": router_bias.astype(F32).reshape(N_EXPERTS, 1),
        "w_gate": w_gate.astype(BF16),
        "w_up": w_up.astype(BF16),
        "w_down": w_down.astype(BF16),
        "ln2_g": row3(ln2_g),
        "ln2_b": row3(ln2_b),
    }


def _trunk(x_tokens, groups, p, depth, tp, tt, tm, tq, tk):
    (_, n_seq0, s0), = groups[:1]
    n_prompt_blocks = (n_seq0 * s0) // tp
    prompt_period = s0 // tp
    sample_period = (groups[1][2] if len(groups) > 1 else s0) // tp
    periods = (n_prompt_blocks, prompt_period, sample_period)

    def layer_body(l, x):
        layer = jnp.reshape(l, (1,)).astype(jnp.int32)
        q, k, vt, gm, kstat = _proj_call(layer, x, p, tp, periods)

        def attention(exact_max):
            return tuple(_attn_call(q, k, vt, kstat, start, n_seq, s_len, min(tq, s_len),
                                    min(tk, s_len), tp, exact_max)
                         for (start, n_seq, s_len) in groups)

        attn = lax.cond(p["attn_needs_exact_max"][l],
                        functools.partial(attention, True), functools.partial(attention, False))
        x1, info, cnt = _post_call(layer, x, attn, gm, p, tt, tiles_per_step=2)
        plan = _plan_routes(info, cnt, tt, tm)
        xs = _scatter_call(x1, plan, tt, tm)
        out_sorted = _expert_call(layer, xs, plan, p, tm)
        return _combine_call(layer, x1, out_sorted, plan, p, tt)

    return lax.fori_loop(0, depth, layer_body, x_tokens)


def kernel(x_prompt, x_sample, w_in, q_norm, k_norm, gmlp_ln_g, gmlp_ln_b, w_spatial, b_spatial,
           attn_out_norm, gmlp_out_norm, w_out, ln1_g, ln1_b, w_router, router_bias,
           w_gate, w_up, w_down, ln2_g, ln2_b):
    bp, sp, d = x_prompt.shape
    bs, ss, _ = x_sample.shape
    assert sp % ss == 0 and (bp * sp) % ss == 0, "sample sequences must tile the prompt stream"
    p = _prepare_params(max(sp, ss), w_in, q_norm, k_norm, gmlp_ln_g, gmlp_ln_b, w_spatial,
                        b_spatial, attn_out_norm, gmlp_out_norm, w_out, ln1_g, ln1_b, w_router,
                        router_bias, w_gate, w_up, w_down, ln2_g, ln2_b)
    x = jnp.concatenate([x_prompt.reshape(bp * sp, d), x_sample.reshape(bs * ss, d)], axis=0)
    groups = ((0, bp, sp), (bp * sp, bs, ss))
    y = _trunk(x.astype(F32), groups, p, w_in.shape[0], tp=min(1024, ss), tt=256, tm=512,
               tq=256, tk=512)
    return (y[:bp * sp].reshape(bp, sp, d), y[bp * sp:].reshape(bs, ss, d))
```
